```python
import math
import jax, jax.numpy as jnp
from jax import lax
import numpy as np

D_MODEL = 1024
BATCH = 4
SEQ = 8192
DEPTH = 4

MEM_LEN = 256
N_MIXERS = 3
MIX_W = 3 * D_MODEL // 4
XATTN_HEADS = 4
XATTN_W = D_MODEL - MIX_W
XATTN_HD = XATTN_W // XATTN_HEADS
DA_HD = 64
DA_HEADS = MIX_W // (2 * DA_HD)
Q_BLOCK = 128
N_BUCKETS = 32
MAX_DISTANCE = 128
HG_DK = 128
HG_HEADS = MIX_W // HG_DK
HG_DV = MIX_W // HG_HEADS
HG_CHUNK = 64
CONV_W = 3
D_FF = ((8 * D_MODEL // 3 + 255) // 256) * 256
LN_EPS = 1e-5
ALPHA = (2 * DEPTH) ** 0.25
BETA = (8 * DEPTH) ** -0.25
N_ATTN = (DEPTH + 2) // 3
N_HGRN = (DEPTH + 1) // 3
N_CONV = DEPTH // 3
ATTN_IN = 3 * MIX_W + XATTN_W
HGRN_IN = 5 * MIX_W + XATTN_W
CONV_IN = 3 * MIX_W + XATTN_W

kernel_name = 'hybrid_diffattn_hgrn2_shortconv_encoder'


def _layer_norm(x, g, b):
    xf = x.astype(jnp.float32)
    xc = xf - jnp.mean(xf, -1, keepdims=True)
    var = jnp.mean(xc * xc, -1, keepdims=True)
    y = xc * lax.rsqrt(var + LN_EPS) * g.astype(jnp.float32) + b.astype(jnp.float32)
    return y.astype(x.dtype)


def _rms_norm(x, g):
    xf = x.astype(jnp.float32)
    y = xf * lax.rsqrt(jnp.mean(xf * xf, -1, keepdims=True) + LN_EPS)
    return (y * g.astype(jnp.float32)).astype(x.dtype)


def _dwconv3(x, w):
    xp = jnp.pad(x, ((0, 0), (1, 1), (0, 0)))
    return xp[:, :-2] * w[0] + xp[:, 1:-1] * w[1] + xp[:, 2:] * w[2]


def _t5_bucket(rel):
    half = N_BUCKETS // 2
    max_exact = half // 2
    ret = jnp.where(rel > 0, half, 0)
    n = jnp.abs(rel)
    nf = jnp.maximum(n, 1).astype(jnp.float32)
    large = max_exact + (jnp.log(nf / max_exact) / math.log(MAX_DISTANCE / max_exact)
                         * (half - max_exact)).astype(jnp.int32)
    large = jnp.minimum(large, half - 1)
    return ret + jnp.where(n < max_exact, n, large)


def _diff_attention(p, lam_params, subln, rel_bias, layer):
    b, s, _ = p.shape
    q, k, v = jnp.split(p, 3, axis=-1)
    q = q.reshape(b, s, DA_HEADS, 2, DA_HD) * (DA_HD ** -0.5)
    k = k.reshape(b, s, DA_HEADS, 2, DA_HD)
    v = v.reshape(b, s, DA_HEADS, 2 * DA_HD)
    lam_init = 0.8 - 0.6 * math.exp(-0.3 * layer)
    lp = lam_params.astype(jnp.float32)
    lam = jnp.exp(jnp.sum(lp[0] * lp[1])) - jnp.exp(jnp.sum(lp[2] * lp[3])) + lam_init
    n_blk = s // Q_BLOCK
    q_blocks = q.reshape(b, n_blk, Q_BLOCK, DA_HEADS, 2, DA_HD).transpose(1, 0, 2, 3, 4, 5)
    k_pos = jnp.arange(s, dtype=jnp.int32)
    table = rel_bias.astype(jnp.float32)

    def block(args):
        qb, blk = args
        q_pos = blk * Q_BLOCK + jnp.arange(Q_BLOCK, dtype=jnp.int32)
        bias = table[_t5_bucket(k_pos[None, :] - q_pos[:, None])].transpose(2, 0, 1)
        logits = jnp.einsum('bqhmd,bkhmd->bmhqk', qb, k).astype(jnp.float32) + bias
        probs = jax.nn.softmax(logits, axis=-1)
        w = (probs[:, 0] - lam * probs[:, 1]).astype(v.dtype)
        return jnp.einsum('bhqk,bkhe->bqhe', w, v)

    o = lax.map(block, (q_blocks, jnp.arange(n_blk, dtype=jnp.int32)))
    o = o.transpose(1, 0, 2, 3, 4).reshape(b, s, DA_HEADS, 2 * DA_HD)
    o = _rms_norm(o, subln) * (1.0 - lam_init)
    return o.reshape(b, s, MIX_W)


def _hgrn2(p, lower_bounds, norm_w, layer):
    b, s, _ = p.shape
    f32 = jnp.float32
    q, i, g, f_fw, f_bw = jnp.split(p, 5, axis=-1)
    lbw = jax.nn.softmax(lower_bounds.astype(f32), axis=0)
    lb = (jnp.cumsum(lbw, axis=0) - lbw[0])[layer][:, None, None, :]
    z = jnp.stack([f_fw, jnp.flip(f_bw, 1)]).astype(f32)
    log_f = jnp.logaddexp(jnp.log(lb), jnp.log1p(-lb) + jax.nn.log_sigmoid(z))
    k = -jnp.expm1(log_f)
    qd = jax.nn.silu(jnp.stack([q, jnp.flip(q, 1)]).astype(f32)) * (HG_DK ** -0.5)
    vd = jnp.stack([i, jnp.flip(i, 1)]).astype(f32)

    def to_chunks(t, d):
        return t.reshape(2 * b, s // HG_CHUNK, HG_CHUNK, HG_HEADS, d).transpose(1, 0, 3, 2, 4)

    mask = jnp.tril(jnp.ones((HG_CHUNK, HG_CHUNK), dtype=bool))

    def step(state, xs):
        qc, kc, vc, gc = xs
        cum = jnp.cumsum(gc, axis=2)
        rel = jnp.where(mask[:, :, None], cum[:, :, :, None, :] - cum[:, :, None, :, :], -jnp.inf)
        attn = jnp.einsum('nhtk,nhtsk,nhsk->nhts', qc, jnp.exp(rel), kc)
        o = attn @ vc + jnp.einsum('nhtk,nhkv->nhtv', qc * jnp.exp(cum), state)
        last = cum[:, :, -1:, :]
        state = (jnp.exp(last[:, :, 0, :])[..., None] * state
                 + jnp.einsum('nhsk,nhsv->nhkv', kc * jnp.exp(last - cum), vc))
        return state, o

    state0 = jnp.zeros((2 * b, HG_HEADS, HG_DK, HG_DV), f32)
    _, o = lax.scan(step, state0, (to_chunks(qd, HG_DK), to_chunks(k, HG_DK),
                                   to_chunks(vd, HG_DV), to_chunks(log_f, HG_DK)))
    o = o.transpose(1, 0, 3, 2, 4).reshape(2, b, s, HG_HEADS, HG_DV)
    o = o[0] + jnp.flip(o[1], 1)
    o = _rms_norm(o, norm_w.reshape(HG_HEADS, HG_DV)).reshape(b, s, MIX_W)
    return (o * jax.nn.silu(g.astype(f32))).astype(p.dtype)


def _short_conv(p, conv_w):
    gb, gc, h = jnp.split(p, 3, axis=-1)
    return gb * _dwconv3(gc * h, conv_w)


def _memory_attention(q, mem, w_kv):
    b, s, _ = q.shape
    m = mem.shape[1]
    km, vm = jnp.split(mem @ w_kv, 2, axis=-1)
    q = q.reshape(b, s, XATTN_HEADS, XATTN_HD) * (XATTN_HD ** -0.5)
    km = km.reshape(b, m, XATTN_HEADS, XATTN_HD)
    vm = vm.reshape(b, m, XATTN_HEADS, XATTN_HD)
    logits = jnp.einsum('bshd,bmhd->bhsm', q, km).astype(jnp.float32)
    probs = jax.nn.softmax(logits, axis=-1).astype(vm.dtype)
    return jnp.einsum('bhsm,bmhd->bshd', probs, vm).reshape(b, s, XATTN_W)


def _conv_ffn(x, w_up, w_conv, w_down):
    h = _dwconv3(x @ w_up, w_conv)
    a, v = jnp.split(h, 2, axis=-1)
    return (jax.nn.silu(a) * v) @ w_down


def setup_inputs(seed: int = 0) -> dict:
    key = jax.random.key(seed)
    ks = jax.random.split(key, 18)

    def nrm(k, shape, scale):
        return jax.random.normal(k, shape, jnp.float32) * scale

    return {
        'x': nrm(ks[0], (BATCH, SEQ, D_MODEL), 1.0),
        'mem': nrm(ks[1], (BATCH, MEM_LEN, D_MODEL), 1.0),
        'rel_bias': nrm(ks[2], (N_BUCKETS, DA_HEADS), 0.5),
        'attn_w_in': nrm(ks[3], (N_ATTN, D_MODEL, ATTN_IN), D_MODEL ** -0.5),
        'attn_lambda': nrm(ks[4], (N_ATTN, 4, DA_HD), 0.1),
        'attn_subln': 1.0 + nrm(ks[5], (N_ATTN, 2 * DA_HD), 0.02),
        'hgrn_w_in': nrm(ks[6], (N_HGRN, D_MODEL, HGRN_IN), D_MODEL ** -0.5),
        'hgrn_lower_bound': 1.0 + nrm(ks[7], (DEPTH, 2, MIX_W), 0.1),
        'hgrn_norm': 1.0 + nrm(ks[8], (N_HGRN, MIX_W), 0.02),
        'conv_w_in': nrm(ks[9], (N_CONV, D_MODEL, CONV_IN), D_MODEL ** -0.5),
        'conv_w': nrm(ks[10], (N_CONV, CONV_W, MIX_W), CONV_W ** -0.5),
        'mem_w_kv': nrm(ks[11], (DEPTH, D_MODEL, 2 * XATTN_W), D_MODEL ** -0.5),
        'w_o': nrm(ks[12], (DEPTH, D_MODEL, D_MODEL), BETA * D_MODEL ** -0.5),
        'ln_gain': 1.0 + nrm(ks[13], (DEPTH, 2, D_MODEL), 0.02),
        'ln_bias': nrm(ks[14], (DEPTH, 2, D_MODEL), 0.02),
        'ffn_w_up': nrm(ks[15], (DEPTH, D_MODEL, 2 * D_FF), D_MODEL ** -0.5),
        'ffn_conv': nrm(ks[16], (DEPTH, CONV_W, 2 * D_FF), CONV_W ** -0.5),
        'ffn_w_down': nrm(ks[17], (DEPTH, D_FF, D_MODEL), BETA * D_FF ** -0.5),
    }


def reference(x, mem, rel_bias, attn_w_in, attn_lambda, attn_subln, hgrn_w_in,
              hgrn_lower_bound, hgrn_norm, conv_w_in, conv_w, mem_w_kv, w_o,
              ln_gain, ln_bias, ffn_w_up, ffn_conv, ffn_w_down):
    for layer in range(DEPTH):
        kind, j = layer % N_MIXERS, layer // N_MIXERS
        if kind == 0:
            p = x @ attn_w_in[j]
            mixed = _diff_attention(p[..., :3 * MIX_W], attn_lambda[j], attn_subln[j], rel_bias, layer)
        elif kind == 1:
            p = x @ hgrn_w_in[j]
            mixed = _hgrn2(p[..., :5 * MIX_W], hgrn_lower_bound, hgrn_norm[j], layer)
        else:
            p = x @ conv_w_in[j]
            mixed = _short_conv(p[..., :3 * MIX_W], conv_w[j])
        recalled = _memory_attention(p[..., 3 * MIX_W if kind != 1 else 5 * MIX_W:], mem, mem_w_kv[layer])
        y = jnp.concatenate([mixed, recalled], axis=-1) @ w_o[layer]
        x = _layer_norm(ALPHA * x + y, ln_gain[layer, 0], ln_bias[layer, 0])
        f = _conv_ffn(x, ffn_w_up[layer], ffn_conv[layer], ffn_w_down[layer])
        x = _layer_norm(ALPHA * x + f, ln_gain[layer, 1], ln_bias[layer, 1])
    return x
```

```python
import functools
import math

import jax
import jax.numpy as jnp
from jax import lax
from jax.experimental import pallas as pl
from jax.experimental.pallas import tpu as pltpu

D_MODEL = 1024
DEPTH = 4
N_MIXERS = 3
MIX_W = 3 * D_MODEL // 4
XATTN_HEADS = 4
XATTN_W = D_MODEL - MIX_W
XATTN_HD = XATTN_W // XATTN_HEADS
DA_HD = 64
DA_HEADS = MIX_W // (2 * DA_HD)
DA_W = 2 * DA_HD
N_BUCKETS = 32
MAX_DISTANCE = 128
HG_DK = 128
HG_HEADS = MIX_W // HG_DK
HG_CHUNK = 64
HG_SUB = 16
D_FF = ((8 * D_MODEL // 3 + 255) // 256) * 256
LN_EPS = 1e-5
ALPHA = (2 * DEPTH) ** 0.25

LANES = 128
SUBLANES_F32 = 8
SUBLANES_BF16 = 16
VMEM_LIMIT = 52 * 1024 * 1024

ROW_TILE = 512
ATTN_TILE = 256
FFN_TILE = D_FF // 2
HG_ROWS = 512

_NEG = -1e30
_BF = jnp.bfloat16
_F32 = jnp.float32
_NT = (((1,), (1,)), ((), ()))


def _params(*sem):
    return pltpu.CompilerParams(dimension_semantics=sem, vmem_limit_bytes=VMEM_LIMIT)


def _sigmoid(z):
    e = jnp.exp(-jnp.abs(z))
    r = 1.0 / (1.0 + e)
    return jnp.where(z >= 0, r, e * r)


def _silu(z):
    return z * _sigmoid(z)


def _layer_norm(r, g, b):
    mu = jnp.mean(r, axis=-1, keepdims=True)
    rc = r - mu
    var = jnp.mean(rc * rc, axis=-1, keepdims=True)
    return rc * lax.rsqrt(var + LN_EPS) * g + b


def _mm_kernel(x_ref, w_ref, o_ref):
    o_ref[...] = jnp.dot(x_ref[...].astype(_BF), w_ref[...],
                         preferred_element_type=_F32).astype(o_ref.dtype)


def _matmul(x, w, out_dtype, tm, tn):
    m, k = x.shape
    n = w.shape[1]
    return pl.pallas_call(
        _mm_kernel,
        grid=(m // tm, n // tn),
        in_specs=[pl.BlockSpec((tm, k), lambda i, j: (i, 0)),
                  pl.BlockSpec((k, tn), lambda i, j: (0, j))],
        out_specs=pl.BlockSpec((tm, tn), lambda i, j: (i, j)),
        out_shape=jax.ShapeDtypeStruct((m, n), out_dtype),
        compiler_params=_params("parallel", "arbitrary"),
        name="proj",
    )(x, w)


def _memattn_kernel(q_ref, km_ref, vm_ref, o_ref):
    q = q_ref[0].astype(_BF)
    km = km_ref[0]
    vm = vm_ref[0]
    lane = lax.broadcasted_iota(jnp.int32, (1, XATTN_W), 1)
    acc = jnp.zeros(q.shape, _F32)
    for h in range(XATTN_HEADS):
        head = (lane >= h * XATTN_HD) & (lane < (h + 1) * XATTN_HD)
        qh = jnp.where(head, q, jnp.zeros_like(q))
        s = lax.dot_general(qh, km, _NT, preferred_element_type=_F32) * (XATTN_HD ** -0.5)
        p = jnp.exp(s - jnp.max(s, axis=-1, keepdims=True))
        l = jnp.sum(p, axis=-1, keepdims=True)
        vh = jnp.where(head, vm, jnp.zeros_like(vm))
        acc = acc + jnp.dot(p.astype(_BF), vh, preferred_element_type=_F32) / l
    o_ref[0] = acc.astype(o_ref.dtype)


def _memory_attention(p3, kv, q_col, tm):
    b, s, _ = p3.shape
    m = kv.shape[1]
    return pl.pallas_call(
        _memattn_kernel,
        grid=(b, s // tm),
        in_specs=[pl.BlockSpec((1, tm, XATTN_W), lambda bi, i: (bi, i, q_col)),
                  pl.BlockSpec((1, m, XATTN_W), lambda bi, i: (bi, 0, 0)),
                  pl.BlockSpec((1, m, XATTN_W), lambda bi, i: (bi, 0, 1))],
        out_specs=pl.BlockSpec((1, tm, XATTN_W), lambda bi, i: (bi, i, 0)),
        out_shape=jax.ShapeDtypeStruct((b, s, XATTN_W), _BF),
        compiler_params=_params("parallel", "parallel"),
        name="memattn",
    )(p3, kv, kv)


def _out_ln_kernel(x_ref, mix_ref, rec_ref, wo_ref, g_ref, b_ref, o_ref):
    y = jnp.dot(mix_ref[...], wo_ref[:MIX_W, :], preferred_element_type=_F32)
    y = y + jnp.dot(rec_ref[...], wo_ref[MIX_W:, :], preferred_element_type=_F32)
    o_ref[...] = _layer_norm(ALPHA * x_ref[...] + y, g_ref[...], b_ref[...])


def _out_ln(x2, mixed, recalled, wo, g, b, tm):
    n = x2.shape[0]
    return pl.pallas_call(
        _out_ln_kernel,
        grid=(n // tm,),
        in_specs=[pl.BlockSpec((tm, D_MODEL), lambda i: (i, 0)),
                  pl.BlockSpec((tm, MIX_W), lambda i: (i, 0)),
                  pl.BlockSpec((tm, XATTN_W), lambda i: (i, 0)),
                  pl.BlockSpec((D_MODEL, D_MODEL), lambda i: (0, 0)),
                  pl.BlockSpec((1, D_MODEL), lambda i: (0, 0)),
                  pl.BlockSpec((1, D_MODEL), lambda i: (0, 0))],
        out_specs=pl.BlockSpec((tm, D_MODEL), lambda i: (i, 0)),
        out_shape=jax.ShapeDtypeStruct((n, D_MODEL), _F32),
        compiler_params=_params("parallel"),
        name="out_ln",
    )(x2, mixed, recalled, wo, g, b)


def _dwconv_rows(h, w):
    n = h.shape[0]
    return (pltpu.roll(h, 1, 0) * w[0:1] + h * w[1:2] + pltpu.roll(h, n - 1, 0) * w[2:3])


def _ffn_kernel(x_ref, xp_ref, xn_ref, wa_ref, wv_ref, ca_ref, cv_ref, wd_ref, g_ref, b_ref,
                o_ref, xb_ref, acc_ref, *, ts, halo):
    i = pl.program_id(1)
    j = pl.program_id(2)

    @pl.when(j == 0)
    def _():
        prev = jnp.where(i == 0, 0.0, xp_ref[0])
        nxt = jnp.where(i == pl.num_programs(1) - 1, 0.0, xn_ref[0])
        xb_ref[0:halo, :] = prev.astype(_BF)
        xb_ref[halo:halo + ts, :] = x_ref[0].astype(_BF)
        xb_ref[halo + ts:, :] = nxt.astype(_BF)
        acc_ref[...] = jnp.zeros_like(acc_ref)

    xb = xb_ref[...]
    ha = jnp.dot(xb, wa_ref[...], preferred_element_type=_F32)
    hv = jnp.dot(xb, wv_ref[...], preferred_element_type=_F32)
    a = _dwconv_rows(ha, ca_ref[...])[halo:halo + ts]
    v = _dwconv_rows(hv, cv_ref[...])[halo:halo + ts]
    gated = (_silu(a) * v).astype(_BF)
    acc_ref[...] += jnp.dot(gated, wd_ref[...], preferred_element_type=_F32)

    @pl.when(j == pl.num_programs(2) - 1)
    def _():
        o_ref[0] = _layer_norm(ALPHA * x_ref[0] + acc_ref[...], g_ref[...], b_ref[...])


def _conv_ffn_ln(x3, w_up, w_conv, w_down, g, b, ts, fb):
    bsz, s, d = x3.shape
    halo = SUBLANES_BF16
    nj = D_FF // fb
    hb = ts // halo
    last = s // halo - 1
    kern = functools.partial(_ffn_kernel, ts=ts, halo=halo)
    return pl.pallas_call(
        kern,
        grid=(bsz, s // ts, nj),
        in_specs=[pl.BlockSpec((1, ts, d), lambda bi, i, j: (bi, i, 0)),
                  pl.BlockSpec((1, halo, d), lambda bi, i, j: (bi, jnp.maximum(i * hb - 1, 0), 0)),
                  pl.BlockSpec((1, halo, d), lambda bi, i, j: (bi, jnp.minimum((i + 1) * hb, last), 0)),
                  pl.BlockSpec((d, fb), lambda bi, i, j: (0, j)),
                  pl.BlockSpec((d, fb), lambda bi, i, j: (0, nj + j)),
                  pl.BlockSpec((3, fb), lambda bi, i, j: (0, j)),
                  pl.BlockSpec((3, fb), lambda bi, i, j: (0, nj + j)),
                  pl.BlockSpec((fb, d), lambda bi, i, j: (j, 0)),
                  pl.BlockSpec((1, d), lambda bi, i, j: (0, 0)),
                  pl.BlockSpec((1, d), lambda bi, i, j: (0, 0))],
        out_specs=pl.BlockSpec((1, ts, d), lambda bi, i, j: (bi, i, 0)),
        out_shape=jax.ShapeDtypeStruct((bsz, s, d), _F32),
        scratch_shapes=[pltpu.VMEM((ts + 2 * halo, d), _BF), pltpu.VMEM((ts, d), _F32)],
        compiler_params=_params("parallel", "parallel", "arbitrary"),
        name="conv_ffn",
    )(x3, x3, x3, w_up, w_up, w_conv, w_conv, w_down, g, b)


def _attn_kernel(lam_ref, q_ref, k_ref, v_ref, bias_ref, subln_ref, o_ref,
                 m_ref, l_ref, acc_ref, *, t, out_scale):
    i = pl.program_id(2)
    nkv = k_ref.shape[1] // t
    lane = lax.broadcasted_iota(jnp.int32, (1, DA_W), 1)
    q = q_ref[0] * (DA_HD ** -0.5)
    zero = jnp.zeros_like(q)
    qs = (jnp.where(lane < DA_HD, q, zero), jnp.where(lane >= DA_HD, q, zero))

    m_ref[...] = jnp.full_like(m_ref, _NEG)
    l_ref[...] = jnp.zeros_like(l_ref)
    acc_ref[...] = jnp.zeros_like(acc_ref)

    def body(j, carry):
        r0 = pl.multiple_of(j * t, t)
        kj = k_ref[0, pl.ds(r0, t), :]
        vj = v_ref[0, pl.ds(r0, t), :]
        bias = bias_ref[jnp.clip(j - i, -2, 2) + 2, 0]
        for mi in range(2):
            s = lax.dot_general(qs[mi], kj, _NT, preferred_element_type=_F32) + bias
            m_prev = m_ref[mi]
            m_new = jnp.maximum(m_prev, jnp.max(s, axis=-1, keepdims=True))
            alpha = jnp.exp(m_prev - m_new)
            p = jnp.exp(s - m_new)
            l_ref[mi] = alpha * l_ref[mi] + jnp.sum(p, axis=-1, keepdims=True)
            acc_ref[mi] = alpha * acc_ref[mi] + jnp.dot(p.astype(_BF), vj,
                                                        preferred_element_type=_F32)
            m_ref[mi] = m_new
        return carry

    lax.fori_loop(0, nkv, body, 0)

    lam = lam_ref[0]
    o = acc_ref[0] / l_ref[0] - lam * (acc_ref[1] / l_ref[1])
    ms = jnp.mean(o * o, axis=-1, keepdims=True)
    o_ref[0] = (o * lax.rsqrt(ms + LN_EPS) * subln_ref[...] * out_scale).astype(o_ref.dtype)


def _t5_bucket(rel):
    half = N_BUCKETS // 2
    max_exact = half // 2
    ret = jnp.where(rel > 0, half, 0)
    n = jnp.abs(rel)
    nf = jnp.maximum(n, 1).astype(_F32)
    large = max_exact + (jnp.log(nf / max_exact) / math.log(MAX_DISTANCE / max_exact)
                         * (half - max_exact)).astype(jnp.int32)
    large = jnp.minimum(large, half - 1)
    return ret + jnp.where(n < max_exact, n, large)


def _bias_tiles(rel_bias, t):
    assert t + 1 >= MAX_DISTANCE
    r = jnp.arange(t, dtype=jnp.int32)[:, None]
    c = jnp.arange(t, dtype=jnp.int32)[None, :]
    d = jnp.arange(-2, 3, dtype=jnp.int32)[:, None, None]
    table = rel_bias.astype(_F32)
    return table[_t5_bucket(d * t + c - r)].transpose(0, 3, 1, 2)


def _diff_attention(p3, lam, subln, bias, layer, t):
    b, s, _ = p3.shape
    lam_init = 0.8 - 0.6 * math.exp(-0.3 * layer)
    kern = functools.partial(_attn_kernel, t=t, out_scale=1.0 - lam_init)
    return pl.pallas_call(
        kern,
        grid=(b, DA_HEADS, s // t),
        in_specs=[pl.BlockSpec(memory_space=pltpu.SMEM),
                  pl.BlockSpec((1, t, DA_W), lambda bi, h, i: (bi, i, h)),
                  pl.BlockSpec((1, s, DA_W), lambda bi, h, i: (bi, 0, DA_HEADS + h)),
                  pl.BlockSpec((1, s, DA_W), lambda bi, h, i: (bi, 0, 2 * DA_HEADS + h)),
                  pl.BlockSpec((5, 1, t, t), lambda bi, h, i: (0, h, 0, 0)),
                  pl.BlockSpec((1, DA_W), lambda bi, h, i: (0, 0))],
        out_specs=pl.BlockSpec((1, t, DA_W), lambda bi, h, i: (bi, i, h)),
        out_shape=jax.ShapeDtypeStruct((b, s, MIX_W), _BF),
        scratch_shapes=[pltpu.VMEM((2, t, 1), _F32), pltpu.VMEM((2, t, 1), _F32),
                        pltpu.VMEM((2, t, DA_W), _F32)],
        compiler_params=_params("parallel", "parallel", "parallel"),
        name="diff_attn",
    )(lam, p3, p3, p3, bias, subln)


def _hgrn_kernel(q_ref, v_ref, z_ref, lb_ref, o_ref, st_ref, *, rows):
    @pl.when(pl.program_id(1) == 0)
    def _():
        st_ref[...] = jnp.zeros_like(st_ref)

    c_rows = HG_CHUNK
    row = lax.broadcasted_iota(jnp.int32, (c_rows, c_rows), 0)
    col = lax.broadcasted_iota(jnp.int32, (c_rows, c_rows), 1)
    tri = (col <= row).astype(_BF)
    n_sub = c_rows // HG_SUB

    def chunk(c, carry):
        r0 = pl.multiple_of(c * c_rows, c_rows)
        for h in range(HG_HEADS):
            cs = slice(h * HG_DK, (h + 1) * HG_DK)
            z = z_ref[0, pl.ds(r0, c_rows), cs]
            lb = lb_ref[0, :, cs]
            sg = _sigmoid(z)
            f = lb + (1.0 - lb) * sg
            kk = (1.0 - lb) * _sigmoid(-z)
            logf = jnp.log(f)
            hi = logf.astype(_BF)
            lo = (logf - hi.astype(_F32)).astype(_BF)
            cum = (jnp.dot(tri, hi, preferred_element_type=_F32)
                   + jnp.dot(tri, lo, preferred_element_type=_F32))
            last = cum[c_rows - 1:c_rows, :]
            qh = _silu(q_ref[0, pl.ds(r0, c_rows), cs]) * (HG_DK ** -0.5)
            v = v_ref[0, pl.ds(r0, c_rows), cs]
            vb = v.astype(_BF)
            st = st_ref[h]
            o_inter = lax.dot_general((qh * jnp.exp(cum)).astype(_BF), st.astype(_BF), _NT,
                                      preferred_element_type=_F32)
            parts = []
            for b in range(n_sub):
                lo_r, hi_r = b * HG_SUB, (b + 1) * HG_SUB
                base = cum[lo_r - 1:lo_r, :] if b else jnp.zeros_like(last)
                qq = (qh[lo_r:hi_r] * jnp.exp(cum[lo_r:hi_r] - base)).astype(_BF)
                kt = (kk[:hi_r] * jnp.exp(base - cum[:hi_r])).astype(_BF)
                a = lax.dot_general(qq, kt, _NT, preferred_element_type=_F32)
                sub_row = lax.broadcasted_iota(jnp.int32, a.shape, 0) + lo_r
                sub_col = lax.broadcasted_iota(jnp.int32, a.shape, 1)
                a = jnp.where(sub_col <= sub_row, a, 0.0)
                parts.append(o_inter[lo_r:hi_r]
                             + jnp.dot(a.astype(_BF), vb[:hi_r], preferred_element_type=_F32))
            o_ref[0, pl.ds(r0, c_rows), cs] = jnp.concatenate(parts, axis=0)
            ks = (kk * jnp.exp(last - cum)).astype(_BF)
            st_ref[h] = st * jnp.exp(last) + jnp.dot(v.T.astype(_BF), ks,
                                                     preferred_element_type=_F32)
        return carry

    lax.fori_loop(0, rows // c_rows, chunk, 0)


def _hgrn_scan(qd, vd, zd, lb, rows):
    n, s, w = qd.shape
    bsz = n // 2
    spec = pl.BlockSpec((1, rows, w), lambda ni, i: (ni, i, 0))
    return pl.pallas_call(
        functools.partial(_hgrn_kernel, rows=rows),
        grid=(n, s // rows),
        in_specs=[spec, spec, spec, pl.BlockSpec((1, 1, w), lambda ni, i: (ni // bsz, 0, 0))],
        out_specs=spec,
        out_shape=jax.ShapeDtypeStruct((n, s, w), _F32),
        scratch_shapes=[pltpu.VMEM((HG_HEADS, HG_DK, HG_DK), _F32)],
        compiler_params=_params("parallel", "arbitrary"),
        name="hgrn_scan",
    )(qd, vd, zd, lb)


def _hgrn_finish_kernel(of_ref, ob_ref, g_ref, w_ref, o_ref):
    o = of_ref[...] + ob_ref[...]
    gate = _silu(g_ref[...])
    w = w_ref[...]
    for h in range(HG_HEADS):
        cs = slice(h * HG_DK, (h + 1) * HG_DK)
        oh = o[:, cs]
        ms = jnp.mean(oh * oh, axis=-1, keepdims=True)
        o_ref[:, cs] = (oh * lax.rsqrt(ms + LN_EPS) * w[:, cs] * gate[:, cs]).astype(o_ref.dtype)


def _hgrn_finish(o_fw, o_bw, p2, g_col, norm_w, tm):
    n = o_fw.shape[0]
    return pl.pallas_call(
        _hgrn_finish_kernel,
        grid=(n // tm,),
        in_specs=[pl.BlockSpec((tm, MIX_W), lambda i: (i, 0)),
                  pl.BlockSpec((tm, MIX_W), lambda i: (i, 0)),
                  pl.BlockSpec((tm, MIX_W), lambda i: (i, g_col)),
                  pl.BlockSpec((1, MIX_W), lambda i: (0, 0))],
        out_specs=pl.BlockSpec((tm, MIX_W), lambda i: (i, 0)),
        out_shape=jax.ShapeDtypeStruct((n, MIX_W), _BF),
        compiler_params=_params("parallel"),
        name="hgrn_finish",
    )(o_fw, o_bw, p2, norm_w)


def _sconv_kernel(gb_ref, gc_ref, h_ref, gcp_ref, hp_ref, gcn_ref, hn_ref, w_ref, o_ref, *, halo):
    i = pl.program_id(1)
    u = gc_ref[0].astype(_F32) * h_ref[0].astype(_F32)
    n = u.shape[0]
    u_prev = (gcp_ref[0].astype(_F32) * hp_ref[0].astype(_F32))[halo - 1:halo]
    u_next = (gcn_ref[0].astype(_F32) * hn_ref[0].astype(_F32))[0:1]
    u_prev = jnp.where(i == 0, 0.0, u_prev)
    u_next = jnp.where(i == pl.num_programs(1) - 1, 0.0, u_next)
    row = lax.broadcasted_iota(jnp.int32, u.shape, 0)
    down = jnp.where(row == 0, u_prev, pltpu.roll(u, 1, 0))
    up = jnp.where(row == n - 1, u_next, pltpu.roll(u, n - 1, 0))
    w = w_ref[...]
    y = down * w[0:1] + u * w[1:2] + up * w[2:3]
    o_ref[0] = (gb_ref[0].astype(_F32) * y).astype(o_ref.dtype)


def _short_conv(p3, conv_w, ts):
    b, s, _ = p3.shape
    halo = SUBLANES_BF16
    hb = ts // halo
    last = s // halo - 1
    main = lambda c: pl.BlockSpec((1, ts, MIX_W), lambda bi, i: (bi, i, c))
    prev = lambda c: pl.BlockSpec((1, halo, MIX_W), lambda bi, i: (bi, jnp.maximum(i * hb - 1, 0), c))
    nxt = lambda c: pl.BlockSpec((1, halo, MIX_W), lambda bi, i: (bi, jnp.minimum((i + 1) * hb, last), c))
    return pl.pallas_call(
        functools.partial(_sconv_kernel, halo=halo),
        grid=(b, s // ts),
        in_specs=[main(0), main(1), main(2), prev(1), prev(2), nxt(1), nxt(2),
                  pl.BlockSpec((3, MIX_W), lambda bi, i: (0, 0))],
        out_specs=pl.BlockSpec((1, ts, MIX_W), lambda bi, i: (bi, i, 0)),
        out_shape=jax.ShapeDtypeStruct((b, s, MIX_W), _BF),
        compiler_params=_params("parallel", "parallel"),
        name="short_conv",
    )(p3, p3, p3, p3, p3, p3, p3, conv_w)


def _tile(n, pref):
    return pref if n % pref == 0 else n


def kernel(x, mem, rel_bias, attn_w_in, attn_lambda, attn_subln, hgrn_w_in, hgrn_lower_bound,
           hgrn_norm, conv_w_in, conv_w, mem_w_kv, w_o, ln_gain, ln_bias, ffn_w_up, ffn_conv,
           ffn_w_down):
    b, s, d = x.shape
    n = b * s
    m = mem.shape[1]
    tm = _tile(s, ROW_TILE)
    t_attn = _tile(s, ATTN_TILE)
    hg_rows = _tile(s, HG_ROWS)

    bias = _bias_tiles(rel_bias, t_attn)
    mem2 = mem.reshape(b * m, d)
    lbw = jax.nn.softmax(hgrn_lower_bound.astype(_F32), axis=0)
    lb_all = jnp.cumsum(lbw, axis=0) - lbw[0]

    for layer in range(DEPTH):
        kind, j = layer % N_MIXERS, layer // N_MIXERS
        x2 = x.reshape(n, d)
        kv = _matmul(mem2, mem_w_kv[layer].astype(_BF), _BF, _tile(b * m, ROW_TILE), 2 * XATTN_W)
        kv = kv.reshape(b, m, 2 * XATTN_W)
        if kind == 0:
            w_in = attn_w_in[j].astype(_BF)
            p = _matmul(x2, w_in, _BF, tm, w_in.shape[1] // 2).reshape(b, s, -1)
            lp = attn_lambda[j].astype(_F32)
            lam_init = 0.8 - 0.6 * math.exp(-0.3 * layer)
            lam = jnp.exp(jnp.sum(lp[0] * lp[1])) - jnp.exp(jnp.sum(lp[2] * lp[3])) + lam_init
            mixed = _diff_attention(p, lam.reshape(1), attn_subln[j].astype(_F32).reshape(1, DA_W),
                                    bias, layer, t_attn)
            q_col = 3 * MIX_W // XATTN_W
        elif kind == 1:
            w_in = hgrn_w_in[j].astype(_BF)
            p = _matmul(x2, w_in, _F32, tm, w_in.shape[1] // 2).reshape(b, s, -1)
            q, i_, g, f_fw, f_bw = (p[..., c * MIX_W:(c + 1) * MIX_W] for c in range(5))
            qd = jnp.concatenate([q, jnp.flip(q, 1)], axis=0)
            vd = jnp.concatenate([i_, jnp.flip(i_, 1)], axis=0)
            zd = jnp.concatenate([f_fw, jnp.flip(f_bw, 1)], axis=0)
            o = _hgrn_scan(qd, vd, zd, lb_all[layer].reshape(2, 1, MIX_W), hg_rows)
            o_fw = o[:b].reshape(n, MIX_W)
            o_bw = jnp.flip(o[b:], 1).reshape(n, MIX_W)
            mixed = _hgrn_finish(o_fw, o_bw, p.reshape(n, -1), 2,
                                 hgrn_norm[j].astype(_F32).reshape(1, MIX_W), tm)
            q_col = 5 * MIX_W // XATTN_W
        else:
            w_in = conv_w_in[j].astype(_BF)
            p = _matmul(x2, w_in, _BF, tm, w_in.shape[1] // 2).reshape(b, s, -1)
            mixed = _short_conv(p, conv_w[j].astype(_F32), tm)
            q_col = 3 * MIX_W // XATTN_W
        recalled = _memory_attention(p, kv, q_col, tm)
        x2 = _out_ln(x2, mixed.reshape(n, MIX_W), recalled.reshape(n, XATTN_W),
                     w_o[layer].astype(_BF), ln_gain[layer, 0].reshape(1, d),
                     ln_bias[layer, 0].reshape(1, d), tm)
        x = _conv_ffn_ln(x2.reshape(b, s, d), ffn_w_up[layer].astype(_BF),
                         ffn_conv[layer].astype(_F32), ffn_w_down[layer].astype(_BF),
                         ln_gain[layer, 1].reshape(1, d), ln_bias[layer, 1].reshape(1, d),
                         tm, FFN_TILE)
    return x
```

```python
import functools
import math

import jax
import jax.numpy as jnp
from jax import lax
from jax.experimental import pallas as pl
from jax.experimental.pallas import tpu as pltpu

D_MODEL = 1024
DEPTH = 4
N_MIXERS = 3
MIX_W = 3 * D_MODEL // 4
XATTN_HEADS = 4
XATTN_W = D_MODEL - MIX_W
XATTN_HD = XATTN_W // XATTN_HEADS
DA_HD = 64
DA_HEADS = MIX_W // (2 * DA_HD)
DA_W = 2 * DA_HD
N_BUCKETS = 32
MAX_DISTANCE = 128
HG_DK = 128
HG_HEADS = MIX_W // HG_DK
HG_CHUNK = 64
HG_SUB = 16
D_FF = ((8 * D_MODEL // 3 + 255) // 256) * 256
LN_EPS = 1e-5
ALPHA = (2 * DEPTH) ** 0.25
LOG2E = math.log2(math.e)

LANES = 128
SUBLANES_F32 = 8
SUBLANES_BF16 = 16
VMEM_LIMIT = 52 * 1024 * 1024

ROW_TILE = 512
ATTN_TILE = 256
FFN_TILE = D_FF // 2
HG_ROWS = 512

_NEG = -1e30
_BF = jnp.bfloat16
_F32 = jnp.float32
_NT = (((1,), (1,)), ((), ()))


def _params(*sem):
    return pltpu.CompilerParams(dimension_semantics=sem, vmem_limit_bytes=VMEM_LIMIT)


def _sigmoid(z):
    e = jnp.exp(-jnp.abs(z))
    r = 1.0 / (1.0 + e)
    return jnp.where(z >= 0, r, e * r)


def _silu(z):
    return z * _sigmoid(z)


def _layer_norm(r, g, b):
    mu = jnp.mean(r, axis=-1, keepdims=True)
    rc = r - mu
    var = jnp.mean(rc * rc, axis=-1, keepdims=True)
    return rc * lax.rsqrt(var + LN_EPS) * g + b


def _mm_kernel(x_ref, w_ref, o_ref):
    o_ref[...] = jnp.dot(x_ref[...].astype(_BF), w_ref[...],
                         preferred_element_type=_F32).astype(o_ref.dtype)


def _matmul(x, w, out_dtype, tm, tn):
    m, k = x.shape
    n = w.shape[1]
    return pl.pallas_call(
        _mm_kernel,
        grid=(m // tm, n // tn),
        in_specs=[pl.BlockSpec((tm, k), lambda i, j: (i, 0)),
                  pl.BlockSpec((k, tn), lambda i, j: (0, j))],
        out_specs=pl.BlockSpec((tm, tn), lambda i, j: (i, j)),
        out_shape=jax.ShapeDtypeStruct((m, n), out_dtype),
        compiler_params=_params("parallel", "arbitrary"),
        name="proj",
    )(x, w)


def _memattn_kernel(q_ref, km_ref, vm_ref, o_ref):
    q = q_ref[0].astype(_BF)
    km = km_ref[0]
    vm = vm_ref[0]
    lane = lax.broadcasted_iota(jnp.int32, (1, XATTN_W), 1)
    acc = jnp.zeros(q.shape, _F32)
    for h in range(XATTN_HEADS):
        head = (lane >= h * XATTN_HD) & (lane < (h + 1) * XATTN_HD)
        qh = jnp.where(head, q, jnp.zeros_like(q))
        s = lax.dot_general(qh, km, _NT, preferred_element_type=_F32) * (XATTN_HD ** -0.5)
        p = jnp.exp(s - jnp.max(s, axis=-1, keepdims=True))
        l = jnp.sum(p, axis=-1, keepdims=True)
        vh = jnp.where(head, vm, jnp.zeros_like(vm))
        acc = acc + jnp.dot(p.astype(_BF), vh, preferred_element_type=_F32) / l
    o_ref[0] = acc.astype(o_ref.dtype)


def _memory_attention(p3, kv, q_col, tm):
    b, s, _ = p3.shape
    m = kv.shape[1]
    return pl.pallas_call(
        _memattn_kernel,
        grid=(b, s // tm),
        in_specs=[pl.BlockSpec((1, tm, XATTN_W), lambda bi, i: (bi, i, q_col)),
                  pl.BlockSpec((1, m, XATTN_W), lambda bi, i: (bi, 0, 0)),
                  pl.BlockSpec((1, m, XATTN_W), lambda bi, i: (bi, 0, 1))],
        out_specs=pl.BlockSpec((1, tm, XATTN_W), lambda bi, i: (bi, i, 0)),
        out_shape=jax.ShapeDtypeStruct((b, s, XATTN_W), _BF),
        compiler_params=_params("parallel", "parallel"),
        name="memattn",
    )(p3, kv, kv)


def _out_ln_kernel(x_ref, mix_ref, rec_ref, wo_ref, g_ref, b_ref, o_ref):
    y = jnp.dot(mix_ref[...], wo_ref[:MIX_W, :], preferred_element_type=_F32)
    y = y + jnp.dot(rec_ref[...], wo_ref[MIX_W:, :], preferred_element_type=_F32)
    o_ref[...] = _layer_norm(ALPHA * x_ref[...] + y, g_ref[...], b_ref[...])


def _out_ln(x2, mixed, recalled, wo, g, b, tm):
    n = x2.shape[0]
    return pl.pallas_call(
        _out_ln_kernel,
        grid=(n // tm,),
        in_specs=[pl.BlockSpec((tm, D_MODEL), lambda i: (i, 0)),
                  pl.BlockSpec((tm, MIX_W), lambda i: (i, 0)),
                  pl.BlockSpec((tm, XATTN_W), lambda i: (i, 0)),
                  pl.BlockSpec((D_MODEL, D_MODEL), lambda i: (0, 0)),
                  pl.BlockSpec((1, D_MODEL), lambda i: (0, 0)),
                  pl.BlockSpec((1, D_MODEL), lambda i: (0, 0))],
        out_specs=pl.BlockSpec((tm, D_MODEL), lambda i: (i, 0)),
        out_shape=jax.ShapeDtypeStruct((n, D_MODEL), _F32),
        compiler_params=_params("parallel"),
        name="out_ln",
    )(x2, mixed, recalled, wo, g, b)


def _dwconv_rows(h, w):
    n = h.shape[0]
    return (pltpu.roll(h, 1, 0) * w[0:1] + h * w[1:2] + pltpu.roll(h, n - 1, 0) * w[2:3])


def _ffn_kernel(x_ref, xp_ref, xn_ref, wa_ref, wv_ref, ca_ref, cv_ref, wd_ref, g_ref, b_ref,
                o_ref, xb_ref, acc_ref, *, ts, halo):
    i = pl.program_id(1)
    j = pl.program_id(2)

    @pl.when(j == 0)
    def _():
        prev = jnp.where(i == 0, 0.0, xp_ref[0])
        nxt = jnp.where(i == pl.num_programs(1) - 1, 0.0, xn_ref[0])
        xb_ref[0:halo, :] = prev.astype(_BF)
        xb_ref[halo:halo + ts, :] = x_ref[0].astype(_BF)
        xb_ref[halo + ts:, :] = nxt.astype(_BF)
        acc_ref[...] = jnp.zeros_like(acc_ref)

    xb = xb_ref[...]
    ha = jnp.dot(xb, wa_ref[...], preferred_element_type=_F32)
    hv = jnp.dot(xb, wv_ref[...], preferred_element_type=_F32)
    a = _dwconv_rows(ha, ca_ref[...])[halo:halo + ts]
    v = _dwconv_rows(hv, cv_ref[...])[halo:halo + ts]
    gated = (_silu(a) * v).astype(_BF)
    acc_ref[...] += jnp.dot(gated, wd_ref[...], preferred_element_type=_F32)

    @pl.when(j == pl.num_programs(2) - 1)
    def _():
        o_ref[0] = _layer_norm(ALPHA * x_ref[0] + acc_ref[...], g_ref[...], b_ref[...])


def _conv_ffn_ln(x3, w_up, w_conv, w_down, g, b, ts, fb):
    bsz, s, d = x3.shape
    halo = SUBLANES_BF16
    nj = D_FF // fb
    hb = ts // halo
    last = s // halo - 1
    kern = functools.partial(_ffn_kernel, ts=ts, halo=halo)
    return pl.pallas_call(
        kern,
        grid=(bsz, s // ts, nj),
        in_specs=[pl.BlockSpec((1, ts, d), lambda bi, i, j: (bi, i, 0)),
                  pl.BlockSpec((1, halo, d), lambda bi, i, j: (bi, jnp.maximum(i * hb - 1, 0), 0)),
                  pl.BlockSpec((1, halo, d), lambda bi, i, j: (bi, jnp.minimum((i + 1) * hb, last), 0)),
                  pl.BlockSpec((d, fb), lambda bi, i, j: (0, j)),
                  pl.BlockSpec((d, fb), lambda bi, i, j: (0, nj + j)),
                  pl.BlockSpec((3, fb), lambda bi, i, j: (0, j)),
                  pl.BlockSpec((3, fb), lambda bi, i, j: (0, nj + j)),
                  pl.BlockSpec((fb, d), lambda bi, i, j: (j, 0)),
                  pl.BlockSpec((1, d), lambda bi, i, j: (0, 0)),
                  pl.BlockSpec((1, d), lambda bi, i, j: (0, 0))],
        out_specs=pl.BlockSpec((1, ts, d), lambda bi, i, j: (bi, i, 0)),
        out_shape=jax.ShapeDtypeStruct((bsz, s, d), _F32),
        scratch_shapes=[pltpu.VMEM((ts + 2 * halo, d), _BF), pltpu.VMEM((ts, d), _F32)],
        compiler_params=_params("parallel", "parallel", "arbitrary"),
        name="conv_ffn",
    )(x3, x3, x3, w_up, w_up, w_conv, w_conv, w_down, g, b)


def _fold_rows(x, op, group=4 * SUBLANES_F32):
    acc = x[:group]
    for g in range(1, x.shape[0] // group):
        acc = op(acc, x[g * group:(g + 1) * group])
    while acc.shape[0] > SUBLANES_F32:
        half = acc.shape[0] // 2
        acc = op(acc[:half], acc[half:])
    return acc


def _attn_kernel(lam_ref, far_ref, q_ref, k_ref, vt_ref, bias_ref, subln_ref, o_ref,
                 s_a, s_b, t_a, t_b, p_a, p_b, acc_ref, *, tq, tk, out_scale):
    h = pl.program_id(1)
    i = pl.program_id(2)
    nkv = k_ref.shape[1] // tk
    lane = lax.broadcasted_iota(jnp.int32, (1, DA_W), 1)
    q = q_ref[0]
    zero = jnp.zeros_like(q)
    qs = (jnp.where(lane < DA_HD, q, zero), jnp.where(lane >= DA_HD, q, zero))

    j0 = (i - 1) // 2
    lo = jnp.maximum(j0, 0)
    hi = jnp.minimum(j0 + 2, nkv)
    far_left = far_ref[2 * h]
    far_right = far_ref[2 * h + 1]

    def scores(j, s_out, t_out):
        kj = k_ref[0, pl.ds(pl.multiple_of(j * tk, tk), tk), :]
        for mi in range(2):
            s = lax.dot_general(kj, qs[mi], _NT, preferred_element_type=_F32)
            s_out[mi] = s
            t_out[mi] = _fold_rows(s, jnp.maximum)

    def add_near_bias(j, s_out, t_out):
        @pl.when((j >= lo) & (j < hi))
        def _():
            tile = bias_ref[2 * j - i + 2, 0]
            for mi in range(2):
                s = s_out[mi] + tile
                s_out[mi] = s
                t_out[mi] = _fold_rows(s, jnp.maximum)

    def weighted_values(j, p_in, mi):
        vtj = vt_ref[0, :, pl.ds(pl.multiple_of(j * tk, tk), tk)]
        return jnp.dot(vtj, p_in[mi], preferred_element_type=_F32)

    def half_step(j, carry, cur, nxt):
        s_cur, t_cur, p_cur = cur
        s_nxt, t_nxt, p_prev = nxt
        jn = jnp.minimum(j + 1, nkv - 1)
        scores(jn, s_nxt, t_nxt)
        shift = jnp.where(j < lo, far_left, jnp.where(j >= hi, far_right, 0.0))
        new = []
        for mi in range(2):
            m_prev, l_prev = carry[mi]
            m_new = jnp.maximum(m_prev, jnp.max(t_cur[mi], axis=0, keepdims=True) + shift)
            alpha = jnp.exp2(m_prev - m_new)
            p = jnp.exp2(s_cur[mi] - (m_new - shift))
            l_new = alpha * l_prev + jnp.sum(_fold_rows(p, jnp.add), axis=0, keepdims=True)
            acc_ref[mi] = alpha * (acc_ref[mi] + weighted_values(jnp.maximum(j - 1, 0), p_prev, mi))
            p_cur[mi] = p.astype(_BF)
            new.append((m_new, l_new))
        add_near_bias(jn, s_nxt, t_nxt)
        return tuple(new)

    acc_ref[...] = jnp.zeros_like(acc_ref)
    p_b[...] = jnp.zeros_like(p_b)
    scores(0, s_a, t_a)
    add_near_bias(0, s_a, t_a)

    def pair(jj, carry):
        carry = half_step(2 * jj, carry, (s_a, t_a, p_a), (s_b, t_b, p_b))
        return half_step(2 * jj + 1, carry, (s_b, t_b, p_b), (s_a, t_a, p_a))

    init = ((jnp.full((1, tq), _NEG, _F32), jnp.zeros((1, tq), _F32)),) * 2
    c = lax.fori_loop(0, nkv // 2, pair, init)

    lam = lam_ref[0]
    acc = [acc_ref[mi] + weighted_values(nkv - 1, p_b, mi) for mi in range(2)]
    ot = acc[0] / c[0][1] - lam * (acc[1] / c[1][1])
    o = ot.T
    ms = jnp.mean(o * o, axis=-1, keepdims=True)
    o_ref[0] = (o * lax.rsqrt(ms + LN_EPS) * subln_ref[...] * out_scale).astype(o_ref.dtype)


def _t5_bucket(rel):
    half = N_BUCKETS // 2
    max_exact = half // 2
    ret = jnp.where(rel > 0, half, 0)
    n = jnp.abs(rel)
    nf = jnp.maximum(n, 1).astype(_F32)
    large = max_exact + (jnp.log(nf / max_exact) / math.log(MAX_DISTANCE / max_exact)
                         * (half - max_exact)).astype(jnp.int32)
    large = jnp.minimum(large, half - 1)
    return ret + jnp.where(n < max_exact, n, large)


def _bias_tiles(rel_bias, tq):
    assert tq + 1 >= MAX_DISTANCE
    c = jnp.arange(2 * tq, dtype=jnp.int32)[:, None]
    r = jnp.arange(tq, dtype=jnp.int32)[None, :]
    d = jnp.arange(-2, 2, dtype=jnp.int32)[:, None, None]
    table = rel_bias.astype(_F32) * LOG2E
    tiles = table[_t5_bucket(d * tq + c - r)].transpose(0, 3, 1, 2)
    far = table[_t5_bucket(jnp.array([-MAX_DISTANCE, MAX_DISTANCE], jnp.int32))]
    return tiles, far.T.reshape(-1)


def _diff_attention(p3, vt, lam, subln, bias, far, layer, tq):
    b, s, _ = p3.shape
    tk = 2 * tq
    assert s % (2 * tk) == 0
    lam_init = 0.8 - 0.6 * math.exp(-0.3 * layer)
    kern = functools.partial(_attn_kernel, tq=tq, tk=tk, out_scale=1.0 - lam_init)
    return pl.pallas_call(
        kern,
        grid=(b, DA_HEADS, s // tq),
        in_specs=[pl.BlockSpec(memory_space=pltpu.SMEM),
                  pl.BlockSpec(memory_space=pltpu.SMEM),
                  pl.BlockSpec((1, tq, DA_W), lambda bi, h, i: (bi, i, h)),
                  pl.BlockSpec((1, s, DA_W), lambda bi, h, i: (bi, 0, DA_HEADS + h)),
                  pl.BlockSpec((1, DA_W, s), lambda bi, h, i: (bi, h, 0)),
                  pl.BlockSpec((4, 1, tk, tq), lambda bi, h, i: (0, h, 0, 0)),
                  pl.BlockSpec((1, DA_W), lambda bi, h, i: (0, 0))],
        out_specs=pl.BlockSpec((1, tq, DA_W), lambda bi, h, i: (bi, i, h)),
        out_shape=jax.ShapeDtypeStruct((b, s, MIX_W), _BF),
        scratch_shapes=[pltpu.VMEM((2, tk, tq), _F32), pltpu.VMEM((2, tk, tq), _F32),
                        pltpu.VMEM((2, SUBLANES_F32, tq), _F32),
                        pltpu.VMEM((2, SUBLANES_F32, tq), _F32),
                        pltpu.VMEM((2, tk, tq), _BF), pltpu.VMEM((2, tk, tq), _BF),
                        pltpu.VMEM((2, DA_W, tq), _F32)],
        compiler_params=_params("parallel", "parallel", "parallel"),
        name="diff_attn",
    )(lam, far, p3, p3, vt, bias, subln)


def _hgrn_kernel(q_ref, v_ref, z_ref, lb_ref, o_ref, st_ref, *, rows):
    @pl.when(pl.program_id(1) == 0)
    def _():
        st_ref[...] = jnp.zeros_like(st_ref)

    c_rows = HG_CHUNK
    row = lax.broadcasted_iota(jnp.int32, (c_rows, c_rows), 0)
    col = lax.broadcasted_iota(jnp.int32, (c_rows, c_rows), 1)
    tri = (col <= row).astype(_BF)
    n_sub = c_rows // HG_SUB

    def chunk(c, carry):
        r0 = pl.multiple_of(c * c_rows, c_rows)
        for h in range(HG_HEADS):
            cs = slice(h * HG_DK, (h + 1) * HG_DK)
            z = z_ref[0, pl.ds(r0, c_rows), cs]
            lb = lb_ref[0, :, cs]
            sg = _sigmoid(z)
            f = lb + (1.0 - lb) * sg
            kk = (1.0 - lb) * _sigmoid(-z)
            logf = jnp.log(f)
            hi = logf.astype(_BF)
            lo = (logf - hi.astype(_F32)).astype(_BF)
            cum = (jnp.dot(tri, hi, preferred_element_type=_F32)
                   + jnp.dot(tri, lo, preferred_element_type=_F32))
            last = cum[c_rows - 1:c_rows, :]
            qh = _silu(q_ref[0, pl.ds(r0, c_rows), cs]) * (HG_DK ** -0.5)
            v = v_ref[0, pl.ds(r0, c_rows), cs]
            vb = v.astype(_BF)
            st = st_ref[h]
            o_inter = lax.dot_general((qh * jnp.exp(cum)).astype(_BF), st.astype(_BF), _NT,
                                      preferred_element_type=_F32)
            parts = []
            for b in range(n_sub):
                lo_r, hi_r = b * HG_SUB, (b + 1) * HG_SUB
                base = cum[lo_r - 1:lo_r, :] if b else jnp.zeros_like(last)
                qq = (qh[lo_r:hi_r] * jnp.exp(cum[lo_r:hi_r] - base)).astype(_BF)
                kt = (kk[:hi_r] * jnp.exp(base - cum[:hi_r])).astype(_BF)
                a = lax.dot_general(qq, kt, _NT, preferred_element_type=_F32)
                sub_row = lax.broadcasted_iota(jnp.int32, a.shape, 0) + lo_r
                sub_col = lax.broadcasted_iota(jnp.int32, a.shape, 1)
                a = jnp.where(sub_col <= sub_row, a, 0.0)
                parts.append(o_inter[lo_r:hi_r]
                             + jnp.dot(a.astype(_BF), vb[:hi_r], preferred_element_type=_F32))
            o_ref[0, pl.ds(r0, c_rows), cs] = jnp.concatenate(parts, axis=0)
            ks = (kk * jnp.exp(last - cum)).astype(_BF)
            st_ref[h] = st * jnp.exp(last) + jnp.dot(v.T.astype(_BF), ks,
                                                     preferred_element_type=_F32)
        return carry

    lax.fori_loop(0, rows // c_rows, chunk, 0)


def _hgrn_scan(qd, vd, zd, lb, rows):
    n, s, w = qd.shape
    bsz = n // 2
    spec = pl.BlockSpec((1, rows, w), lambda ni, i: (ni, i, 0))
    return pl.pallas_call(
        functools.partial(_hgrn_kernel, rows=rows),
        grid=(n, s // rows),
        in_specs=[spec, spec, spec, pl.BlockSpec((1, 1, w), lambda ni, i: (ni // bsz, 0, 0))],
        out_specs=spec,
        out_shape=jax.ShapeDtypeStruct((n, s, w), _F32),
        scratch_shapes=[pltpu.VMEM((HG_HEADS, HG_DK, HG_DK), _F32)],
        compiler_params=_params("parallel", "arbitrary"),
        name="hgrn_scan",
    )(qd, vd, zd, lb)


def _hgrn_finish_kernel(of_ref, ob_ref, g_ref, w_ref, o_ref):
    o = of_ref[...] + ob_ref[...]
    gate = _silu(g_ref[...])
    w = w_ref[...]
    for h in range(HG_HEADS):
        cs = slice(h * HG_DK, (h + 1) * HG_DK)
        oh = o[:, cs]
        ms = jnp.mean(oh * oh, axis=-1, keepdims=True)
        o_ref[:, cs] = (oh * lax.rsqrt(ms + LN_EPS) * w[:, cs] * gate[:, cs]).astype(o_ref.dtype)


def _hgrn_finish(o_fw, o_bw, p2, g_col, norm_w, tm):
    n = o_fw.shape[0]
    return pl.pallas_call(
        _hgrn_finish_kernel,
        grid=(n // tm,),
        in_specs=[pl.BlockSpec((tm, MIX_W), lambda i: (i, 0)),
                  pl.BlockSpec((tm, MIX_W), lambda i: (i, 0)),
                  pl.BlockSpec((tm, MIX_W), lambda i: (i, g_col)),
                  pl.BlockSpec((1, MIX_W), lambda i: (0, 0))],
        out_specs=pl.BlockSpec((tm, MIX_W), lambda i: (i, 0)),
        out_shape=jax.ShapeDtypeStruct((n, MIX_W), _BF),
        compiler_params=_params("parallel"),
        name="hgrn_finish",
    )(o_fw, o_bw, p2, norm_w)


def _sconv_kernel(gb_ref, gc_ref, h_ref, gcp_ref, hp_ref, gcn_ref, hn_ref, w_ref, o_ref, *, halo):
    i = pl.program_id(1)
    u = gc_ref[0].astype(_F32) * h_ref[0].astype(_F32)
    n = u.shape[0]
    u_prev = (gcp_ref[0].astype(_F32) * hp_ref[0].astype(_F32))[halo - 1:halo]
    u_next = (gcn_ref[0].astype(_F32) * hn_ref[0].astype(_F32))[0:1]
    u_prev = jnp.where(i == 0, 0.0, u_prev)
    u_next = jnp.where(i == pl.num_programs(1) - 1, 0.0, u_next)
    row = lax.broadcasted_iota(jnp.int32, u.shape, 0)
    down = jnp.where(row == 0, u_prev, pltpu.roll(u, 1, 0))
    up = jnp.where(row == n - 1, u_next, pltpu.roll(u, n - 1, 0))
    w = w_ref[...]
    y = down * w[0:1] + u * w[1:2] + up * w[2:3]
    o_ref[0] = (gb_ref[0].astype(_F32) * y).astype(o_ref.dtype)


def _short_conv(p3, conv_w, ts):
    b, s, _ = p3.shape
    halo = SUBLANES_BF16
    hb = ts // halo
    last = s // halo - 1
    main = lambda c: pl.BlockSpec((1, ts, MIX_W), lambda bi, i: (bi, i, c))
    prev = lambda c: pl.BlockSpec((1, halo, MIX_W), lambda bi, i: (bi, jnp.maximum(i * hb - 1, 0), c))
    nxt = lambda c: pl.BlockSpec((1, halo, MIX_W), lambda bi, i: (bi, jnp.minimum((i + 1) * hb, last), c))
    return pl.pallas_call(
        functools.partial(_sconv_kernel, halo=halo),
        grid=(b, s // ts),
        in_specs=[main(0), main(1), main(2), prev(1), prev(2), nxt(1), nxt(2),
                  pl.BlockSpec((3, MIX_W), lambda bi, i: (0, 0))],
        out_specs=pl.BlockSpec((1, ts, MIX_W), lambda bi, i: (bi, i, 0)),
        out_shape=jax.ShapeDtypeStruct((b, s, MIX_W), _BF),
        compiler_params=_params("parallel", "parallel"),
        name="short_conv",
    )(p3, p3, p3, p3, p3, p3, p3, conv_w)


def _tile(n, pref):
    return pref if n % pref == 0 else n


def kernel(x, mem, rel_bias, attn_w_in, attn_lambda, attn_subln, hgrn_w_in, hgrn_lower_bound,
           hgrn_norm, conv_w_in, conv_w, mem_w_kv, w_o, ln_gain, ln_bias, ffn_w_up, ffn_conv,
           ffn_w_down):
    b, s, d = x.shape
    n = b * s
    m = mem.shape[1]
    tm = _tile(s, ROW_TILE)
    t_attn = _tile(s, ATTN_TILE)
    hg_rows = _tile(s, HG_ROWS)

    bias, far = _bias_tiles(rel_bias, t_attn)
    mem2 = mem.reshape(b * m, d)
    lbw = jax.nn.softmax(hgrn_lower_bound.astype(_F32), axis=0)
    lb_all = jnp.cumsum(lbw, axis=0) - lbw[0]

    for layer in range(DEPTH):
        kind, j = layer % N_MIXERS, layer // N_MIXERS
        x2 = x.reshape(n, d)
        kv = _matmul(mem2, mem_w_kv[layer].astype(_BF), _BF, _tile(b * m, ROW_TILE), 2 * XATTN_W)
        kv = kv.reshape(b, m, 2 * XATTN_W)
        if kind == 0:
            q_scale = jnp.where(jnp.arange(attn_w_in.shape[2]) < MIX_W, DA_HD ** -0.5 * LOG2E, 1.0)
            w_in = (attn_w_in[j] * q_scale).astype(_BF)
            p = _matmul(x2, w_in, _BF, tm, w_in.shape[1] // 2).reshape(b, s, -1)
            lp = attn_lambda[j].astype(_F32)
            lam_init = 0.8 - 0.6 * math.exp(-0.3 * layer)
            lam = jnp.exp(jnp.sum(lp[0] * lp[1])) - jnp.exp(jnp.sum(lp[2] * lp[3])) + lam_init
            vt = jnp.swapaxes(p[..., 2 * MIX_W:3 * MIX_W], 1, 2)
            mixed = _diff_attention(p, vt, lam.reshape(1),
                                    attn_subln[j].astype(_F32).reshape(1, DA_W),
                                    bias, far, layer, t_attn)
            q_col = 3 * MIX_W // XATTN_W
        elif kind == 1:
            w_in = hgrn_w_in[j].astype(_BF)
            p = _matmul(x2, w_in, _F32, tm, w_in.shape[1] // 2).reshape(b, s, -1)
            q, i_, g, f_fw, f_bw = (p[..., c * MIX_W:(c + 1) * MIX_W] for c in range(5))
            qd = jnp.concatenate([q, jnp.flip(q, 1)], axis=0)
            vd = jnp.concatenate([i_, jnp.flip(i_, 1)], axis=0)
            zd = jnp.concatenate([f_fw, jnp.flip(f_bw, 1)], axis=0)
            o = _hgrn_scan(qd, vd, zd, lb_all[layer].reshape(2, 1, MIX_W), hg_rows)
            o_fw = o[:b].reshape(n, MIX_W)
            o_bw = jnp.flip(o[b:], 1).reshape(n, MIX_W)
            mixed = _hgrn_finish(o_fw, o_bw, p.reshape(n, -1), 2,
                                 hgrn_norm[j].astype(_F32).reshape(1, MIX_W), tm)
            q_col = 5 * MIX_W // XATTN_W
        else:
            w_in = conv_w_in[j].astype(_BF)
            p = _matmul(x2, w_in, _BF, tm, w_in.shape[1] // 2).reshape(b, s, -1)
            mixed = _short_conv(p, conv_w[j].astype(_F32), tm)
            q_col = 3 * MIX_W // XATTN_W
        recalled = _memory_attention(p, kv, q_col, tm)
        x2 = _out_ln(x2, mixed.reshape(n, MIX_W), recalled.reshape(n, XATTN_W),
                     w_o[layer].astype(_BF), ln_gain[layer, 0].reshape(1, d),
                     ln_bias[layer, 0].reshape(1, d), tm)
        x = _conv_ffn_ln(x2.reshape(b, s, d), ffn_w_up[layer].astype(_BF),
                         ffn_conv[layer].astype(_F32), ffn_w_down[layer].astype(_BF),
                         ln_gain[layer, 1].reshape(1, d), ln_bias[layer, 1].reshape(1, d),
                         tm, FFN_TILE)
    return x
```

```python
import functools
import math

import jax
import jax.numpy as jnp
from jax import lax
from jax.experimental import pallas as pl
from jax.experimental.pallas import tpu as pltpu

D_MODEL = 1024
DEPTH = 4
N_MIXERS = 3
MIX_W = 3 * D_MODEL // 4
XATTN_HEADS = 4
XATTN_W = D_MODEL - MIX_W
XATTN_HD = XATTN_W // XATTN_HEADS
DA_HD = 64
DA_HEADS = MIX_W // (2 * DA_HD)
DA_W = 2 * DA_HD
N_BUCKETS = 32
MAX_DISTANCE = 128
HG_DK = 128
HG_HEADS = MIX_W // HG_DK
HG_CHUNK = 64
HG_SUB = 16
D_FF = ((8 * D_MODEL // 3 + 255) // 256) * 256
LN_EPS = 1e-5
ALPHA = (2 * DEPTH) ** 0.25
LOG2E = math.log2(math.e)

LANES = 128
SUBLANES_F32 = 8
SUBLANES_BF16 = 16
VMEM_LIMIT = 52 * 1024 * 1024

ROW_TILE = 512
ATTN_Q_TILE = 256
ATTN_K_TILE = 1024
FFN_TILE = D_FF // 2
HG_ROWS = 512

_NEG = -1e30
_BF = jnp.bfloat16
_F32 = jnp.float32
_NT = (((1,), (1,)), ((), ()))


def _params(*sem):
    return pltpu.CompilerParams(dimension_semantics=sem, vmem_limit_bytes=VMEM_LIMIT)


def _sigmoid(z):
    e = jnp.exp(-jnp.abs(z))
    r = 1.0 / (1.0 + e)
    return jnp.where(z >= 0, r, e * r)


def _silu(z):
    return z * _sigmoid(z)


def _layer_norm(r, g, b):
    mu = jnp.mean(r, axis=-1, keepdims=True)
    rc = r - mu
    var = jnp.mean(rc * rc, axis=-1, keepdims=True)
    return rc * lax.rsqrt(var + LN_EPS) * g + b


def _mm_kernel(x_ref, w_ref, o_ref):
    o_ref[...] = jnp.dot(x_ref[...].astype(_BF), w_ref[...],
                         preferred_element_type=_F32).astype(o_ref.dtype)


def _matmul(x, w, out_dtype, tm, tn):
    m, k = x.shape
    n = w.shape[1]
    return pl.pallas_call(
        _mm_kernel,
        grid=(m // tm, n // tn),
        in_specs=[pl.BlockSpec((tm, k), lambda i, j: (i, 0)),
                  pl.BlockSpec((k, tn), lambda i, j: (0, j))],
        out_specs=pl.BlockSpec((tm, tn), lambda i, j: (i, j)),
        out_shape=jax.ShapeDtypeStruct((m, n), out_dtype),
        compiler_params=_params("parallel", "arbitrary"),
        name="proj",
    )(x, w)


def _memattn_kernel(q_ref, km_ref, vm_ref, o_ref):
    q = q_ref[0].astype(_BF)
    km = km_ref[0]
    vm = vm_ref[0]
    lane = lax.broadcasted_iota(jnp.int32, (1, XATTN_W), 1)
    acc = jnp.zeros(q.shape, _F32)
    for h in range(XATTN_HEADS):
        head = (lane >= h * XATTN_HD) & (lane < (h + 1) * XATTN_HD)
        qh = jnp.where(head, q, jnp.zeros_like(q))
        s = lax.dot_general(qh, km, _NT, preferred_element_type=_F32) * (XATTN_HD ** -0.5)
        p = jnp.exp(s - jnp.max(s, axis=-1, keepdims=True))
        l = jnp.sum(p, axis=-1, keepdims=True)
        vh = jnp.where(head, vm, jnp.zeros_like(vm))
        acc = acc + jnp.dot(p.astype(_BF), vh, preferred_element_type=_F32) / l
    o_ref[0] = acc.astype(o_ref.dtype)


def _memory_attention(p3, kv, q_col, tm):
    b, s, _ = p3.shape
    m = kv.shape[1]
    return pl.pallas_call(
        _memattn_kernel,
        grid=(b, s // tm),
        in_specs=[pl.BlockSpec((1, tm, XATTN_W), lambda bi, i: (bi, i, q_col)),
                  pl.BlockSpec((1, m, XATTN_W), lambda bi, i: (bi, 0, 0)),
                  pl.BlockSpec((1, m, XATTN_W), lambda bi, i: (bi, 0, 1))],
        out_specs=pl.BlockSpec((1, tm, XATTN_W), lambda bi, i: (bi, i, 0)),
        out_shape=jax.ShapeDtypeStruct((b, s, XATTN_W), _BF),
        compiler_params=_params("parallel", "parallel"),
        name="memattn",
    )(p3, kv, kv)


def _out_ln_kernel(x_ref, mix_ref, rec_ref, wo_ref, g_ref, b_ref, o_ref):
    y = jnp.dot(mix_ref[...], wo_ref[:MIX_W, :], preferred_element_type=_F32)
    y = y + jnp.dot(rec_ref[...], wo_ref[MIX_W:, :], preferred_element_type=_F32)
    o_ref[...] = _layer_norm(ALPHA * x_ref[...] + y, g_ref[...], b_ref[...])


def _out_ln(x2, mixed, recalled, wo, g, b, tm):
    n = x2.shape[0]
    return pl.pallas_call(
        _out_ln_kernel,
        grid=(n // tm,),
        in_specs=[pl.BlockSpec((tm, D_MODEL), lambda i: (i, 0)),
                  pl.BlockSpec((tm, MIX_W), lambda i: (i, 0)),
                  pl.BlockSpec((tm, XATTN_W), lambda i: (i, 0)),
                  pl.BlockSpec((D_MODEL, D_MODEL), lambda i: (0, 0)),
                  pl.BlockSpec((1, D_MODEL), lambda i: (0, 0)),
                  pl.BlockSpec((1, D_MODEL), lambda i: (0, 0))],
        out_specs=pl.BlockSpec((tm, D_MODEL), lambda i: (i, 0)),
        out_shape=jax.ShapeDtypeStruct((n, D_MODEL), _F32),
        compiler_params=_params("parallel"),
        name="out_ln",
    )(x2, mixed, recalled, wo, g, b)


def _dwconv_rows(h, w):
    n = h.shape[0]
    return (pltpu.roll(h, 1, 0) * w[0:1] + h * w[1:2] + pltpu.roll(h, n - 1, 0) * w[2:3])


def _ffn_kernel(x_ref, xp_ref, xn_ref, wa_ref, wv_ref, ca_ref, cv_ref, wd_ref, g_ref, b_ref,
                o_ref, xb_ref, acc_ref, *, ts, halo):
    i = pl.program_id(1)
    j = pl.program_id(2)

    @pl.when(j == 0)
    def _():
        prev = jnp.where(i == 0, 0.0, xp_ref[0])
        nxt = jnp.where(i == pl.num_programs(1) - 1, 0.0, xn_ref[0])
        xb_ref[0:halo, :] = prev.astype(_BF)
        xb_ref[halo:halo + ts, :] = x_ref[0].astype(_BF)
        xb_ref[halo + ts:, :] = nxt.astype(_BF)
        acc_ref[...] = jnp.zeros_like(acc_ref)

    xb = xb_ref[...]
    ha = jnp.dot(xb, wa_ref[...], preferred_element_type=_F32)
    hv = jnp.dot(xb, wv_ref[...], preferred_element_type=_F32)
    a = _dwconv_rows(ha, ca_ref[...])[halo:halo + ts]
    v = _dwconv_rows(hv, cv_ref[...])[halo:halo + ts]
    gated = (_silu(a) * v).astype(_BF)
    acc_ref[...] += jnp.dot(gated, wd_ref[...], preferred_element_type=_F32)

    @pl.when(j == pl.num_programs(2) - 1)
    def _():
        o_ref[0] = _layer_norm(ALPHA * x_ref[0] + acc_ref[...], g_ref[...], b_ref[...])


def _conv_ffn_ln(x3, w_up, w_conv, w_down, g, b, ts, fb):
    bsz, s, d = x3.shape
    halo = SUBLANES_BF16
    nj = D_FF // fb
    hb = ts // halo
    last = s // halo - 1
    kern = functools.partial(_ffn_kernel, ts=ts, halo=halo)
    return pl.pallas_call(
        kern,
        grid=(bsz, s // ts, nj),
        in_specs=[pl.BlockSpec((1, ts, d), lambda bi, i, j: (bi, i, 0)),
                  pl.BlockSpec((1, halo, d), lambda bi, i, j: (bi, jnp.maximum(i * hb - 1, 0), 0)),
                  pl.BlockSpec((1, halo, d), lambda bi, i, j: (bi, jnp.minimum((i + 1) * hb, last), 0)),
                  pl.BlockSpec((d, fb), lambda bi, i, j: (0, j)),
                  pl.BlockSpec((d, fb), lambda bi, i, j: (0, nj + j)),
                  pl.BlockSpec((3, fb), lambda bi, i, j: (0, j)),
                  pl.BlockSpec((3, fb), lambda bi, i, j: (0, nj + j)),
                  pl.BlockSpec((fb, d), lambda bi, i, j: (j, 0)),
                  pl.BlockSpec((1, d), lambda bi, i, j: (0, 0)),
                  pl.BlockSpec((1, d), lambda bi, i, j: (0, 0))],
        out_specs=pl.BlockSpec((1, ts, d), lambda bi, i, j: (bi, i, 0)),
        out_shape=jax.ShapeDtypeStruct((bsz, s, d), _F32),
        scratch_shapes=[pltpu.VMEM((ts + 2 * halo, d), _BF), pltpu.VMEM((ts, d), _F32)],
        compiler_params=_params("parallel", "parallel", "arbitrary"),
        name="conv_ffn",
    )(x3, x3, x3, w_up, w_up, w_conv, w_conv, w_down, g, b)


def _fold_rows(x, op, group=SUBLANES_F32):
    acc = x[:group]
    for g in range(1, x.shape[0] // group):
        acc = op(acc, x[g * group:(g + 1) * group])
    while acc.shape[0] > SUBLANES_F32:
        half = acc.shape[0] // 2
        acc = op(acc[:half], acc[half:])
    return acc


def _attn_kernel(lam_ref, far_ref, q_ref, k_ref, vt_ref, bias_ref, subln_ref, o_ref,
                 s_a, s_b, acc_ref, *, tq, tk, out_scale):
    h = pl.program_id(1)
    i = pl.program_id(2)
    nkv = k_ref.shape[1] // tk
    lane = lax.broadcasted_iota(jnp.int32, (1, DA_W), 1)
    q = q_ref[0]
    zero = jnp.zeros_like(q)
    qs = (jnp.where(lane < DA_HD, q, zero), jnp.where(lane >= DA_HD, q, zero))

    unit, u_min, u_max = _near_offsets(tq, tk)
    far_left = far_ref[3 * h]
    far_right = far_ref[3 * h + 1]
    bias_max = far_ref[3 * h + 2]

    def offset(j):
        return (tk * j - tq * i) // unit

    def side_select(j, left, mid, right):
        u = offset(j)
        return jnp.where(u < u_min, left, jnp.where(u > u_max, right, mid))

    def scores(j, s_out, mi):
        kj = k_ref[0, pl.ds(pl.multiple_of(j * tk, tk), tk), :]
        s = lax.dot_general(kj, qs[mi], _NT, preferred_element_type=_F32)
        s_out[mi] = s
        return jnp.max(_fold_rows(s, jnp.maximum), axis=0, keepdims=True)

    def max_bound(j):
        return side_select(j, far_left, bias_max, far_right)

    def add_near_bias(j, s_out):
        u = offset(j)

        @pl.when((u >= u_min) & (u <= u_max))
        def _():
            tile = bias_ref[u - u_min, 0]
            for mi in range(2):
                s_out[mi] = s_out[mi] + tile

    def half_step(j, carry, s_cur, s_nxt):
        jn = jnp.minimum(j + 1, nkv - 1)
        bound_n = max_bound(jn)
        shift = side_select(j, far_left, 0.0, far_right)
        new = []
        for mi in range(2):
            m = carry[mi][0]
            m_nxt = jnp.maximum(m, scores(jn, s_nxt, mi) + bound_n)
            new.append((m_nxt, jnp.exp2(m - m_nxt)))
        vtj = vt_ref[0, 0, :, pl.ds(pl.multiple_of(j * tk, tk), tk)]
        for mi in range(2):
            m, alpha = carry[mi]
            p = jnp.exp2((s_cur[mi] - (m - shift)).astype(_BF))
            acc_ref[mi] = alpha * acc_ref[mi] + jnp.dot(vtj, p, preferred_element_type=_F32)
        add_near_bias(jn, s_nxt)
        return tuple(new)

    acc_ref[...] = jnp.zeros_like(acc_ref)
    init = tuple((scores(0, s_a, mi) + max_bound(0), jnp.zeros((1, tq), _F32)) for mi in range(2))
    add_near_bias(0, s_a)

    def pair(jj, carry):
        carry = half_step(2 * jj, carry, s_a, s_b)
        return half_step(2 * jj + 1, carry, s_b, s_a)

    lax.fori_loop(0, nkv // 2, pair, init)

    lam = lam_ref[0]
    o0 = acc_ref[0, :DA_W] / acc_ref[0, DA_W:DA_W + 1]
    o1 = acc_ref[1, :DA_W] / acc_ref[1, DA_W:DA_W + 1]
    ot = o0 - lam * o1
    o = ot.T
    ms = jnp.mean(o * o, axis=-1, keepdims=True)
    o_ref[0] = (o * lax.rsqrt(ms + LN_EPS) * subln_ref[...] * out_scale).astype(o_ref.dtype)


def _t5_bucket(rel):
    half = N_BUCKETS // 2
    max_exact = half // 2
    ret = jnp.where(rel > 0, half, 0)
    n = jnp.abs(rel)
    nf = jnp.maximum(n, 1).astype(_F32)
    large = max_exact + (jnp.log(nf / max_exact) / math.log(MAX_DISTANCE / max_exact)
                         * (half - max_exact)).astype(jnp.int32)
    large = jnp.minimum(large, half - 1)
    return ret + jnp.where(n < max_exact, n, large)


def _near_offsets(tq, tk):
    unit = math.gcd(tq, tk)
    first = -((tk + MAX_DISTANCE - 2) // unit)
    last = (tq + MAX_DISTANCE - 2) // unit
    return unit, first, last


def _bias_tiles(rel_bias, tq, tk):
    unit, u_min, u_max = _near_offsets(tq, tk)
    n_near = u_max - u_min + 1
    c = jnp.arange(tk, dtype=jnp.int32)[:, None]
    r = jnp.arange(tq, dtype=jnp.int32)[None, :]
    u = jnp.arange(u_min, u_max + 1, dtype=jnp.int32)[:, None, None]
    table = rel_bias.astype(_F32) * LOG2E
    bucket = _t5_bucket(u * unit + c - r)[:, None]
    tiles = jnp.zeros((n_near, DA_HEADS, tk, tq), _F32)
    for b in range(N_BUCKETS):
        tiles = jnp.where(bucket == b, table[b][None, :, None, None], tiles)
    far =table[_t5_bucket(jnp.array([-MAX_DISTANCE, MAX_DISTANCE], jnp.int32))]
    consts = jnp.concatenate([far, jnp.max(table, axis=0, keepdims=True)], axis=0)
    return tiles, consts.T.reshape(-1)


def _diff_attention(p3, vt, lam, subln, bias, far, layer, tq, tk):
    b, s, _ = p3.shape
    vrows = vt.shape[2]
    assert s % (2 * tk) == 0
    lam_init = 0.8 - 0.6 * math.exp(-0.3 * layer)
    kern = functools.partial(_attn_kernel, tq=tq, tk=tk, out_scale=1.0 - lam_init)
    return pl.pallas_call(
        kern,
        grid=(b, DA_HEADS, s // tq),
        in_specs=[pl.BlockSpec(memory_space=pltpu.SMEM),
                  pl.BlockSpec(memory_space=pltpu.SMEM),
                  pl.BlockSpec((1, tq, DA_W), lambda bi, h, i: (bi, i, h)),
                  pl.BlockSpec((1, s, DA_W), lambda bi, h, i: (bi, 0, DA_HEADS + h)),
                  pl.BlockSpec((1, 1, vrows, s), lambda bi, h, i: (bi, h, 0, 0)),
                  pl.BlockSpec((bias.shape[0], 1, tk, tq), lambda bi, h, i: (0, h, 0, 0)),
                  pl.BlockSpec((1, DA_W), lambda bi, h, i: (0, 0))],
        out_specs=pl.BlockSpec((1, tq, DA_W), lambda bi, h, i: (bi, i, h)),
        out_shape=jax.ShapeDtypeStruct((b, s, MIX_W), _BF),
        scratch_shapes=[pltpu.VMEM((2, tk, tq), _F32), pltpu.VMEM((2, tk, tq), _F32),
                        pltpu.VMEM((2, vrows, tq), _F32)],
        compiler_params=_params("parallel", "parallel", "parallel"),
        name="diff_attn",
    )(lam, far, p3, p3, vt, bias, subln)


def _hgrn_kernel(q_ref, v_ref, z_ref, lb_ref, o_ref, st_ref, *, rows):
    @pl.when(pl.program_id(1) == 0)
    def _():
        st_ref[...] = jnp.zeros_like(st_ref)

    c_rows = HG_CHUNK
    row = lax.broadcasted_iota(jnp.int32, (c_rows, c_rows), 0)
    col = lax.broadcasted_iota(jnp.int32, (c_rows, c_rows), 1)
    tri = (col <= row).astype(_BF)
    n_sub = c_rows // HG_SUB

    def chunk(c, carry):
        r0 = pl.multiple_of(c * c_rows, c_rows)
        for h in range(HG_HEADS):
            cs = slice(h * HG_DK, (h + 1) * HG_DK)
            z = z_ref[0, pl.ds(r0, c_rows), cs]
            lb = lb_ref[0, :, cs]
            sg = _sigmoid(z)
            f = lb + (1.0 - lb) * sg
            kk = (1.0 - lb) * _sigmoid(-z)
            logf = jnp.log(f)
            hi = logf.astype(_BF)
            lo = (logf - hi.astype(_F32)).astype(_BF)
            cum = (jnp.dot(tri, hi, preferred_element_type=_F32)
                   + jnp.dot(tri, lo, preferred_element_type=_F32))
            last = cum[c_rows - 1:c_rows, :]
            qh = _silu(q_ref[0, pl.ds(r0, c_rows), cs]) * (HG_DK ** -0.5)
            v = v_ref[0, pl.ds(r0, c_rows), cs]
            vb = v.astype(_BF)
            st = st_ref[h]
            o_inter = lax.dot_general((qh * jnp.exp(cum)).astype(_BF), st.astype(_BF), _NT,
                                      preferred_element_type=_F32)
            parts = []
            for b in range(n_sub):
                lo_r, hi_r = b * HG_SUB, (b + 1) * HG_SUB
                base = cum[lo_r - 1:lo_r, :] if b else jnp.zeros_like(last)
                qq = (qh[lo_r:hi_r] * jnp.exp(cum[lo_r:hi_r] - base)).astype(_BF)
                kt = (kk[:hi_r] * jnp.exp(base - cum[:hi_r])).astype(_BF)
                a = lax.dot_general(qq, kt, _NT, preferred_element_type=_F32)
                sub_row = lax.broadcasted_iota(jnp.int32, a.shape, 0) + lo_r
                sub_col = lax.broadcasted_iota(jnp.int32, a.shape, 1)
                a = jnp.where(sub_col <= sub_row, a, 0.0)
                parts.append(o_inter[lo_r:hi_r]
                             + jnp.dot(a.astype(_BF), vb[:hi_r], preferred_element_type=_F32))
            o_ref[0, pl.ds(r0, c_rows), cs] = jnp.concatenate(parts, axis=0)
            ks = (kk * jnp.exp(last - cum)).astype(_BF)
            st_ref[h] = st * jnp.exp(last) + jnp.dot(v.T.astype(_BF), ks,
                                                     preferred_element_type=_F32)
        return carry

    lax.fori_loop(0, rows // c_rows, chunk, 0)


def _hgrn_scan(qd, vd, zd, lb, rows):
    n, s, w = qd.shape
    bsz = n // 2
    spec = pl.BlockSpec((1, rows, w), lambda ni, i: (ni, i, 0))
    return pl.pallas_call(
        functools.partial(_hgrn_kernel, rows=rows),
        grid=(n, s // rows),
        in_specs=[spec, spec, spec, pl.BlockSpec((1, 1, w), lambda ni, i: (ni // bsz, 0, 0))],
        out_specs=spec,
        out_shape=jax.ShapeDtypeStruct((n, s, w), _F32),
        scratch_shapes=[pltpu.VMEM((HG_HEADS, HG_DK, HG_DK), _F32)],
        compiler_params=_params("parallel", "arbitrary"),
        name="hgrn_scan",
    )(qd, vd, zd, lb)


def _hgrn_finish_kernel(of_ref, ob_ref, g_ref, w_ref, o_ref):
    o = of_ref[...] + ob_ref[...]
    gate = _silu(g_ref[...])
    w = w_ref[...]
    for h in range(HG_HEADS):
        cs = slice(h * HG_DK, (h + 1) * HG_DK)
        oh = o[:, cs]
        ms = jnp.mean(oh * oh, axis=-1, keepdims=True)
        o_ref[:, cs] = (oh * lax.rsqrt(ms + LN_EPS) * w[:, cs] * gate[:, cs]).astype(o_ref.dtype)


def _hgrn_finish(o_fw, o_bw, p2, g_col, norm_w, tm):
    n = o_fw.shape[0]
    return pl.pallas_call(
        _hgrn_finish_kernel,
        grid=(n // tm,),
        in_specs=[pl.BlockSpec((tm, MIX_W), lambda i: (i, 0)),
                  pl.BlockSpec((tm, MIX_W), lambda i: (i, 0)),
                  pl.BlockSpec((tm, MIX_W), lambda i: (i, g_col)),
                  pl.BlockSpec((1, MIX_W), lambda i: (0, 0))],
        out_specs=pl.BlockSpec((tm, MIX_W), lambda i: (i, 0)),
        out_shape=jax.ShapeDtypeStruct((n, MIX_W), _BF),
        compiler_params=_params("parallel"),
        name="hgrn_finish",
    )(o_fw, o_bw, p2, norm_w)


def _sconv_kernel(gb_ref, gc_ref, h_ref, gcp_ref, hp_ref, gcn_ref, hn_ref, w_ref, o_ref, *, halo):
    i = pl.program_id(1)
    u = gc_ref[0].astype(_F32) * h_ref[0].astype(_F32)
    n = u.shape[0]
    u_prev = (gcp_ref[0].astype(_F32) * hp_ref[0].astype(_F32))[halo - 1:halo]
    u_next = (gcn_ref[0].astype(_F32) * hn_ref[0].astype(_F32))[0:1]
    u_prev = jnp.where(i == 0, 0.0, u_prev)
    u_next = jnp.where(i == pl.num_programs(1) - 1, 0.0, u_next)
    row = lax.broadcasted_iota(jnp.int32, u.shape, 0)
    down = jnp.where(row == 0, u_prev, pltpu.roll(u, 1, 0))
    up = jnp.where(row == n - 1, u_next, pltpu.roll(u, n - 1, 0))
    w = w_ref[...]
    y = down * w[0:1] + u * w[1:2] + up * w[2:3]
    o_ref[0] = (gb_ref[0].astype(_F32) * y).astype(o_ref.dtype)


def _short_conv(p3, conv_w, ts):
    b, s, _ = p3.shape
    halo = SUBLANES_BF16
    hb = ts // halo
    last = s // halo - 1
    main = lambda c: pl.BlockSpec((1, ts, MIX_W), lambda bi, i: (bi, i, c))
    prev = lambda c: pl.BlockSpec((1, halo, MIX_W), lambda bi, i: (bi, jnp.maximum(i * hb - 1, 0), c))
    nxt = lambda c: pl.BlockSpec((1, halo, MIX_W), lambda bi, i: (bi, jnp.minimum((i + 1) * hb, last), c))
    return pl.pallas_call(
        functools.partial(_sconv_kernel, halo=halo),
        grid=(b, s // ts),
        in_specs=[main(0), main(1), main(2), prev(1), prev(2), nxt(1), nxt(2),
                  pl.BlockSpec((3, MIX_W), lambda bi, i: (0, 0))],
        out_specs=pl.BlockSpec((1, ts, MIX_W), lambda bi, i: (bi, i, 0)),
        out_shape=jax.ShapeDtypeStruct((b, s, MIX_W), _BF),
        compiler_params=_params("parallel", "parallel"),
        name="short_conv",
    )(p3, p3, p3, p3, p3, p3, p3, conv_w)


def _tile(n, pref):
    return pref if n % pref == 0 else n


def kernel(x, mem, rel_bias, attn_w_in, attn_lambda, attn_subln, hgrn_w_in, hgrn_lower_bound,
           hgrn_norm, conv_w_in, conv_w, mem_w_kv, w_o, ln_gain, ln_bias, ffn_w_up, ffn_conv,
           ffn_w_down):
    b, s, d = x.shape
    n = b * s
    m = mem.shape[1]
    tm = _tile(s, ROW_TILE)
    hg_rows = _tile(s, HG_ROWS)

    bias, far = _bias_tiles(rel_bias, ATTN_Q_TILE, ATTN_K_TILE)
    mem2 = mem.reshape(b * m, d)
    lbw = jax.nn.softmax(hgrn_lower_bound.astype(_F32), axis=0)
    lb_all = jnp.cumsum(lbw, axis=0) - lbw[0]

    for layer in range(DEPTH):
        kind, j = layer % N_MIXERS, layer // N_MIXERS
        x2 = x.reshape(n, d)
        kv = _matmul(mem2, mem_w_kv[layer].astype(_BF), _BF, _tile(b * m, ROW_TILE), 2 * XATTN_W)
        kv = kv.reshape(b, m, 2 * XATTN_W)
        if kind == 0:
            q_scale = jnp.where(jnp.arange(attn_w_in.shape[2]) < MIX_W, DA_HD ** -0.5 * LOG2E, 1.0)
            w_in = (attn_w_in[j] * q_scale).astype(_BF)
            p = _matmul(x2, w_in, _BF, tm, w_in.shape[1] // 2).reshape(b, s, -1)
            lp = attn_lambda[j].astype(_F32)
            lam_init = 0.8 - 0.6 * math.exp(-0.3 * layer)
            lam = jnp.exp(jnp.sum(lp[0] * lp[1])) - jnp.exp(jnp.sum(lp[2] * lp[3])) + lam_init
            vt = jnp.swapaxes(p[..., 2 * MIX_W:3 * MIX_W], 1, 2).reshape(b, DA_HEADS, DA_W, s)
            vt = jnp.concatenate([vt, jnp.ones((b, DA_HEADS, SUBLANES_BF16, s), _BF)], axis=2)
            mixed = _diff_attention(p, vt, lam.reshape(1),
                                    attn_subln[j].astype(_F32).reshape(1, DA_W),
                                    bias, far, layer, ATTN_Q_TILE, ATTN_K_TILE)
            q_col = 3 * MIX_W // XATTN_W
        elif kind == 1:
            w_in = hgrn_w_in[j].astype(_BF)
            p = _matmul(x2, w_in, _F32, tm, w_in.shape[1] // 2).reshape(b, s, -1)
            q, i_, g, f_fw, f_bw = (p[..., c * MIX_W:(c + 1) * MIX_W] for c in range(5))
            qd = jnp.concatenate([q, jnp.flip(q, 1)], axis=0)
            vd = jnp.concatenate([i_, jnp.flip(i_, 1)], axis=0)
            zd = jnp.concatenate([f_fw, jnp.flip(f_bw, 1)], axis=0)
            o = _hgrn_scan(qd, vd, zd, lb_all[layer].reshape(2, 1, MIX_W), hg_rows)
            o_fw = o[:b].reshape(n, MIX_W)
            o_bw = jnp.flip(o[b:], 1).reshape(n, MIX_W)
            mixed = _hgrn_finish(o_fw, o_bw, p.reshape(n, -1), 2,
                                 hgrn_norm[j].astype(_F32).reshape(1, MIX_W), tm)
            q_col = 5 * MIX_W // XATTN_W
        else:
            w_in = conv_w_in[j].astype(_BF)
            p = _matmul(x2, w_in, _BF, tm, w_in.shape[1] // 2).reshape(b, s, -1)
            mixed = _short_conv(p, conv_w[j].astype(_F32), tm)
            q_col = 3 * MIX_W // XATTN_W
        recalled = _memory_attention(p, kv, q_col, tm)
        x2 = _out_ln(x2, mixed.reshape(n, MIX_W), recalled.reshape(n, XATTN_W),
                     w_o[layer].astype(_BF), ln_gain[layer, 0].reshape(1, d),
                     ln_bias[layer, 0].reshape(1, d), tm)
        x = _conv_ffn_ln(x2.reshape(b, s, d), ffn_w_up[layer].astype(_BF),
                         ffn_conv[layer].astype(_F32), ffn_w_down[layer].astype(_BF),
                         ln_gain[layer, 1].reshape(1, d), ln_bias[layer, 1].reshape(1, d),
                         tm, FFN_TILE)
    return x
```

```python
import functools
import math

import jax
import jax.numpy as jnp
from jax import lax
from jax.experimental import pallas as pl
from jax.experimental.pallas import tpu as pltpu

D_MODEL = 1024
DEPTH = 4
N_MIXERS = 3
MIX_W = 3 * D_MODEL // 4
XATTN_HEADS = 4
XATTN_W = D_MODEL - MIX_W
XATTN_HD = XATTN_W // XATTN_HEADS
DA_HD = 64
DA_HEADS = MIX_W // (2 * DA_HD)
DA_W = 2 * DA_HD
N_BUCKETS = 32
MAX_DISTANCE = 128
HG_DK = 128
HG_HEADS = MIX_W // HG_DK
HG_CHUNK = 64
HG_SUB = 16
D_FF = ((8 * D_MODEL // 3 + 255) // 256) * 256
LN_EPS = 1e-5
ALPHA = (2 * DEPTH) ** 0.25
LOG2E = math.log2(math.e)

LANES = 128
SUBLANES_F32 = 8
SUBLANES_BF16 = 16
VMEM_LIMIT = 52 * 1024 * 1024

ROW_TILE = 512
ATTN_Q_TILE = 256
ATTN_K_TILE = 1024
FFN_TILE = D_FF // 2
HG_ROWS = 512
HG_HEAD_GROUP = 3

_NEG = -1e30
_BF = jnp.bfloat16
_F32 = jnp.float32
_NT = (((1,), (1,)), ((), ()))


def _params(*sem):
    return pltpu.CompilerParams(dimension_semantics=sem, vmem_limit_bytes=VMEM_LIMIT)


def _sigmoid(z):
    e = jnp.exp(-jnp.abs(z))
    r = 1.0 / (1.0 + e)
    return jnp.where(z >= 0, r, e * r)


def _silu(z):
    return z * _sigmoid(z)


def _layer_norm(r, g, b):
    mu = jnp.mean(r, axis=-1, keepdims=True)
    rc = r - mu
    var = jnp.mean(rc * rc, axis=-1, keepdims=True)
    return rc * lax.rsqrt(var + LN_EPS) * g + b


def _mm_kernel(x_ref, w_ref, o_ref):
    o_ref[...] = jnp.dot(x_ref[...].astype(_BF), w_ref[...],
                         preferred_element_type=_F32).astype(o_ref.dtype)


def _matmul(x, w, out_dtype, tm, tn):
    m, k = x.shape
    n = w.shape[1]
    return pl.pallas_call(
        _mm_kernel,
        grid=(m // tm, n // tn),
        in_specs=[pl.BlockSpec((tm, k), lambda i, j: (i, 0)),
                  pl.BlockSpec((k, tn), lambda i, j: (0, j))],
        out_specs=pl.BlockSpec((tm, tn), lambda i, j: (i, j)),
        out_shape=jax.ShapeDtypeStruct((m, n), out_dtype),
        compiler_params=_params("parallel", "arbitrary"),
        name="proj",
    )(x, w)


def _memattn_kernel(q_ref, km_ref, vm_ref, o_ref):
    q = q_ref[0].astype(_BF)
    km = km_ref[0]
    vm = vm_ref[0]
    lane = lax.broadcasted_iota(jnp.int32, (1, XATTN_W), 1)
    acc = jnp.zeros(q.shape, _F32)
    for h in range(XATTN_HEADS):
        head = (lane >= h * XATTN_HD) & (lane < (h + 1) * XATTN_HD)
        qh = jnp.where(head, q, jnp.zeros_like(q))
        s = lax.dot_general(qh, km, _NT, preferred_element_type=_F32) * (XATTN_HD ** -0.5)
        p = jnp.exp(s - jnp.max(s, axis=-1, keepdims=True))
        l = jnp.sum(p, axis=-1, keepdims=True)
        vh = jnp.where(head, vm, jnp.zeros_like(vm))
        acc = acc + jnp.dot(p.astype(_BF), vh, preferred_element_type=_F32) / l
    o_ref[0] = acc.astype(o_ref.dtype)


def _memory_attention(p3, kv, q_col, tm):
    b, s, _ = p3.shape
    m = kv.shape[1]
    return pl.pallas_call(
        _memattn_kernel,
        grid=(b, s // tm),
        in_specs=[pl.BlockSpec((1, tm, XATTN_W), lambda bi, i: (bi, i, q_col)),
                  pl.BlockSpec((1, m, XATTN_W), lambda bi, i: (bi, 0, 0)),
                  pl.BlockSpec((1, m, XATTN_W), lambda bi, i: (bi, 0, 1))],
        out_specs=pl.BlockSpec((1, tm, XATTN_W), lambda bi, i: (bi, i, 0)),
        out_shape=jax.ShapeDtypeStruct((b, s, XATTN_W), _BF),
        compiler_params=_params("parallel", "parallel"),
        name="memattn",
    )(p3, kv, kv)


def _out_ln_kernel(x_ref, mix_ref, rec_ref, wo_ref, g_ref, b_ref, o_ref):
    y = jnp.dot(mix_ref[...], wo_ref[:MIX_W, :], preferred_element_type=_F32)
    y = y + jnp.dot(rec_ref[...], wo_ref[MIX_W:, :], preferred_element_type=_F32)
    o_ref[...] = _layer_norm(ALPHA * x_ref[...] + y, g_ref[...], b_ref[...])


def _out_ln(x2, mixed, recalled, wo, g, b, tm):
    n = x2.shape[0]
    return pl.pallas_call(
        _out_ln_kernel,
        grid=(n // tm,),
        in_specs=[pl.BlockSpec((tm, D_MODEL), lambda i: (i, 0)),
                  pl.BlockSpec((tm, MIX_W), lambda i: (i, 0)),
                  pl.BlockSpec((tm, XATTN_W), lambda i: (i, 0)),
                  pl.BlockSpec((D_MODEL, D_MODEL), lambda i: (0, 0)),
                  pl.BlockSpec((1, D_MODEL), lambda i: (0, 0)),
                  pl.BlockSpec((1, D_MODEL), lambda i: (0, 0))],
        out_specs=pl.BlockSpec((tm, D_MODEL), lambda i: (i, 0)),
        out_shape=jax.ShapeDtypeStruct((n, D_MODEL), _F32),
        compiler_params=_params("parallel"),
        name="out_ln",
    )(x2, mixed, recalled, wo, g, b)


def _dwconv_rows(h, w):
    n = h.shape[0]
    return (pltpu.roll(h, 1, 0) * w[0:1] + h * w[1:2] + pltpu.roll(h, n - 1, 0) * w[2:3])


def _ffn_kernel(x_ref, xp_ref, xn_ref, wa_ref, wv_ref, ca_ref, cv_ref, wd_ref, g_ref, b_ref,
                o_ref, xb_ref, acc_ref, *, ts, halo):
    i = pl.program_id(1)
    j = pl.program_id(2)

    @pl.when(j == 0)
    def _():
        prev = jnp.where(i == 0, 0.0, xp_ref[0])
        nxt = jnp.where(i == pl.num_programs(1) - 1, 0.0, xn_ref[0])
        xb_ref[0:halo, :] = prev.astype(_BF)
        xb_ref[halo:halo + ts, :] = x_ref[0].astype(_BF)
        xb_ref[halo + ts:, :] = nxt.astype(_BF)
        acc_ref[...] = jnp.zeros_like(acc_ref)

    xb = xb_ref[...]
    ha = jnp.dot(xb, wa_ref[...], preferred_element_type=_F32)
    hv = jnp.dot(xb, wv_ref[...], preferred_element_type=_F32)
    a = _dwconv_rows(ha, ca_ref[...])[halo:halo + ts]
    v = _dwconv_rows(hv, cv_ref[...])[halo:halo + ts]
    gated = (_silu(a) * v).astype(_BF)
    acc_ref[...] += jnp.dot(gated, wd_ref[...], preferred_element_type=_F32)

    @pl.when(j == pl.num_programs(2) - 1)
    def _():
        o_ref[0] = _layer_norm(ALPHA * x_ref[0] + acc_ref[...], g_ref[...], b_ref[...])


def _conv_ffn_ln(x3, w_up, w_conv, w_down, g, b, ts, fb):
    bsz, s, d = x3.shape
    halo = SUBLANES_BF16
    nj = D_FF // fb
    hb = ts // halo
    last = s // halo - 1
    kern = functools.partial(_ffn_kernel, ts=ts, halo=halo)
    return pl.pallas_call(
        kern,
        grid=(bsz, s // ts, nj),
        in_specs=[pl.BlockSpec((1, ts, d), lambda bi, i, j: (bi, i, 0)),
                  pl.BlockSpec((1, halo, d), lambda bi, i, j: (bi, jnp.maximum(i * hb - 1, 0), 0)),
                  pl.BlockSpec((1, halo, d), lambda bi, i, j: (bi, jnp.minimum((i + 1) * hb, last), 0)),
                  pl.BlockSpec((d, fb), lambda bi, i, j: (0, j)),
                  pl.BlockSpec((d, fb), lambda bi, i, j: (0, nj + j)),
                  pl.BlockSpec((3, fb), lambda bi, i, j: (0, j)),
                  pl.BlockSpec((3, fb), lambda bi, i, j: (0, nj + j)),
                  pl.BlockSpec((fb, d), lambda bi, i, j: (j, 0)),
                  pl.BlockSpec((1, d), lambda bi, i, j: (0, 0)),
                  pl.BlockSpec((1, d), lambda bi, i, j: (0, 0))],
        out_specs=pl.BlockSpec((1, ts, d), lambda bi, i, j: (bi, i, 0)),
        out_shape=jax.ShapeDtypeStruct((bsz, s, d), _F32),
        scratch_shapes=[pltpu.VMEM((ts + 2 * halo, d), _BF), pltpu.VMEM((ts, d), _F32)],
        compiler_params=_params("parallel", "parallel", "arbitrary"),
        name="conv_ffn",
    )(x3, x3, x3, w_up, w_up, w_conv, w_conv, w_down, g, b)


def _fold_rows(x, op, group=SUBLANES_F32):
    acc = x[:group]
    for g in range(1, x.shape[0] // group):
        acc = op(acc, x[g * group:(g + 1) * group])
    while acc.shape[0] > SUBLANES_F32:
        half = acc.shape[0] // 2
        acc = op(acc[:half], acc[half:])
    return acc


def _attn_kernel(lam_ref, far_ref, q_ref, k_ref, vt_ref, bias_ref, subln_ref, o_ref,
                 s_a, s_b, acc_ref, *, tq, tk, out_scale):
    h = pl.program_id(1)
    i = pl.program_id(2)
    nkv = k_ref.shape[1] // tk
    lane = lax.broadcasted_iota(jnp.int32, (1, DA_W), 1)
    q = q_ref[0]
    zero = jnp.zeros_like(q)
    qs = (jnp.where(lane < DA_HD, q, zero), jnp.where(lane >= DA_HD, q, zero))

    unit, u_min, u_max = _near_offsets(tq, tk)
    far_left = far_ref[3 * h]
    far_right = far_ref[3 * h + 1]
    bias_max = far_ref[3 * h + 2]

    def offset(j):
        return (tk * j - tq * i) // unit

    def side_select(j, left, mid, right):
        u = offset(j)
        return jnp.where(u < u_min, left, jnp.where(u > u_max, right, mid))

    def scores(j, s_out, mi):
        kj = k_ref[0, pl.ds(pl.multiple_of(j * tk, tk), tk), :]
        s = lax.dot_general(kj, qs[mi], _NT, preferred_element_type=_F32)
        s_out[mi] = s
        return jnp.max(_fold_rows(s, jnp.maximum), axis=0, keepdims=True)

    def max_bound(j):
        return side_select(j, far_left, bias_max, far_right)

    def add_near_bias(j, s_out):
        u = offset(j)

        @pl.when((u >= u_min) & (u <= u_max))
        def _():
            tile = bias_ref[u - u_min, 0]
            for mi in range(2):
                s_out[mi] = s_out[mi] + tile

    def half_step(j, carry, s_cur, s_nxt):
        jn = jnp.minimum(j + 1, nkv - 1)
        bound_n = max_bound(jn)
        shift = side_select(j, far_left, 0.0, far_right)
        new = []
        for mi in range(2):
            m = carry[mi][0]
            m_nxt = jnp.maximum(m, scores(jn, s_nxt, mi) + bound_n)
            new.append((m_nxt, jnp.exp2(m - m_nxt)))
        vtj = vt_ref[0, 0, :, pl.ds(pl.multiple_of(j * tk, tk), tk)]
        for mi in range(2):
            m, alpha = carry[mi]
            p = jnp.exp2((s_cur[mi] - (m - shift)).astype(_BF))
            acc_ref[mi] = alpha * acc_ref[mi] + jnp.dot(vtj, p, preferred_element_type=_F32)
        add_near_bias(jn, s_nxt)
        return tuple(new)

    acc_ref[...] = jnp.zeros_like(acc_ref)
    init = tuple((scores(0, s_a, mi) + max_bound(0), jnp.zeros((1, tq), _F32)) for mi in range(2))
    add_near_bias(0, s_a)

    def pair(jj, carry):
        carry = half_step(2 * jj, carry, s_a, s_b)
        return half_step(2 * jj + 1, carry, s_b, s_a)

    lax.fori_loop(0, nkv // 2, pair, init)

    lam = lam_ref[0]
    o0 = acc_ref[0, :DA_W] / acc_ref[0, DA_W:DA_W + 1]
    o1 = acc_ref[1, :DA_W] / acc_ref[1, DA_W:DA_W + 1]
    ot = o0 - lam * o1
    o = ot.T
    ms = jnp.mean(o * o, axis=-1, keepdims=True)
    o_ref[0] = (o * lax.rsqrt(ms + LN_EPS) * subln_ref[...] * out_scale).astype(o_ref.dtype)


def _t5_bucket(rel):
    half = N_BUCKETS // 2
    max_exact = half // 2
    ret = jnp.where(rel > 0, half, 0)
    n = jnp.abs(rel)
    nf = jnp.maximum(n, 1).astype(_F32)
    large = max_exact + (jnp.log(nf / max_exact) / math.log(MAX_DISTANCE / max_exact)
                         * (half - max_exact)).astype(jnp.int32)
    large = jnp.minimum(large, half - 1)
    return ret + jnp.where(n < max_exact, n, large)


def _near_offsets(tq, tk):
    unit = math.gcd(tq, tk)
    first = -((tk + MAX_DISTANCE - 2) // unit)
    last = (tq + MAX_DISTANCE - 2) // unit
    return unit, first, last


def _bias_tiles(rel_bias, tq, tk):
    unit, u_min, u_max = _near_offsets(tq, tk)
    n_near = u_max - u_min + 1
    c = jnp.arange(tk, dtype=jnp.int32)[:, None]
    r = jnp.arange(tq, dtype=jnp.int32)[None, :]
    u = jnp.arange(u_min, u_max + 1, dtype=jnp.int32)[:, None, None]
    table = rel_bias.astype(_F32) * LOG2E
    bucket = _t5_bucket(u * unit + c - r)[:, None]
    tiles = jnp.zeros((n_near, DA_HEADS, tk, tq), _F32)
    for b in range(N_BUCKETS):
        tiles = jnp.where(bucket == b, table[b][None, :, None, None], tiles)
    far =table[_t5_bucket(jnp.array([-MAX_DISTANCE, MAX_DISTANCE], jnp.int32))]
    consts = jnp.concatenate([far, jnp.max(table, axis=0, keepdims=True)], axis=0)
    return tiles, consts.T.reshape(-1)


def _diff_attention(p3, vt, lam, subln, bias, far, layer, tq, tk):
    b, s, _ = p3.shape
    vrows = vt.shape[2]
    assert s % (2 * tk) == 0
    lam_init = 0.8 - 0.6 * math.exp(-0.3 * layer)
    kern = functools.partial(_attn_kernel, tq=tq, tk=tk, out_scale=1.0 - lam_init)
    return pl.pallas_call(
        kern,
        grid=(b, DA_HEADS, s // tq),
        in_specs=[pl.BlockSpec(memory_space=pltpu.SMEM),
                  pl.BlockSpec(memory_space=pltpu.SMEM),
                  pl.BlockSpec((1, tq, DA_W), lambda bi, h, i: (bi, i, h)),
                  pl.BlockSpec((1, s, DA_W), lambda bi, h, i: (bi, 0, DA_HEADS + h)),
                  pl.BlockSpec((1, 1, vrows, s), lambda bi, h, i: (bi, h, 0, 0)),
                  pl.BlockSpec((bias.shape[0], 1, tk, tq), lambda bi, h, i: (0, h, 0, 0)),
                  pl.BlockSpec((1, DA_W), lambda bi, h, i: (0, 0))],
        out_specs=pl.BlockSpec((1, tq, DA_W), lambda bi, h, i: (bi, i, h)),
        out_shape=jax.ShapeDtypeStruct((b, s, MIX_W), _BF),
        scratch_shapes=[pltpu.VMEM((2, tk, tq), _F32), pltpu.VMEM((2, tk, tq), _F32),
                        pltpu.VMEM((2, vrows, tq), _F32)],
        compiler_params=_params("parallel", "parallel", "parallel"),
        name="diff_attn",
    )(lam, far, p3, p3, vt, bias, subln)


def _hgrn_chunk_stages(z, q, v, lb, st, tri, rev):
    c_rows = z.shape[0]
    f = lb + (1.0 - lb) * _sigmoid(z)
    kk = (1.0 - lb) * _sigmoid(-z)
    logf = jnp.log(f)
    hi = logf.astype(_BF)
    lo = (logf - hi.astype(_F32)).astype(_BF)
    cum = (jnp.dot(tri, hi, preferred_element_type=_F32)
           + jnp.dot(tri, lo, preferred_element_type=_F32))
    qh = _silu(q) * (HG_DK ** -0.5)
    vb = v.astype(_BF)
    yield None
    total = cum[0:1, :] if rev else cum[c_rows - 1:c_rows, :]
    o_inter = lax.dot_general((qh * jnp.exp(cum)).astype(_BF), st.astype(_BF), _NT,
                              preferred_element_type=_F32)
    ks = (kk * jnp.exp(total - cum)).astype(_BF)
    st_new = st * jnp.exp(total) + jnp.dot(v.T.astype(_BF), ks, preferred_element_type=_F32)
    blocks = []
    for b in range(c_rows // HG_SUB):
        lo_r, hi_r = b * HG_SUB, (b + 1) * HG_SUB
        if rev:
            cols = slice(lo_r, c_rows)
            base = cum[hi_r:hi_r + 1, :] if hi_r < c_rows else jnp.zeros_like(total)
        else:
            cols = slice(0, hi_r)
            base = cum[lo_r - 1:lo_r, :] if b else jnp.zeros_like(total)
        qq = (qh[lo_r:hi_r] * jnp.exp(cum[lo_r:hi_r] - base)).astype(_BF)
        kt = (kk[cols] * jnp.exp(base - cum[cols])).astype(_BF)
        a = lax.dot_general(qq, kt, _NT, preferred_element_type=_F32)
        blocks.append((lo_r, hi_r, cols, a))
    yield None
    parts = []
    for lo_r, hi_r, cols, a in blocks:
        sub_row = lax.broadcasted_iota(jnp.int32, a.shape, 0) + lo_r
        sub_col = lax.broadcasted_iota(jnp.int32, a.shape, 1) + cols.start
        seen = (sub_col >= sub_row) if rev else (sub_col <= sub_row)
        a = jnp.where(seen, a, 0.0)
        parts.append(o_inter[lo_r:hi_r]
                     + jnp.dot(a.astype(_BF), vb[cols], preferred_element_type=_F32))
    yield jnp.concatenate(parts, axis=0), st_new


def _hgrn_kernel(qf_ref, vf_ref, zf_ref, qb_ref, vb_ref, zb_ref, lb_ref, of_ref, ob_ref, st_ref,
                 *, rows):
    @pl.when(pl.program_id(1) == 0)
    def _():
        st_ref[...] = jnp.zeros_like(st_ref)

    c_rows = HG_CHUNK
    n_chunks = rows // c_rows
    row = lax.broadcasted_iota(jnp.int32, (c_rows, c_rows), 0)
    col = lax.broadcasted_iota(jnp.int32, (c_rows, c_rows), 1)
    tris = ((col <= row).astype(_BF), (col >= row).astype(_BF))
    dirs = ((qf_ref, vf_ref, zf_ref, of_ref), (qb_ref, vb_ref, zb_ref, ob_ref))

    def chunk(c, carry):
        starts = (pl.multiple_of(c * c_rows, c_rows),
                  pl.multiple_of((n_chunks - 1 - c) * c_rows, c_rows))
        for h0 in range(0, HG_HEADS, HG_HEAD_GROUP):
            chains = []
            for h in range(h0, h0 + HG_HEAD_GROUP):
                cs = slice(h * HG_DK, (h + 1) * HG_DK)
                for d, (q_ref, v_ref, z_ref, o_ref) in enumerate(dirs):
                    rs = pl.ds(starts[d], c_rows)
                    gen = _hgrn_chunk_stages(z_ref[0, rs, cs], q_ref[0, rs, cs], v_ref[0, rs, cs],
                                             lb_ref[d, :, cs], st_ref[d, h], tris[d], rev=bool(d))
                    chains.append((gen, o_ref, rs, cs, d, h))
            for _ in range(2):
                for chain in chains:
                    next(chain[0])
            for gen, o_ref, rs, cs, d, h in chains:
                o, st = next(gen)
                o_ref[0, rs, cs] = o
                st_ref[d, h] = st
        return carry

    lax.fori_loop(0, n_chunks, chunk, 0)


def _hgrn_scan(p3, lb, rows):
    b, s, _ = p3.shape
    w = MIX_W
    t = s // rows
    fwd = lambda c: pl.BlockSpec((1, rows, w), lambda bi, i: (bi, i, c))
    bwd = lambda c: pl.BlockSpec((1, rows, w), lambda bi, i: (bi, t - 1 - i, c))
    out = jax.ShapeDtypeStruct((b, s, w), _F32)
    return pl.pallas_call(
        functools.partial(_hgrn_kernel, rows=rows),
        grid=(b, t),
        in_specs=[fwd(0), fwd(1), fwd(3), bwd(0), bwd(1), bwd(4),
                  pl.BlockSpec((2, 1, w), lambda bi, i: (0, 0, 0))],
        out_specs=[fwd(0), bwd(0)],
        out_shape=[out, out],
        scratch_shapes=[pltpu.VMEM((2, HG_HEADS, HG_DK, HG_DK), _F32)],
        compiler_params=_params("parallel", "arbitrary"),
        name="hgrn_scan",
    )(p3, p3, p3, p3, p3, p3, lb)


def _hgrn_finish_kernel(of_ref, ob_ref, g_ref, w_ref, o_ref):
    o = of_ref[...] + ob_ref[...]
    gate = _silu(g_ref[...])
    w = w_ref[...]
    for h in range(HG_HEADS):
        cs = slice(h * HG_DK, (h + 1) * HG_DK)
        oh = o[:, cs]
        ms = jnp.mean(oh * oh, axis=-1, keepdims=True)
        o_ref[:, cs] = (oh * lax.rsqrt(ms + LN_EPS) * w[:, cs] * gate[:, cs]).astype(o_ref.dtype)


def _hgrn_finish(o_fw, o_bw, p2, g_col, norm_w, tm):
    n = o_fw.shape[0]
    return pl.pallas_call(
        _hgrn_finish_kernel,
        grid=(n // tm,),
        in_specs=[pl.BlockSpec((tm, MIX_W), lambda i: (i, 0)),
                  pl.BlockSpec((tm, MIX_W), lambda i: (i, 0)),
                  pl.BlockSpec((tm, MIX_W), lambda i: (i, g_col)),
                  pl.BlockSpec((1, MIX_W), lambda i: (0, 0))],
        out_specs=pl.BlockSpec((tm, MIX_W), lambda i: (i, 0)),
        out_shape=jax.ShapeDtypeStruct((n, MIX_W), _BF),
        compiler_params=_params("parallel"),
        name="hgrn_finish",
    )(o_fw, o_bw, p2, norm_w)


def _sconv_kernel(gb_ref, gc_ref, h_ref, gcp_ref, hp_ref, gcn_ref, hn_ref, w_ref, o_ref, *, halo):
    i = pl.program_id(1)
    u = gc_ref[0].astype(_F32) * h_ref[0].astype(_F32)
    n = u.shape[0]
    u_prev = (gcp_ref[0].astype(_F32) * hp_ref[0].astype(_F32))[halo - 1:halo]
    u_next = (gcn_ref[0].astype(_F32) * hn_ref[0].astype(_F32))[0:1]
    u_prev = jnp.where(i == 0, 0.0, u_prev)
    u_next = jnp.where(i == pl.num_programs(1) - 1, 0.0, u_next)
    row = lax.broadcasted_iota(jnp.int32, u.shape, 0)
    down = jnp.where(row == 0, u_prev, pltpu.roll(u, 1, 0))
    up = jnp.where(row == n - 1, u_next, pltpu.roll(u, n - 1, 0))
    w = w_ref[...]
    y = down * w[0:1] + u * w[1:2] + up * w[2:3]
    o_ref[0] = (gb_ref[0].astype(_F32) * y).astype(o_ref.dtype)


def _short_conv(p3, conv_w, ts):
    b, s, _ = p3.shape
    halo = SUBLANES_BF16
    hb = ts // halo
    last = s // halo - 1
    main = lambda c: pl.BlockSpec((1, ts, MIX_W), lambda bi, i: (bi, i, c))
    prev = lambda c: pl.BlockSpec((1, halo, MIX_W), lambda bi, i: (bi, jnp.maximum(i * hb - 1, 0), c))
    nxt = lambda c: pl.BlockSpec((1, halo, MIX_W), lambda bi, i: (bi, jnp.minimum((i + 1) * hb, last), c))
    return pl.pallas_call(
        functools.partial(_sconv_kernel, halo=halo),
        grid=(b, s // ts),
        in_specs=[main(0), main(1), main(2), prev(1), prev(2), nxt(1), nxt(2),
                  pl.BlockSpec((3, MIX_W), lambda bi, i: (0, 0))],
        out_specs=pl.BlockSpec((1, ts, MIX_W), lambda bi, i: (bi, i, 0)),
        out_shape=jax.ShapeDtypeStruct((b, s, MIX_W), _BF),
        compiler_params=_params("parallel", "parallel"),
        name="short_conv",
    )(p3, p3, p3, p3, p3, p3, p3, conv_w)


def _tile(n, pref):
    return pref if n % pref == 0 else n


def kernel(x, mem, rel_bias, attn_w_in, attn_lambda, attn_subln, hgrn_w_in, hgrn_lower_bound,
           hgrn_norm, conv_w_in, conv_w, mem_w_kv, w_o, ln_gain, ln_bias, ffn_w_up, ffn_conv,
           ffn_w_down):
    b, s, d = x.shape
    n = b * s
    m = mem.shape[1]
    tm = _tile(s, ROW_TILE)
    hg_rows = _tile(s, HG_ROWS)

    bias, far = _bias_tiles(rel_bias, ATTN_Q_TILE, ATTN_K_TILE)
    mem2 = mem.reshape(b * m, d)
    lbw = jax.nn.softmax(hgrn_lower_bound.astype(_F32), axis=0)
    lb_all = jnp.cumsum(lbw, axis=0) - lbw[0]

    for layer in range(DEPTH):
        kind, j = layer % N_MIXERS, layer // N_MIXERS
        x2 = x.reshape(n, d)
        kv = _matmul(mem2, mem_w_kv[layer].astype(_BF), _BF, _tile(b * m, ROW_TILE), 2 * XATTN_W)
        kv = kv.reshape(b, m, 2 * XATTN_W)
        if kind == 0:
            q_scale = jnp.where(jnp.arange(attn_w_in.shape[2]) < MIX_W, DA_HD ** -0.5 * LOG2E, 1.0)
            w_in = (attn_w_in[j] * q_scale).astype(_BF)
            p = _matmul(x2, w_in, _BF, tm, w_in.shape[1] // 2).reshape(b, s, -1)
            lp = attn_lambda[j].astype(_F32)
            lam_init = 0.8 - 0.6 * math.exp(-0.3 * layer)
            lam = jnp.exp(jnp.sum(lp[0] * lp[1])) - jnp.exp(jnp.sum(lp[2] * lp[3])) + lam_init
            vt = jnp.swapaxes(p[..., 2 * MIX_W:3 * MIX_W], 1, 2).reshape(b, DA_HEADS, DA_W, s)
            vt = jnp.concatenate([vt, jnp.ones((b, DA_HEADS, SUBLANES_BF16, s), _BF)], axis=2)
            mixed = _diff_attention(p, vt, lam.reshape(1),
                                    attn_subln[j].astype(_F32).reshape(1, DA_W),
                                    bias, far, layer, ATTN_Q_TILE, ATTN_K_TILE)
            q_col = 3 * MIX_W // XATTN_W
        elif kind == 1:
            w_in = hgrn_w_in[j].astype(_BF)
            p = _matmul(x2, w_in, _F32, tm, w_in.shape[1] // 2).reshape(b, s, -1)
            o_fw, o_bw = _hgrn_scan(p, lb_all[layer].reshape(2, 1, MIX_W), hg_rows)
            mixed = _hgrn_finish(o_fw.reshape(n, MIX_W), o_bw.reshape(n, MIX_W),
                                 p.reshape(n, -1), 2,
                                 hgrn_norm[j].astype(_F32).reshape(1, MIX_W), tm)
            q_col = 5 * MIX_W // XATTN_W
        else:
            w_in = conv_w_in[j].astype(_BF)
            p = _matmul(x2, w_in, _BF, tm, w_in.shape[1] // 2).reshape(b, s, -1)
            mixed = _short_conv(p, conv_w[j].astype(_F32), tm)
            q_col = 3 * MIX_W // XATTN_W
        recalled = _memory_attention(p, kv, q_col, tm)
        x2 = _out_ln(x2, mixed.reshape(n, MIX_W), recalled.reshape(n, XATTN_W),
                     w_o[layer].astype(_BF), ln_gain[layer, 0].reshape(1, d),
                     ln_bias[layer, 0].reshape(1, d), tm)
        x = _conv_ffn_ln(x2.reshape(b, s, d), ffn_w_up[layer].astype(_BF),
                         ffn_conv[layer].astype(_F32), ffn_w_down[layer].astype(_BF),
                         ln_gain[layer, 1].reshape(1, d), ln_bias[layer, 1].reshape(1, d),
                         tm, FFN_TILE)
    return x
```

```python
import functools
import math

import jax
import jax.numpy as jnp
from jax import lax
from jax.experimental import pallas as pl
from jax.experimental.pallas import tpu as pltpu

D_MODEL = 1024
DEPTH = 4
N_MIXERS = 3
MIX_W = 3 * D_MODEL // 4
XATTN_HEADS = 4
XATTN_W = D_MODEL - MIX_W
XATTN_HD = XATTN_W // XATTN_HEADS
DA_HD = 64
DA_HEADS = MIX_W // (2 * DA_HD)
DA_W = 2 * DA_HD
N_BUCKETS = 32
MAX_DISTANCE = 128
HG_DK = 128
HG_HEADS = MIX_W // HG_DK
HG_CHUNK = 64
HG_SUB = 16
D_FF = ((8 * D_MODEL // 3 + 255) // 256) * 256
LN_EPS = 1e-5
ALPHA = (2 * DEPTH) ** 0.25
LOG2E = math.log2(math.e)

LANES = 128
SUBLANES_F32 = 8
SUBLANES_BF16 = 16
VMEM_LIMIT = 52 * 1024 * 1024

ROW_TILE = 512
ATTN_Q_TILE = 256
ATTN_K_TILE = 1024
MXU_WIDTH = 256
FFN_TILE = MXU_WIDTH
HG_ROWS = 512
HG_HEAD_GROUP = 3

_NEG = -1e30
_BF = jnp.bfloat16
_F32 = jnp.float32
_NT = (((1,), (1,)), ((), ()))


def _params(*sem):
    return pltpu.CompilerParams(dimension_semantics=sem, vmem_limit_bytes=VMEM_LIMIT)


def _sigmoid(z):
    e = jnp.exp(-jnp.abs(z))
    r = 1.0 / (1.0 + e)
    return jnp.where(z >= 0, r, e * r)


def _silu(z):
    return z * _sigmoid(z)


def _layer_norm(r, g, b):
    mu = jnp.mean(r, axis=-1, keepdims=True)
    rc = r - mu
    var = jnp.mean(rc * rc, axis=-1, keepdims=True)
    return rc * lax.rsqrt(var + LN_EPS) * g + b


def _mm_kernel(x_ref, w_ref, o_ref):
    o_ref[...] = jnp.dot(x_ref[...].astype(_BF), w_ref[...],
                         preferred_element_type=_F32).astype(o_ref.dtype)


def _matmul(x, w, out_dtype, tm):
    m, k = x.shape
    n = w.shape[1]
    return pl.pallas_call(
        _mm_kernel,
        grid=(m // tm,),
        in_specs=[pl.BlockSpec((tm, k), lambda i: (i, 0)),
                  pl.BlockSpec((k, n), lambda i: (0, 0), pipeline_mode=pl.Buffered(1))],
        out_specs=pl.BlockSpec((tm, n), lambda i: (i, 0)),
        out_shape=jax.ShapeDtypeStruct((m, n), out_dtype),
        compiler_params=_params("parallel"),
        name="proj",
    )(x, w)


def _memattn_kernel(q_ref, km_ref, vm_ref, o_ref):
    q = q_ref[0].astype(_BF)
    km = km_ref[0]
    vm = vm_ref[0]
    lane = lax.broadcasted_iota(jnp.int32, (1, XATTN_W), 1)
    acc = jnp.zeros(q.shape, _F32)
    for h in range(XATTN_HEADS):
        head = (lane >= h * XATTN_HD) & (lane < (h + 1) * XATTN_HD)
        qh = jnp.where(head, q, jnp.zeros_like(q))
        s = lax.dot_general(qh, km, _NT, preferred_element_type=_F32) * (XATTN_HD ** -0.5)
        p = jnp.exp(s - jnp.max(s, axis=-1, keepdims=True))
        l = jnp.sum(p, axis=-1, keepdims=True)
        vh = jnp.where(head, vm, jnp.zeros_like(vm))
        acc = acc + jnp.dot(p.astype(_BF), vh, preferred_element_type=_F32) / l
    o_ref[0] = acc.astype(o_ref.dtype)


def _memory_attention(p3, kv, q_col, tm):
    b, s, _ = p3.shape
    m = kv.shape[1]
    return pl.pallas_call(
        _memattn_kernel,
        grid=(b, s // tm),
        in_specs=[pl.BlockSpec((1, tm, XATTN_W), lambda bi, i: (bi, i, q_col)),
                  pl.BlockSpec((1, m, XATTN_W), lambda bi, i: (bi, 0, 0)),
                  pl.BlockSpec((1, m, XATTN_W), lambda bi, i: (bi, 0, 1))],
        out_specs=pl.BlockSpec((1, tm, XATTN_W), lambda bi, i: (bi, i, 0)),
        out_shape=jax.ShapeDtypeStruct((b, s, XATTN_W), _BF),
        compiler_params=_params("parallel", "parallel"),
        name="memattn",
    )(p3, kv, kv)


def _out_ln_kernel(x_ref, mix_ref, rec_ref, wo_ref, g_ref, b_ref, o_ref):
    y = jnp.dot(mix_ref[...], wo_ref[:MIX_W, :], preferred_element_type=_F32)
    y = y + jnp.dot(rec_ref[...], wo_ref[MIX_W:, :], preferred_element_type=_F32)
    o_ref[...] = _layer_norm(ALPHA * x_ref[...] + y, g_ref[...], b_ref[...])


def _out_ln(x2, mixed, recalled, wo, g, b, tm):
    n = x2.shape[0]
    return pl.pallas_call(
        _out_ln_kernel,
        grid=(n // tm,),
        in_specs=[pl.BlockSpec((tm, D_MODEL), lambda i: (i, 0)),
                  pl.BlockSpec((tm, MIX_W), lambda i: (i, 0)),
                  pl.BlockSpec((tm, XATTN_W), lambda i: (i, 0)),
                  pl.BlockSpec((D_MODEL, D_MODEL), lambda i: (0, 0)),
                  pl.BlockSpec((1, D_MODEL), lambda i: (0, 0)),
                  pl.BlockSpec((1, D_MODEL), lambda i: (0, 0))],
        out_specs=pl.BlockSpec((tm, D_MODEL), lambda i: (i, 0)),
        out_shape=jax.ShapeDtypeStruct((n, D_MODEL), _F32),
        compiler_params=_params("parallel"),
        name="out_ln",
    )(x2, mixed, recalled, wo, g, b)


def _dwconv_rows(h, w):
    n = h.shape[0]
    return (pltpu.roll(h, 1, 0) * w[0:1] + h * w[1:2] + pltpu.roll(h, n - 1, 0) * w[2:3])


def _ffn_kernel(x_ref, xp_ref, xn_ref, wu_ref, wc_ref, wd_ref, g_ref, b_ref, o_ref, gated_ref,
                *, ts, halo, fb):
    i = pl.program_id(1)
    prev = jnp.where(i == 0, 0.0, xp_ref[0])
    nxt = jnp.where(i == pl.num_programs(1) - 1, 0.0, xn_ref[0])
    xb = jnp.concatenate([prev, x_ref[0], nxt], axis=0).astype(_BF)
    for c in range(D_FF // fb):
        ca = slice(c * fb, (c + 1) * fb)
        cv = slice(D_FF + c * fb, D_FF + (c + 1) * fb)
        ha = jnp.dot(xb, wu_ref[:, ca], preferred_element_type=_F32)
        hv = jnp.dot(xb, wu_ref[:, cv], preferred_element_type=_F32)
        a = _dwconv_rows(ha, wc_ref[:, ca])[halo:halo + ts]
        v = _dwconv_rows(hv, wc_ref[:, cv])[halo:halo + ts]
        gated_ref[:, ca] = (_silu(a) * v).astype(_BF)
    f = jnp.dot(gated_ref[...], wd_ref[...], preferred_element_type=_F32)
    o_ref[0] = _layer_norm(ALPHA * x_ref[0] + f, g_ref[...], b_ref[...])


def _conv_ffn_ln(x3, w_up, w_conv, w_down, g, b, ts, fb):
    bsz, s, d = x3.shape
    halo = SUBLANES_BF16
    hb = ts // halo
    last = s // halo - 1
    kern = functools.partial(_ffn_kernel, ts=ts, halo=halo, fb=fb)
    resident = lambda shape: pl.BlockSpec(shape, lambda bi, i: (0, 0),
                                          pipeline_mode=pl.Buffered(1))
    return pl.pallas_call(
        kern,
        grid=(bsz, s // ts),
        in_specs=[pl.BlockSpec((1, ts, d), lambda bi, i: (bi, i, 0)),
                  pl.BlockSpec((1, halo, d), lambda bi, i: (bi, jnp.maximum(i * hb - 1, 0), 0)),
                  pl.BlockSpec((1, halo, d), lambda bi, i: (bi, jnp.minimum((i + 1) * hb, last), 0)),
                  resident((d, 2 * D_FF)),
                  resident((3, 2 * D_FF)),
                  resident((D_FF, d)),
                  pl.BlockSpec((1, d), lambda bi, i: (0, 0)),
                  pl.BlockSpec((1, d), lambda bi, i: (0, 0))],
        out_specs=pl.BlockSpec((1, ts, d), lambda bi, i: (bi, i, 0)),
        out_shape=jax.ShapeDtypeStruct((bsz, s, d), _F32),
        scratch_shapes=[pltpu.VMEM((ts, D_FF), _BF)],
        compiler_params=_params("parallel", "parallel"),
        name="conv_ffn",
    )(x3, x3, x3, w_up, w_conv, w_down, g, b)


def _fold_rows(x, op, group=SUBLANES_F32):
    acc = x[:group]
    for g in range(1, x.shape[0] // group):
        acc = op(acc, x[g * group:(g + 1) * group])
    while acc.shape[0] > SUBLANES_F32:
        half = acc.shape[0] // 2
        acc = op(acc[:half], acc[half:])
    return acc


def _attn_kernel(lam_ref, far_ref, q_ref, k_ref, vt_ref, bias_ref, subln_ref, o_ref,
                 s_a, s_b, acc_ref, *, tq, tk, out_scale):
    h = pl.program_id(1)
    i = pl.program_id(2)
    nkv = k_ref.shape[1] // tk
    lane = lax.broadcasted_iota(jnp.int32, (1, DA_W), 1)
    q = q_ref[0]
    zero = jnp.zeros_like(q)
    qs = (jnp.where(lane < DA_HD, q, zero), jnp.where(lane >= DA_HD, q, zero))

    unit, u_min, u_max = _near_offsets(tq, tk)
    far_left = far_ref[3 * h]
    far_right = far_ref[3 * h + 1]
    bias_max = far_ref[3 * h + 2]

    def offset(j):
        return (tk * j - tq * i) // unit

    def side_select(j, left, mid, right):
        u = offset(j)
        return jnp.where(u < u_min, left, jnp.where(u > u_max, right, mid))

    def scores(j, s_out, mi):
        kj = k_ref[0, pl.ds(pl.multiple_of(j * tk, tk), tk), :]
        s = lax.dot_general(kj, qs[mi], _NT, preferred_element_type=_F32)
        s_out[mi] = s
        return jnp.max(_fold_rows(s, jnp.maximum), axis=0, keepdims=True)

    def max_bound(j):
        return side_select(j, far_left, bias_max, far_right)

    def add_near_bias(j, s_out):
        u = offset(j)

        @pl.when((u >= u_min) & (u <= u_max))
        def _():
            tile = bias_ref[u - u_min, 0]
            for mi in range(2):
                s_out[mi] = s_out[mi] + tile

    def half_step(j, carry, s_cur, s_nxt):
        jn = jnp.minimum(j + 1, nkv - 1)
        bound_n = max_bound(jn)
        shift = side_select(j, far_left, 0.0, far_right)
        new = []
        for mi in range(2):
            m = carry[mi][0]
            m_nxt = jnp.maximum(m, scores(jn, s_nxt, mi) + bound_n)
            new.append((m_nxt, jnp.exp2(m - m_nxt)))
        vtj = vt_ref[0, 0, :, pl.ds(pl.multiple_of(j * tk, tk), tk)]
        for mi in range(2):
            m, alpha = carry[mi]
            p = jnp.exp2((s_cur[mi] - (m - shift)).astype(_BF))
            acc_ref[mi] = alpha * acc_ref[mi] + jnp.dot(vtj, p, preferred_element_type=_F32)
        add_near_bias(jn, s_nxt)
        return tuple(new)

    acc_ref[...] = jnp.zeros_like(acc_ref)
    init = tuple((scores(0, s_a, mi) + max_bound(0), jnp.zeros((1, tq), _F32)) for mi in range(2))
    add_near_bias(0, s_a)

    def pair(jj, carry):
        carry = half_step(2 * jj, carry, s_a, s_b)
        return half_step(2 * jj + 1, carry, s_b, s_a)

    lax.fori_loop(0, nkv // 2, pair, init)

    lam = lam_ref[0]
    o0 = acc_ref[0, :DA_W] / acc_ref[0, DA_W:DA_W + 1]
    o1 = acc_ref[1, :DA_W] / acc_ref[1, DA_W:DA_W + 1]
    ot = o0 - lam * o1
    o = ot.T
    ms = jnp.mean(o * o, axis=-1, keepdims=True)
    o_ref[0] = (o * lax.rsqrt(ms + LN_EPS) * subln_ref[...] * out_scale).astype(o_ref.dtype)


def _t5_bucket(rel):
    half = N_BUCKETS // 2
    max_exact = half // 2
    ret = jnp.where(rel > 0, half, 0)
    n = jnp.abs(rel)
    nf = jnp.maximum(n, 1).astype(_F32)
    large = max_exact + (jnp.log(nf / max_exact) / math.log(MAX_DISTANCE / max_exact)
                         * (half - max_exact)).astype(jnp.int32)
    large = jnp.minimum(large, half - 1)
    return ret + jnp.where(n < max_exact, n, large)


def _near_offsets(tq, tk):
    unit = math.gcd(tq, tk)
    first = -((tk + MAX_DISTANCE - 2) // unit)
    last = (tq + MAX_DISTANCE - 2) // unit
    return unit, first, last


def _bias_tiles(rel_bias, tq, tk):
    unit, u_min, u_max = _near_offsets(tq, tk)
    n_near = u_max - u_min + 1
    c = jnp.arange(tk, dtype=jnp.int32)[:, None]
    r = jnp.arange(tq, dtype=jnp.int32)[None, :]
    u = jnp.arange(u_min, u_max + 1, dtype=jnp.int32)[:, None, None]
    table = rel_bias.astype(_F32) * LOG2E
    bucket = _t5_bucket(u * unit + c - r)[:, None]
    tiles = jnp.zeros((n_near, DA_HEADS, tk, tq), _F32)
    for b in range(N_BUCKETS):
        tiles = jnp.where(bucket == b, table[b][None, :, None, None], tiles)
    far =table[_t5_bucket(jnp.array([-MAX_DISTANCE, MAX_DISTANCE], jnp.int32))]
    consts = jnp.concatenate([far, jnp.max(table, axis=0, keepdims=True)], axis=0)
    return tiles, consts.T.reshape(-1)


def _diff_attention(p3, vt, lam, subln, bias, far, layer, tq, tk):
    b, s, _ = p3.shape
    vrows = vt.shape[2]
    assert s % (2 * tk) == 0
    lam_init = 0.8 - 0.6 * math.exp(-0.3 * layer)
    kern = functools.partial(_attn_kernel, tq=tq, tk=tk, out_scale=1.0 - lam_init)
    return pl.pallas_call(
        kern,
        grid=(b, DA_HEADS, s // tq),
        in_specs=[pl.BlockSpec(memory_space=pltpu.SMEM),
                  pl.BlockSpec(memory_space=pltpu.SMEM),
                  pl.BlockSpec((1, tq, DA_W), lambda bi, h, i: (bi, i, h)),
                  pl.BlockSpec((1, s, DA_W), lambda bi, h, i: (bi, 0, DA_HEADS + h)),
                  pl.BlockSpec((1, 1, vrows, s), lambda bi, h, i: (bi, h, 0, 0)),
                  pl.BlockSpec((bias.shape[0], 1, tk, tq), lambda bi, h, i: (0, h, 0, 0)),
                  pl.BlockSpec((1, DA_W), lambda bi, h, i: (0, 0))],
        out_specs=pl.BlockSpec((1, tq, DA_W), lambda bi, h, i: (bi, i, h)),
        out_shape=jax.ShapeDtypeStruct((b, s, MIX_W), _BF),
        scratch_shapes=[pltpu.VMEM((2, tk, tq), _F32), pltpu.VMEM((2, tk, tq), _F32),
                        pltpu.VMEM((2, vrows, tq), _F32)],
        compiler_params=_params("parallel", "parallel", "parallel"),
        name="diff_attn",
    )(lam, far, p3, p3, vt, bias, subln)


def _hgrn_chunk_stages(z, q, v, lb, st, tri, rev):
    c_rows = z.shape[0]
    f = lb + (1.0 - lb) * _sigmoid(z)
    kk = (1.0 - lb) * _sigmoid(-z)
    logf = jnp.log(f)
    hi = logf.astype(_BF)
    lo = (logf - hi.astype(_F32)).astype(_BF)
    cum = (jnp.dot(tri, hi, preferred_element_type=_F32)
           + jnp.dot(tri, lo, preferred_element_type=_F32))
    qh = _silu(q) * (HG_DK ** -0.5)
    vb = v.astype(_BF)
    yield None
    total = cum[0:1, :] if rev else cum[c_rows - 1:c_rows, :]
    o_inter = lax.dot_general((qh * jnp.exp(cum)).astype(_BF), st.astype(_BF), _NT,
                              preferred_element_type=_F32)
    ks = (kk * jnp.exp(total - cum)).astype(_BF)
    st_new = st * jnp.exp(total) + jnp.dot(v.T.astype(_BF), ks, preferred_element_type=_F32)
    blocks = []
    for b in range(c_rows // HG_SUB):
        lo_r, hi_r = b * HG_SUB, (b + 1) * HG_SUB
        if rev:
            cols = slice(lo_r, c_rows)
            base = cum[hi_r:hi_r + 1, :] if hi_r < c_rows else jnp.zeros_like(total)
        else:
            cols = slice(0, hi_r)
            base = cum[lo_r - 1:lo_r, :] if b else jnp.zeros_like(total)
        qq = (qh[lo_r:hi_r] * jnp.exp(cum[lo_r:hi_r] - base)).astype(_BF)
        kt = (kk[cols] * jnp.exp(base - cum[cols])).astype(_BF)
        a = lax.dot_general(qq, kt, _NT, preferred_element_type=_F32)
        blocks.append((lo_r, hi_r, cols, a))
    yield None
    parts = []
    for lo_r, hi_r, cols, a in blocks:
        sub_row = lax.broadcasted_iota(jnp.int32, a.shape, 0) + lo_r
        sub_col = lax.broadcasted_iota(jnp.int32, a.shape, 1) + cols.start
        seen = (sub_col >= sub_row) if rev else (sub_col <= sub_row)
        a = jnp.where(seen, a, 0.0)
        parts.append(o_inter[lo_r:hi_r]
                     + jnp.dot(a.astype(_BF), vb[cols], preferred_element_type=_F32))
    yield jnp.concatenate(parts, axis=0), st_new


def _hgrn_kernel(qf_ref, vf_ref, zf_ref, qb_ref, vb_ref, zb_ref, lb_ref, of_ref, ob_ref, st_ref,
                 *, rows):
    @pl.when(pl.program_id(1) == 0)
    def _():
        st_ref[...] = jnp.zeros_like(st_ref)

    c_rows = HG_CHUNK
    n_chunks = rows // c_rows
    row = lax.broadcasted_iota(jnp.int32, (c_rows, c_rows), 0)
    col = lax.broadcasted_iota(jnp.int32, (c_rows, c_rows), 1)
    tris = ((col <= row).astype(_BF), (col >= row).astype(_BF))
    dirs = ((qf_ref, vf_ref, zf_ref, of_ref), (qb_ref, vb_ref, zb_ref, ob_ref))

    def chunk(c, carry):
        starts = (pl.multiple_of(c * c_rows, c_rows),
                  pl.multiple_of((n_chunks - 1 - c) * c_rows, c_rows))
        for h0 in range(0, HG_HEADS, HG_HEAD_GROUP):
            chains = []
            for h in range(h0, h0 + HG_HEAD_GROUP):
                cs = slice(h * HG_DK, (h + 1) * HG_DK)
                for d, (q_ref, v_ref, z_ref, o_ref) in enumerate(dirs):
                    rs = pl.ds(starts[d], c_rows)
                    gen = _hgrn_chunk_stages(z_ref[0, rs, cs], q_ref[0, rs, cs], v_ref[0, rs, cs],
                                             lb_ref[d, :, cs], st_ref[d, h], tris[d], rev=bool(d))
                    chains.append((gen, o_ref, rs, cs, d, h))
            for _ in range(2):
                for chain in chains:
                    next(chain[0])
            for gen, o_ref, rs, cs, d, h in chains:
                o, st = next(gen)
                o_ref[0, rs, cs] = o
                st_ref[d, h] = st
        return carry

    lax.fori_loop(0, n_chunks, chunk, 0)


def _hgrn_scan(p3, lb, rows):
    b, s, _ = p3.shape
    w = MIX_W
    t = s // rows
    fwd = lambda c: pl.BlockSpec((1, rows, w), lambda bi, i: (bi, i, c))
    bwd = lambda c: pl.BlockSpec((1, rows, w), lambda bi, i: (bi, t - 1 - i, c))
    out = jax.ShapeDtypeStruct((b, s, w), _F32)
    return pl.pallas_call(
        functools.partial(_hgrn_kernel, rows=rows),
        grid=(b, t),
        in_specs=[fwd(0), fwd(1), fwd(3), bwd(0), bwd(1), bwd(4),
                  pl.BlockSpec((2, 1, w), lambda bi, i: (0, 0, 0))],
        out_specs=[fwd(0), bwd(0)],
        out_shape=[out, out],
        scratch_shapes=[pltpu.VMEM((2, HG_HEADS, HG_DK, HG_DK), _F32)],
        compiler_params=_params("parallel", "arbitrary"),
        name="hgrn_scan",
    )(p3, p3, p3, p3, p3, p3, lb)


def _hgrn_finish_kernel(of_ref, ob_ref, g_ref, w_ref, o_ref):
    o = of_ref[...] + ob_ref[...]
    gate = _silu(g_ref[...])
    w = w_ref[...]
    for h in range(HG_HEADS):
        cs = slice(h * HG_DK, (h + 1) * HG_DK)
        oh = o[:, cs]
        ms = jnp.mean(oh * oh, axis=-1, keepdims=True)
        o_ref[:, cs] = (oh * lax.rsqrt(ms + LN_EPS) * w[:, cs] * gate[:, cs]).astype(o_ref.dtype)


def _hgrn_finish(o_fw, o_bw, p2, g_col, norm_w, tm):
    n = o_fw.shape[0]
    return pl.pallas_call(
        _hgrn_finish_kernel,
        grid=(n // tm,),
        in_specs=[pl.BlockSpec((tm, MIX_W), lambda i: (i, 0)),
                  pl.BlockSpec((tm, MIX_W), lambda i: (i, 0)),
                  pl.BlockSpec((tm, MIX_W), lambda i: (i, g_col)),
                  pl.BlockSpec((1, MIX_W), lambda i: (0, 0))],
        out_specs=pl.BlockSpec((tm, MIX_W), lambda i: (i, 0)),
        out_shape=jax.ShapeDtypeStruct((n, MIX_W), _BF),
        compiler_params=_params("parallel"),
        name="hgrn_finish",
    )(o_fw, o_bw, p2, norm_w)


def _sconv_kernel(gb_ref, gc_ref, h_ref, gcp_ref, hp_ref, gcn_ref, hn_ref, w_ref, o_ref, *, halo):
    i = pl.program_id(1)
    u = gc_ref[0].astype(_F32) * h_ref[0].astype(_F32)
    n = u.shape[0]
    u_prev = (gcp_ref[0].astype(_F32) * hp_ref[0].astype(_F32))[halo - 1:halo]
    u_next = (gcn_ref[0].astype(_F32) * hn_ref[0].astype(_F32))[0:1]
    u_prev = jnp.where(i == 0, 0.0, u_prev)
    u_next = jnp.where(i == pl.num_programs(1) - 1, 0.0, u_next)
    row = lax.broadcasted_iota(jnp.int32, u.shape, 0)
    down = jnp.where(row == 0, u_prev, pltpu.roll(u, 1, 0))
    up = jnp.where(row == n - 1, u_next, pltpu.roll(u, n - 1, 0))
    w = w_ref[...]
    y = down * w[0:1] + u * w[1:2] + up * w[2:3]
    o_ref[0] = (gb_ref[0].astype(_F32) * y).astype(o_ref.dtype)


def _short_conv(p3, conv_w, ts):
    b, s, _ = p3.shape
    halo = SUBLANES_BF16
    hb = ts // halo
    last = s // halo - 1
    main = lambda c: pl.BlockSpec((1, ts, MIX_W), lambda bi, i: (bi, i, c))
    prev = lambda c: pl.BlockSpec((1, halo, MIX_W), lambda bi, i: (bi, jnp.maximum(i * hb - 1, 0), c))
    nxt = lambda c: pl.BlockSpec((1, halo, MIX_W), lambda bi, i: (bi, jnp.minimum((i + 1) * hb, last), c))
    return pl.pallas_call(
        functools.partial(_sconv_kernel, halo=halo),
        grid=(b, s // ts),
        in_specs=[main(0), main(1), main(2), prev(1), prev(2), nxt(1), nxt(2),
                  pl.BlockSpec((3, MIX_W), lambda bi, i: (0, 0))],
        out_specs=pl.BlockSpec((1, ts, MIX_W), lambda bi, i: (bi, i, 0)),
        out_shape=jax.ShapeDtypeStruct((b, s, MIX_W), _BF),
        compiler_params=_params("parallel", "parallel"),
        name="short_conv",
    )(p3, p3, p3, p3, p3, p3, p3, conv_w)


def _tile(n, pref):
    return pref if n % pref == 0 else n


def kernel(x, mem, rel_bias, attn_w_in, attn_lambda, attn_subln, hgrn_w_in, hgrn_lower_bound,
           hgrn_norm, conv_w_in, conv_w, mem_w_kv, w_o, ln_gain, ln_bias, ffn_w_up, ffn_conv,
           ffn_w_down):
    b, s, d = x.shape
    n = b * s
    m = mem.shape[1]
    tm = _tile(s, ROW_TILE)
    hg_rows = _tile(s, HG_ROWS)

    bias, far = _bias_tiles(rel_bias, ATTN_Q_TILE, ATTN_K_TILE)
    mem2 = mem.reshape(b * m, d)
    lbw = jax.nn.softmax(hgrn_lower_bound.astype(_F32), axis=0)
    lb_all = jnp.cumsum(lbw, axis=0) - lbw[0]

    for layer in range(DEPTH):
        kind, j = layer % N_MIXERS, layer // N_MIXERS
        x2 = x.reshape(n, d)
        kv = _matmul(mem2, mem_w_kv[layer].astype(_BF), _BF, _tile(b * m, ROW_TILE))
        kv = kv.reshape(b, m, 2 * XATTN_W)
        if kind == 0:
            q_scale = jnp.where(jnp.arange(attn_w_in.shape[2]) < MIX_W, DA_HD ** -0.5 * LOG2E, 1.0)
            w_in = (attn_w_in[j] * q_scale).astype(_BF)
            p = _matmul(x2, w_in, _BF, tm).reshape(b, s, -1)
            lp = attn_lambda[j].astype(_F32)
            lam_init = 0.8 - 0.6 * math.exp(-0.3 * layer)
            lam = jnp.exp(jnp.sum(lp[0] * lp[1])) - jnp.exp(jnp.sum(lp[2] * lp[3])) + lam_init
            vt = jnp.swapaxes(p[..., 2 * MIX_W:3 * MIX_W], 1, 2).reshape(b, DA_HEADS, DA_W, s)
            vt = jnp.concatenate([vt, jnp.ones((b, DA_HEADS, SUBLANES_BF16, s), _BF)], axis=2)
            mixed = _diff_attention(p, vt, lam.reshape(1),
                                    attn_subln[j].astype(_F32).reshape(1, DA_W),
                                    bias, far, layer, ATTN_Q_TILE, ATTN_K_TILE)
            q_col = 3 * MIX_W // XATTN_W
        elif kind == 1:
            w_in = hgrn_w_in[j].astype(_BF)
            p = _matmul(x2, w_in, _F32, tm).reshape(b, s, -1)
            o_fw, o_bw = _hgrn_scan(p, lb_all[layer].reshape(2, 1, MIX_W), hg_rows)
            mixed = _hgrn_finish(o_fw.reshape(n, MIX_W), o_bw.reshape(n, MIX_W),
                                 p.reshape(n, -1), 2,
                                 hgrn_norm[j].astype(_F32).reshape(1, MIX_W), tm)
            q_col = 5 * MIX_W // XATTN_W
        else:
            w_in = conv_w_in[j].astype(_BF)
            p = _matmul(x2, w_in, _BF, tm).reshape(b, s, -1)
            mixed = _short_conv(p, conv_w[j].astype(_F32), tm)
            q_col = 3 * MIX_W // XATTN_W
        recalled = _memory_attention(p, kv, q_col, tm)
        x2 = _out_ln(x2, mixed.reshape(n, MIX_W), recalled.reshape(n, XATTN_W),
                     w_o[layer].astype(_BF), ln_gain[layer, 0].reshape(1, d),
                     ln_bias[layer, 0].reshape(1, d), tm)
        x = _conv_ffn_ln(x2.reshape(b, s, d), ffn_w_up[layer].astype(_BF),
                         ffn_conv[layer].astype(_F32), ffn_w_down[layer].astype(_BF),
                         ln_gain[layer, 1].reshape(1, d), ln_bias[layer, 1].reshape(1, d),
                         tm, FFN_TILE)
    return x
```

```python
import functools
import math

import jax
import jax.numpy as jnp
from jax import lax
from jax.experimental import pallas as pl
from jax.experimental.pallas import tpu as pltpu

D_MODEL = 1024
DEPTH = 4
N_MIXERS = 3
MIX_W = 3 * D_MODEL // 4
XATTN_HEADS = 4
XATTN_W = D_MODEL - MIX_W
XATTN_HD = XATTN_W // XATTN_HEADS
DA_HD = 64
DA_HEADS = MIX_W // (2 * DA_HD)
DA_W = 2 * DA_HD
N_BUCKETS = 32
MAX_DISTANCE = 128
HG_DK = 128
HG_HEADS = MIX_W // HG_DK
HG_CHUNK = 64
HG_SUB = 16
D_FF = ((8 * D_MODEL // 3 + 255) // 256) * 256
LN_EPS = 1e-5
ALPHA = (2 * DEPTH) ** 0.25
LOG2E = math.log2(math.e)

LANES = 128
SUBLANES_F32 = 8
SUBLANES_BF16 = 16
VMEM_LIMIT = 52 * 1024 * 1024

ROW_TILE = 512
ATTN_Q_TILE = 256
ATTN_K_TILE = 1024
ATTN_HEADS_PER_STEP = 1
ATTN_ROW_CHUNK = 256
MXU_WIDTH = 256
FFN_TILE = MXU_WIDTH
HG_ROWS = 512
HG_HEAD_GROUP = 3

_NEG = -1e30
_BF = jnp.bfloat16
_F32 = jnp.float32
_NT = (((1,), (1,)), ((), ()))


def _params(*sem):
    return pltpu.CompilerParams(dimension_semantics=sem, vmem_limit_bytes=VMEM_LIMIT)


def _sigmoid(z):
    e = jnp.exp(-jnp.abs(z))
    r = 1.0 / (1.0 + e)
    return jnp.where(z >= 0, r, e * r)


def _silu(z):
    return z * _sigmoid(z)


def _layer_norm(r, g, b):
    mu = jnp.mean(r, axis=-1, keepdims=True)
    rc = r - mu
    var = jnp.mean(rc * rc, axis=-1, keepdims=True)
    return rc * lax.rsqrt(var + LN_EPS) * g + b


def _mm_kernel(x_ref, w_ref, o_ref):
    o_ref[...] = jnp.dot(x_ref[...].astype(_BF), w_ref[...],
                         preferred_element_type=_F32).astype(o_ref.dtype)


def _matmul(x, w, out_dtype, tm):
    m, k = x.shape
    n = w.shape[1]
    return pl.pallas_call(
        _mm_kernel,
        grid=(m // tm,),
        in_specs=[pl.BlockSpec((tm, k), lambda i: (i, 0)),
                  pl.BlockSpec((k, n), lambda i: (0, 0), pipeline_mode=pl.Buffered(1))],
        out_specs=pl.BlockSpec((tm, n), lambda i: (i, 0)),
        out_shape=jax.ShapeDtypeStruct((m, n), out_dtype),
        compiler_params=_params("parallel"),
        name="proj",
    )(x, w)


def _memattn_kernel(q_ref, km_ref, vm_ref, o_ref):
    q = q_ref[0].astype(_BF)
    km = km_ref[0]
    vm = vm_ref[0]
    lane = lax.broadcasted_iota(jnp.int32, (1, XATTN_W), 1)
    acc = jnp.zeros(q.shape, _F32)
    for h in range(XATTN_HEADS):
        head = (lane >= h * XATTN_HD) & (lane < (h + 1) * XATTN_HD)
        qh = jnp.where(head, q, jnp.zeros_like(q))
        s = lax.dot_general(qh, km, _NT, preferred_element_type=_F32) * (XATTN_HD ** -0.5)
        p = jnp.exp(s - jnp.max(s, axis=-1, keepdims=True))
        l = jnp.sum(p, axis=-1, keepdims=True)
        vh = jnp.where(head, vm, jnp.zeros_like(vm))
        acc = acc + jnp.dot(p.astype(_BF), vh, preferred_element_type=_F32) / l
    o_ref[0] = acc.astype(o_ref.dtype)


def _memory_attention(p3, kv, q_col, tm):
    b, s, _ = p3.shape
    m = kv.shape[1]
    return pl.pallas_call(
        _memattn_kernel,
        grid=(b, s // tm),
        in_specs=[pl.BlockSpec((1, tm, XATTN_W), lambda bi, i: (bi, i, q_col)),
                  pl.BlockSpec((1, m, XATTN_W), lambda bi, i: (bi, 0, 0)),
                  pl.BlockSpec((1, m, XATTN_W), lambda bi, i: (bi, 0, 1))],
        out_specs=pl.BlockSpec((1, tm, XATTN_W), lambda bi, i: (bi, i, 0)),
        out_shape=jax.ShapeDtypeStruct((b, s, XATTN_W), _BF),
        compiler_params=_params("parallel", "parallel"),
        name="memattn",
    )(p3, kv, kv)


def _out_ln_kernel(x_ref, mix_ref, rec_ref, wo_ref, g_ref, b_ref, o_ref):
    y = jnp.dot(mix_ref[...], wo_ref[:MIX_W, :], preferred_element_type=_F32)
    y = y + jnp.dot(rec_ref[...], wo_ref[MIX_W:, :], preferred_element_type=_F32)
    o_ref[...] = _layer_norm(ALPHA * x_ref[...] + y, g_ref[...], b_ref[...])


def _out_ln(x2, mixed, recalled, wo, g, b, tm):
    n = x2.shape[0]
    return pl.pallas_call(
        _out_ln_kernel,
        grid=(n // tm,),
        in_specs=[pl.BlockSpec((tm, D_MODEL), lambda i: (i, 0)),
                  pl.BlockSpec((tm, MIX_W), lambda i: (i, 0)),
                  pl.BlockSpec((tm, XATTN_W), lambda i: (i, 0)),
                  pl.BlockSpec((D_MODEL, D_MODEL), lambda i: (0, 0)),
                  pl.BlockSpec((1, D_MODEL), lambda i: (0, 0)),
                  pl.BlockSpec((1, D_MODEL), lambda i: (0, 0))],
        out_specs=pl.BlockSpec((tm, D_MODEL), lambda i: (i, 0)),
        out_shape=jax.ShapeDtypeStruct((n, D_MODEL), _F32),
        compiler_params=_params("parallel"),
        name="out_ln",
    )(x2, mixed, recalled, wo, g, b)


def _dwconv_rows(h, w):
    n = h.shape[0]
    return (pltpu.roll(h, 1, 0) * w[0:1] + h * w[1:2] + pltpu.roll(h, n - 1, 0) * w[2:3])


def _ffn_kernel(x_ref, xp_ref, xn_ref, wu_ref, wc_ref, wd_ref, g_ref, b_ref, o_ref, gated_ref,
                *, ts, halo, fb):
    i = pl.program_id(1)
    prev = jnp.where(i == 0, 0.0, xp_ref[0])
    nxt = jnp.where(i == pl.num_programs(1) - 1, 0.0, xn_ref[0])
    xb = jnp.concatenate([prev, x_ref[0], nxt], axis=0).astype(_BF)
    for c in range(D_FF // fb):
        ca = slice(c * fb, (c + 1) * fb)
        cv = slice(D_FF + c * fb, D_FF + (c + 1) * fb)
        ha = jnp.dot(xb, wu_ref[:, ca], preferred_element_type=_F32)
        hv = jnp.dot(xb, wu_ref[:, cv], preferred_element_type=_F32)
        a = _dwconv_rows(ha, wc_ref[:, ca])[halo:halo + ts]
        v = _dwconv_rows(hv, wc_ref[:, cv])[halo:halo + ts]
        gated_ref[:, ca] = (_silu(a) * v).astype(_BF)
    f = jnp.dot(gated_ref[...], wd_ref[...], preferred_element_type=_F32)
    o_ref[0] = _layer_norm(ALPHA * x_ref[0] + f, g_ref[...], b_ref[...])


def _conv_ffn_ln(x3, w_up, w_conv, w_down, g, b, ts, fb):
    bsz, s, d = x3.shape
    halo = SUBLANES_BF16
    hb = ts // halo
    last = s // halo - 1
    kern = functools.partial(_ffn_kernel, ts=ts, halo=halo, fb=fb)
    resident = lambda shape: pl.BlockSpec(shape, lambda bi, i: (0, 0),
                                          pipeline_mode=pl.Buffered(1))
    return pl.pallas_call(
        kern,
        grid=(bsz, s // ts),
        in_specs=[pl.BlockSpec((1, ts, d), lambda bi, i: (bi, i, 0)),
                  pl.BlockSpec((1, halo, d), lambda bi, i: (bi, jnp.maximum(i * hb - 1, 0), 0)),
                  pl.BlockSpec((1, halo, d), lambda bi, i: (bi, jnp.minimum((i + 1) * hb, last), 0)),
                  resident((d, 2 * D_FF)),
                  resident((3, 2 * D_FF)),
                  resident((D_FF, d)),
                  pl.BlockSpec((1, d), lambda bi, i: (0, 0)),
                  pl.BlockSpec((1, d), lambda bi, i: (0, 0))],
        out_specs=pl.BlockSpec((1, ts, d), lambda bi, i: (bi, i, 0)),
        out_shape=jax.ShapeDtypeStruct((bsz, s, d), _F32),
        scratch_shapes=[pltpu.VMEM((ts, D_FF), _BF)],
        compiler_params=_params("parallel", "parallel"),
        name="conv_ffn",
    )(x3, x3, x3, w_up, w_conv, w_down, g, b)


def _fold_rows(x, op, group=SUBLANES_F32):
    acc = x[:group]
    for g in range(1, x.shape[0] // group):
        acc = op(acc, x[g * group:(g + 1) * group])
    while acc.shape[0] > SUBLANES_F32:
        half = acc.shape[0] // 2
        acc = op(acc[:half], acc[half:])
    return acc


def _attn_kernel(lam_ref, far_ref, q_ref, k_ref, vt_ref, bias_ref, subln_ref, o_ref,
                 s_a, s_b, acc_ref, *, tq, tk, out_scale):
    heads = q_ref.shape[2] // DA_W
    streams = [(hh, mi) for hh in range(heads) for mi in range(2)]
    n_str = len(streams)
    hp = pl.program_id(1)
    nkv = k_ref.shape[1] // tk
    nq = q_ref.shape[1] // tq
    n_chunks = tk // ATTN_ROW_CHUNK
    lane = lax.broadcasted_iota(jnp.int32, (1, heads * DA_W), 1)

    unit, u_min, u_max = _near_offsets(tq, tk)
    consts = [[far_ref[3 * (heads * hp + hh) + c] for c in range(3)] for hh in range(heads)]

    def stream_queries(i):
        q = q_ref[0, pl.ds(pl.multiple_of(i * tq, tq), tq), :]
        zero = jnp.zeros_like(q)
        return [jnp.where((lane >= hh * DA_W + mi * DA_HD) & (lane < hh * DA_W + (mi + 1) * DA_HD),
                          q, zero).T for hh, mi in streams]

    def side_select(i, j, left, mid, right):
        u = (tk * j - tq * i) // unit
        return jnp.where(u < u_min, left, jnp.where(u > u_max, right, mid))

    def max_bound(i, j, hh):
        return side_select(i, j, consts[hh][0], consts[hh][2], consts[hh][1])

    def score_rows(qs, j, r, s_out, st):
        k0 = pl.multiple_of(j * tk + r * ATTN_ROW_CHUNK, ATTN_ROW_CHUNK)
        s = jnp.dot(k_ref[0, pl.ds(k0, ATTN_ROW_CHUNK), :], qs[st], preferred_element_type=_F32)
        s_out[st, r * ATTN_ROW_CHUNK:(r + 1) * ATTN_ROW_CHUNK, :] = s
        return _fold_rows(s, jnp.maximum)

    def add_near_bias(i, j, s_out):
        u = (tk * j - tq * i) // unit

        @pl.when((u >= u_min) & (u <= u_max))
        def _():
            for st, (hh, _) in enumerate(streams):
                s_out[st] = s_out[st] + bias_ref[u - u_min, hh]

    def half_step(i, j, carry, s_cur, s_nxt, nxt):
        ni, nj, nqs = nxt
        m_eff = [carry[st][0] - side_select(i, j, consts[hh][0], 0.0, consts[hh][1])
                 for st, (hh, _) in enumerate(streams)]
        tmax = [None] * n_str
        pv = [None] * n_str
        for r in range(n_chunks):
            rows = slice(r * ATTN_ROW_CHUNK, (r + 1) * ATTN_ROW_CHUNK)
            for st in range(n_str):
                t = score_rows(nqs, nj, r, s_nxt, st)
                tmax[st] = t if r == 0 else jnp.maximum(tmax[st], t)
            v0 = pl.multiple_of(j * tk + r * ATTN_ROW_CHUNK, ATTN_ROW_CHUNK)
            for st, (hh, _) in enumerate(streams):
                p = jnp.exp2((s_cur[st, rows, :] - m_eff[st]).astype(_BF))
                d = jnp.dot(vt_ref[0, hh, :, pl.ds(v0, ATTN_ROW_CHUNK)], p,
                            preferred_element_type=_F32)
                pv[st] = d if r == 0 else pv[st] + d
        new, tile_max = [], []
        for st, (hh, _) in enumerate(streams):
            m, alpha = carry[st]
            acc_ref[st] = alpha * acc_ref[st] + pv[st]
            cand = jnp.max(tmax[st], axis=0, keepdims=True) + max_bound(ni, nj, hh)
            m_nxt = jnp.maximum(m, cand)
            new.append((m_nxt, jnp.exp2(m - m_nxt)))
            tile_max.append(cand)
        add_near_bias(ni, nj, s_nxt)
        return tuple(new), tuple(tile_max)

    qs0 = stream_queries(0)
    first = []
    for st, (hh, _) in enumerate(streams):
        parts = [score_rows(qs0, 0, r, s_a, st) for r in range(n_chunks)]
        first.append(jnp.max(functools.reduce(jnp.maximum, parts), axis=0, keepdims=True)
                     + max_bound(0, 0, hh))
    add_near_bias(0, 0, s_a)
    lam = lam_ref[0]

    def query_tile(i, m_first):
        qs = stream_queries(i)
        acc_ref[...] = jnp.zeros_like(acc_ref)
        carry = tuple((m_first[st], jnp.zeros((1, tq), _F32)) for st in range(n_str))

        def pair(jj, carry):
            j = 2 * jj
            carry, _ = half_step(i, j, carry, s_a, s_b, (i, j + 1, qs))
            carry, _ = half_step(i, j + 1, carry, s_b, s_a, (i, j + 2, qs))
            return carry

        carry = lax.fori_loop(0, nkv // 2 - 1, pair, carry)
        carry, _ = half_step(i, nkv - 2, carry, s_a, s_b, (i, nkv - 1, qs))
        i_next = jnp.minimum(i + 1, nq - 1)
        _, m_next = half_step(i, nkv - 1, carry, s_b, s_a, (i_next, 0, stream_queries(i_next)))

        q0 = pl.multiple_of(i * tq, tq)
        for hh in range(heads):
            o0 = acc_ref[2 * hh, :DA_W] / acc_ref[2 * hh, DA_W:DA_W + 1]
            o1 = acc_ref[2 * hh + 1, :DA_W] / acc_ref[2 * hh + 1, DA_W:DA_W + 1]
            o = (o0 - lam * o1).T
            ms = jnp.mean(o * o, axis=-1, keepdims=True)
            o_ref[0, pl.ds(q0, tq), hh * DA_W:(hh + 1) * DA_W] = (
                o * lax.rsqrt(ms + LN_EPS) * subln_ref[...] * out_scale).astype(o_ref.dtype)
        return m_next

    lax.fori_loop(0, nq, query_tile, tuple(first))


def _attn_kernel_per_tile(lam_ref, far_ref, q_ref, k_ref, vt_ref, bias_ref, subln_ref, o_ref,
                          s_a, s_b, acc_ref, *, tq, tk, out_scale):
    heads = q_ref.shape[2] // DA_W
    streams = [(hh, mi) for hh in range(heads) for mi in range(2)]
    hp = pl.program_id(1)
    i = pl.program_id(2)
    nkv = k_ref.shape[1] // tk
    lane = lax.broadcasted_iota(jnp.int32, (1, heads * DA_W), 1)
    q = q_ref[0]
    zero = jnp.zeros_like(q)
    qs = [jnp.where((lane >= hh * DA_W + mi * DA_HD) & (lane < hh * DA_W + (mi + 1) * DA_HD), q, zero).T
          for hh, mi in streams]

    unit, u_min, u_max = _near_offsets(tq, tk)
    consts = [[far_ref[3 * (heads * hp + hh) + c] for c in range(3)] for hh in range(heads)]

    def offset(j):
        return (tk * j - tq * i) // unit

    def side_select(j, left, mid, right):
        u = offset(j)
        return jnp.where(u < u_min, left, jnp.where(u > u_max, right, mid))

    def score_rows(j, r, s_out, st):
        k0 = pl.multiple_of(j * tk + r * ATTN_ROW_CHUNK, ATTN_ROW_CHUNK)
        s = jnp.dot(k_ref[0, pl.ds(k0, ATTN_ROW_CHUNK), :], qs[st], preferred_element_type=_F32)
        s_out[st, r * ATTN_ROW_CHUNK:(r + 1) * ATTN_ROW_CHUNK, :] = s
        return _fold_rows(s, jnp.maximum)

    def max_bound(j, hh):
        return side_select(j, consts[hh][0], consts[hh][2], consts[hh][1])

    def add_near_bias(j, s_out):
        u = offset(j)

        @pl.when((u >= u_min) & (u <= u_max))
        def _():
            for st, (hh, _) in enumerate(streams):
                s_out[st] = s_out[st] + bias_ref[u - u_min, hh]

    def half_step(j, carry, s_cur, s_nxt):
        jn = jnp.minimum(j + 1, nkv - 1)
        m_eff = [carry[st][0] - side_select(j, consts[hh][0], 0.0, consts[hh][1])
                 for st, (hh, _) in enumerate(streams)]
        tmax = [None] * len(streams)
        pv = [None] * len(streams)
        for r in range(tk // ATTN_ROW_CHUNK):
            rows = slice(r * ATTN_ROW_CHUNK, (r + 1) * ATTN_ROW_CHUNK)
            for st in range(len(streams)):
                t = score_rows(jn, r, s_nxt, st)
                tmax[st] = t if r == 0 else jnp.maximum(tmax[st], t)
            v0 = pl.multiple_of(j * tk + r * ATTN_ROW_CHUNK, ATTN_ROW_CHUNK)
            for st, (hh, _) in enumerate(streams):
                p = jnp.exp2((s_cur[st, rows, :] - m_eff[st]).astype(_BF))
                d = jnp.dot(vt_ref[0, hh, :, pl.ds(v0, ATTN_ROW_CHUNK)], p, preferred_element_type=_F32)
                pv[st] = d if r == 0 else pv[st] + d
        new = []
        for st, (hh, _) in enumerate(streams):
            m, alpha = carry[st]
            acc_ref[st] = alpha * acc_ref[st] + pv[st]
            m_nxt = jnp.maximum(m, jnp.max(tmax[st], axis=0, keepdims=True) + max_bound(jn, hh))
            new.append((m_nxt, jnp.exp2(m - m_nxt)))
        add_near_bias(jn, s_nxt)
        return tuple(new)

    acc_ref[...] = jnp.zeros_like(acc_ref)
    init = []
    for st, (hh, _) in enumerate(streams):
        parts = [score_rows(0, r, s_a, st) for r in range(tk // ATTN_ROW_CHUNK)]
        m0 = jnp.max(functools.reduce(jnp.maximum, parts), axis=0, keepdims=True) + max_bound(0, hh)
        init.append((m0, jnp.zeros((1, tq), _F32)))
    add_near_bias(0, s_a)

    def pair(jj, carry):
        carry = half_step(2 * jj, carry, s_a, s_b)
        return half_step(2 * jj + 1, carry, s_b, s_a)

    lax.fori_loop(0, nkv // 2, pair, tuple(init))

    lam = lam_ref[0]
    for hh in range(heads):
        o0 = acc_ref[2 * hh, :DA_W] / acc_ref[2 * hh, DA_W:DA_W + 1]
        o1 = acc_ref[2 * hh + 1, :DA_W] / acc_ref[2 * hh + 1, DA_W:DA_W + 1]
        o = (o0 - lam * o1).T
        ms = jnp.mean(o * o, axis=-1, keepdims=True)
        o_ref[0, :, hh * DA_W:(hh + 1) * DA_W] = (
            o * lax.rsqrt(ms + LN_EPS) * subln_ref[...] * out_scale).astype(o_ref.dtype)


def _t5_bucket(rel):
    half = N_BUCKETS // 2
    max_exact = half // 2
    ret = jnp.where(rel > 0, half, 0)
    n = jnp.abs(rel)
    nf = jnp.maximum(n, 1).astype(_F32)
    large = max_exact + (jnp.log(nf / max_exact) / math.log(MAX_DISTANCE / max_exact)
                         * (half - max_exact)).astype(jnp.int32)
    large = jnp.minimum(large, half - 1)
    return ret + jnp.where(n < max_exact, n, large)


def _near_offsets(tq, tk):
    unit = math.gcd(tq, tk)
    first = -((tk + MAX_DISTANCE - 2) // unit)
    last = (tq + MAX_DISTANCE - 2) // unit
    return unit, first, last


def _bias_tiles(rel_bias, tq, tk):
    unit, u_min, u_max = _near_offsets(tq, tk)
    n_near = u_max - u_min + 1
    c = jnp.arange(tk, dtype=jnp.int32)[:, None]
    r = jnp.arange(tq, dtype=jnp.int32)[None, :]
    u = jnp.arange(u_min, u_max + 1, dtype=jnp.int32)[:, None, None]
    table = rel_bias.astype(_F32) * LOG2E
    bucket = _t5_bucket(u * unit + c - r)[:, None]
    tiles = jnp.zeros((n_near, DA_HEADS, tk, tq), _F32)
    for b in range(N_BUCKETS):
        tiles = jnp.where(bucket == b, table[b][None, :, None, None], tiles)
    far =table[_t5_bucket(jnp.array([-MAX_DISTANCE, MAX_DISTANCE], jnp.int32))]
    consts = jnp.concatenate([far, jnp.max(table, axis=0, keepdims=True)], axis=0)
    return tiles, consts.T.reshape(-1)


def _diff_attention(p3, vt, lam, subln, bias, far, layer, tq, tk):
    b, s, _ = p3.shape
    vrows = vt.shape[2]
    assert s % (2 * tk) == 0
    lam_init = 0.8 - 0.6 * math.exp(-0.3 * layer)
    kern = functools.partial(_attn_kernel, tq=tq, tk=tk, out_scale=1.0 - lam_init)
    nh = ATTN_HEADS_PER_STEP
    groups = DA_HEADS // nh
    return pl.pallas_call(
        kern,
        grid=(b, groups),
        in_specs=[pl.BlockSpec(memory_space=pltpu.SMEM),
                  pl.BlockSpec(memory_space=pltpu.SMEM),
                  pl.BlockSpec((1, s, nh * DA_W), lambda bi, h: (bi, 0, h)),
                  pl.BlockSpec((1, s, nh * DA_W), lambda bi, h: (bi, 0, groups + h)),
                  pl.BlockSpec((1, nh, vrows, s), lambda bi, h: (bi, h, 0, 0)),
                  pl.BlockSpec((bias.shape[0], nh, tk, tq), lambda bi, h: (0, h, 0, 0)),
                  pl.BlockSpec((1, DA_W), lambda bi, h: (0, 0))],
        out_specs=pl.BlockSpec((1, s, nh * DA_W), lambda bi, h: (bi, 0, h)),
        out_shape=jax.ShapeDtypeStruct((b, s, MIX_W), _BF),
        scratch_shapes=[pltpu.VMEM((2 * nh, tk, tq), _F32), pltpu.VMEM((2 * nh, tk, tq), _F32),
                        pltpu.VMEM((2 * nh, vrows, tq), _F32)],
        compiler_params=_params("parallel", "parallel"),
        name="diff_attn",
    )(lam, far, p3, p3, vt, bias, subln)


def _hgrn_chunk_stages(z, q, v, lb, st, tri, rev):
    c_rows = z.shape[0]
    f = lb + (1.0 - lb) * _sigmoid(z)
    kk = (1.0 - lb) * _sigmoid(-z)
    logf = jnp.log(f)
    hi = logf.astype(_BF)
    lo = (logf - hi.astype(_F32)).astype(_BF)
    cum = (jnp.dot(tri, hi, preferred_element_type=_F32)
           + jnp.dot(tri, lo, preferred_element_type=_F32))
    qh = _silu(q) * (HG_DK ** -0.5)
    vb = v.astype(_BF)
    yield None
    total = cum[0:1, :] if rev else cum[c_rows - 1:c_rows, :]
    o_inter = lax.dot_general((qh * jnp.exp(cum)).astype(_BF), st.astype(_BF), _NT,
                              preferred_element_type=_F32)
    ks = (kk * jnp.exp(total - cum)).astype(_BF)
    st_new = st * jnp.exp(total) + jnp.dot(v.T.astype(_BF), ks, preferred_element_type=_F32)
    blocks = []
    for b in range(c_rows // HG_SUB):
        lo_r, hi_r = b * HG_SUB, (b + 1) * HG_SUB
        if rev:
            cols = slice(lo_r, c_rows)
            base = cum[hi_r:hi_r + 1, :] if hi_r < c_rows else jnp.zeros_like(total)
        else:
            cols = slice(0, hi_r)
            base = cum[lo_r - 1:lo_r, :] if b else jnp.zeros_like(total)
        qq = (qh[lo_r:hi_r] * jnp.exp(cum[lo_r:hi_r] - base)).astype(_BF)
        kt = (kk[cols] * jnp.exp(base - cum[cols])).astype(_BF)
        a = lax.dot_general(qq, kt, _NT, preferred_element_type=_F32)
        blocks.append((lo_r, hi_r, cols, a))
    yield None
    parts = []
    for lo_r, hi_r, cols, a in blocks:
        sub_row = lax.broadcasted_iota(jnp.int32, a.shape, 0) + lo_r
        sub_col = lax.broadcasted_iota(jnp.int32, a.shape, 1) + cols.start
        seen = (sub_col >= sub_row) if rev else (sub_col <= sub_row)
        a = jnp.where(seen, a, 0.0)
        parts.append(o_inter[lo_r:hi_r]
                     + jnp.dot(a.astype(_BF), vb[cols], preferred_element_type=_F32))
    yield jnp.concatenate(parts, axis=0), st_new


def _hgrn_kernel(qf_ref, vf_ref, zf_ref, qb_ref, vb_ref, zb_ref, lb_ref, of_ref, ob_ref, st_ref,
                 *, rows):
    @pl.when(pl.program_id(1) == 0)
    def _():
        st_ref[...] = jnp.zeros_like(st_ref)

    c_rows = HG_CHUNK
    n_chunks = rows // c_rows
    row = lax.broadcasted_iota(jnp.int32, (c_rows, c_rows), 0)
    col = lax.broadcasted_iota(jnp.int32, (c_rows, c_rows), 1)
    tris = ((col <= row).astype(_BF), (col >= row).astype(_BF))
    dirs = ((qf_ref, vf_ref, zf_ref, of_ref), (qb_ref, vb_ref, zb_ref, ob_ref))

    def chunk(c, carry):
        starts = (pl.multiple_of(c * c_rows, c_rows),
                  pl.multiple_of((n_chunks - 1 - c) * c_rows, c_rows))
        for h0 in range(0, HG_HEADS, HG_HEAD_GROUP):
            chains = []
            for h in range(h0, h0 + HG_HEAD_GROUP):
                cs = slice(h * HG_DK, (h + 1) * HG_DK)
                for d, (q_ref, v_ref, z_ref, o_ref) in enumerate(dirs):
                    rs = pl.ds(starts[d], c_rows)
                    gen = _hgrn_chunk_stages(z_ref[0, rs, cs], q_ref[0, rs, cs], v_ref[0, rs, cs],
                                             lb_ref[d, :, cs], st_ref[d, h], tris[d], rev=bool(d))
                    chains.append((gen, o_ref, rs, cs, d, h))
            for _ in range(2):
                for chain in chains:
                    next(chain[0])
            for gen, o_ref, rs, cs, d, h in chains:
                o, st = next(gen)
                o_ref[0, rs, cs] = o
                st_ref[d, h] = st
        return carry

    lax.fori_loop(0, n_chunks, chunk, 0)


def _hgrn_scan(p3, lb, rows):
    b, s, _ = p3.shape
    w = MIX_W
    t = s // rows
    fwd = lambda c: pl.BlockSpec((1, rows, w), lambda bi, i: (bi, i, c))
    bwd = lambda c: pl.BlockSpec((1, rows, w), lambda bi, i: (bi, t - 1 - i, c))
    out = jax.ShapeDtypeStruct((b, s, w), _F32)
    return pl.pallas_call(
        functools.partial(_hgrn_kernel, rows=rows),
        grid=(b, t),
        in_specs=[fwd(0), fwd(1), fwd(3), bwd(0), bwd(1), bwd(4),
                  pl.BlockSpec((2, 1, w), lambda bi, i: (0, 0, 0))],
        out_specs=[fwd(0), bwd(0)],
        out_shape=[out, out],
        scratch_shapes=[pltpu.VMEM((2, HG_HEADS, HG_DK, HG_DK), _F32)],
        compiler_params=_params("parallel", "arbitrary"),
        name="hgrn_scan",
    )(p3, p3, p3, p3, p3, p3, lb)


def _hgrn_finish_kernel(of_ref, ob_ref, g_ref, w_ref, o_ref):
    o = of_ref[...] + ob_ref[...]
    gate = _silu(g_ref[...])
    w = w_ref[...]
    for h in range(HG_HEADS):
        cs = slice(h * HG_DK, (h + 1) * HG_DK)
        oh = o[:, cs]
        ms = jnp.mean(oh * oh, axis=-1, keepdims=True)
        o_ref[:, cs] = (oh * lax.rsqrt(ms + LN_EPS) * w[:, cs] * gate[:, cs]).astype(o_ref.dtype)


def _hgrn_finish(o_fw, o_bw, p2, g_col, norm_w, tm):
    n = o_fw.shape[0]
    return pl.pallas_call(
        _hgrn_finish_kernel,
        grid=(n // tm,),
        in_specs=[pl.BlockSpec((tm, MIX_W), lambda i: (i, 0)),
                  pl.BlockSpec((tm, MIX_W), lambda i: (i, 0)),
                  pl.BlockSpec((tm, MIX_W), lambda i: (i, g_col)),
                  pl.BlockSpec((1, MIX_W), lambda i: (0, 0))],
        out_specs=pl.BlockSpec((tm, MIX_W), lambda i: (i, 0)),
        out_shape=jax.ShapeDtypeStruct((n, MIX_W), _BF),
        compiler_params=_params("parallel"),
        name="hgrn_finish",
    )(o_fw, o_bw, p2, norm_w)


def _sconv_kernel(gb_ref, gc_ref, h_ref, gcp_ref, hp_ref, gcn_ref, hn_ref, w_ref, o_ref, *, halo):
    i = pl.program_id(1)
    u = gc_ref[0].astype(_F32) * h_ref[0].astype(_F32)
    n = u.shape[0]
    u_prev = (gcp_ref[0].astype(_F32) * hp_ref[0].astype(_F32))[halo - 1:halo]
    u_next = (gcn_ref[0].astype(_F32) * hn_ref[0].astype(_F32))[0:1]
    u_prev = jnp.where(i == 0, 0.0, u_prev)
    u_next = jnp.where(i == pl.num_programs(1) - 1, 0.0, u_next)
    row = lax.broadcasted_iota(jnp.int32, u.shape, 0)
    down = jnp.where(row == 0, u_prev, pltpu.roll(u, 1, 0))
    up = jnp.where(row == n - 1, u_next, pltpu.roll(u, n - 1, 0))
    w = w_ref[...]
    y = down * w[0:1] + u * w[1:2] + up * w[2:3]
    o_ref[0] = (gb_ref[0].astype(_F32) * y).astype(o_ref.dtype)


def _short_conv(p3, conv_w, ts):
    b, s, _ = p3.shape
    halo = SUBLANES_BF16
    hb = ts // halo
    last = s // halo - 1
    main = lambda c: pl.BlockSpec((1, ts, MIX_W), lambda bi, i: (bi, i, c))
    prev = lambda c: pl.BlockSpec((1, halo, MIX_W), lambda bi, i: (bi, jnp.maximum(i * hb - 1, 0), c))
    nxt = lambda c: pl.BlockSpec((1, halo, MIX_W), lambda bi, i: (bi, jnp.minimum((i + 1) * hb, last), c))
    return pl.pallas_call(
        functools.partial(_sconv_kernel, halo=halo),
        grid=(b, s // ts),
        in_specs=[main(0), main(1), main(2), prev(1), prev(2), nxt(1), nxt(2),
                  pl.BlockSpec((3, MIX_W), lambda bi, i: (0, 0))],
        out_specs=pl.BlockSpec((1, ts, MIX_W), lambda bi, i: (bi, i, 0)),
        out_shape=jax.ShapeDtypeStruct((b, s, MIX_W), _BF),
        compiler_params=_params("parallel", "parallel"),
        name="short_conv",
    )(p3, p3, p3, p3, p3, p3, p3, conv_w)


def _tile(n, pref):
    return pref if n % pref == 0 else n


def kernel(x, mem, rel_bias, attn_w_in, attn_lambda, attn_subln, hgrn_w_in, hgrn_lower_bound,
           hgrn_norm, conv_w_in, conv_w, mem_w_kv, w_o, ln_gain, ln_bias, ffn_w_up, ffn_conv,
           ffn_w_down):
    b, s, d = x.shape
    n = b * s
    m = mem.shape[1]
    tm = _tile(s, ROW_TILE)
    hg_rows = _tile(s, HG_ROWS)

    bias, far = _bias_tiles(rel_bias, ATTN_Q_TILE, ATTN_K_TILE)
    mem2 = mem.reshape(b * m, d)
    lbw = jax.nn.softmax(hgrn_lower_bound.astype(_F32), axis=0)
    lb_all = jnp.cumsum(lbw, axis=0) - lbw[0]

    for layer in range(DEPTH):
        kind, j = layer % N_MIXERS, layer // N_MIXERS
        x2 = x.reshape(n, d)
        kv = _matmul(mem2, mem_w_kv[layer].astype(_BF), _BF, _tile(b * m, ROW_TILE))
        kv = kv.reshape(b, m, 2 * XATTN_W)
        if kind == 0:
            q_scale = jnp.where(jnp.arange(attn_w_in.shape[2]) < MIX_W, DA_HD ** -0.5 * LOG2E, 1.0)
            w_in = (attn_w_in[j] * q_scale).astype(_BF)
            p = _matmul(x2, w_in, _BF, tm).reshape(b, s, -1)
            lp = attn_lambda[j].astype(_F32)
            lam_init = 0.8 - 0.6 * math.exp(-0.3 * layer)
            lam = jnp.exp(jnp.sum(lp[0] * lp[1])) - jnp.exp(jnp.sum(lp[2] * lp[3])) + lam_init
            vt = jnp.swapaxes(p[..., 2 * MIX_W:3 * MIX_W], 1, 2).reshape(b, DA_HEADS, DA_W, s)
            vt = jnp.concatenate([vt, jnp.ones((b, DA_HEADS, SUBLANES_BF16, s), _BF)], axis=2)
            mixed = _diff_attention(p, vt, lam.reshape(1),
                                    attn_subln[j].astype(_F32).reshape(1, DA_W),
                                    bias, far, layer, ATTN_Q_TILE, ATTN_K_TILE)
            q_col = 3 * MIX_W // XATTN_W
        elif kind == 1:
            w_in = hgrn_w_in[j].astype(_BF)
            p = _matmul(x2, w_in, _F32, tm).reshape(b, s, -1)
            o_fw, o_bw = _hgrn_scan(p, lb_all[layer].reshape(2, 1, MIX_W), hg_rows)
            mixed = _hgrn_finish(o_fw.reshape(n, MIX_W), o_bw.reshape(n, MIX_W),
                                 p.reshape(n, -1), 2,
                                 hgrn_norm[j].astype(_F32).reshape(1, MIX_W), tm)
            q_col = 5 * MIX_W // XATTN_W
        else:
            w_in = conv_w_in[j].astype(_BF)
            p = _matmul(x2, w_in, _BF, tm).reshape(b, s, -1)
            mixed = _short_conv(p, conv_w[j].astype(_F32), tm)
            q_col = 3 * MIX_W // XATTN_W
        recalled = _memory_attention(p, kv, q_col, tm)
        x2 = _out_ln(x2, mixed.reshape(n, MIX_W), recalled.reshape(n, XATTN_W),
                     w_o[layer].astype(_BF), ln_gain[layer, 0].reshape(1, d),
                     ln_bias[layer, 0].reshape(1, d), tm)
        x = _conv_ffn_ln(x2.reshape(b, s, d), ffn_w_up[layer].astype(_BF),
                         ffn_conv[layer].astype(_F32), ffn_w_down[layer].astype(_BF),
                         ln_gain[layer, 1].reshape(1, d), ln_bias[layer, 1].reshape(1, d),
                         tm, FFN_TILE)
    return x
```

```python
import functools
import math

import jax
import jax.numpy as jnp
from jax import lax
from jax.experimental import pallas as pl
from jax.experimental.pallas import tpu as pltpu

D_MODEL = 1024
DEPTH = 4
N_MIXERS = 3
MIX_W = 3 * D_MODEL // 4
XATTN_HEADS = 4
XATTN_W = D_MODEL - MIX_W
XATTN_HD = XATTN_W // XATTN_HEADS
DA_HD = 64
DA_HEADS = MIX_W // (2 * DA_HD)
DA_W = 2 * DA_HD
N_BUCKETS = 32
MAX_DISTANCE = 128
HG_DK = 128
HG_HEADS = MIX_W // HG_DK
HG_CHUNK = 64
HG_SUB = 16
D_FF = ((8 * D_MODEL // 3 + 255) // 256) * 256
LN_EPS = 1e-5
ALPHA = (2 * DEPTH) ** 0.25
LOG2E = math.log2(math.e)

LANES = 128
SUBLANES_F32 = 8
SUBLANES_BF16 = 16
VMEM_LIMIT = 52 * 1024 * 1024

ROW_TILE = 512
ATTN_Q_TILE = 256
ATTN_K_TILE = 1024
ATTN_HEADS_PER_STEP = 1
ATTN_ROW_CHUNK = 256
MXU_WIDTH = 256
FFN_TILE = MXU_WIDTH
HG_ROWS = 512
HG_HEAD_GROUP = 3

_NEG = -1e30
_BF = jnp.bfloat16
_F32 = jnp.float32
_NT = (((1,), (1,)), ((), ()))


def _params(*sem):
    return pltpu.CompilerParams(dimension_semantics=sem, vmem_limit_bytes=VMEM_LIMIT)


def _sigmoid(z):
    e = jnp.exp(-jnp.abs(z))
    r = 1.0 / (1.0 + e)
    return jnp.where(z >= 0, r, e * r)


def _silu(z):
    return z * _sigmoid(z)


def _layer_norm(r, g, b):
    mu = jnp.mean(r, axis=-1, keepdims=True)
    rc = r - mu
    var = jnp.mean(rc * rc, axis=-1, keepdims=True)
    return rc * lax.rsqrt(var + LN_EPS) * g + b


def _mm_kernel(x_ref, w_ref, o_ref):
    o_ref[...] = jnp.dot(x_ref[...].astype(_BF), w_ref[...],
                         preferred_element_type=_F32).astype(o_ref.dtype)


def _matmul(x, w, out_dtype, tm):
    m, k = x.shape
    n = w.shape[1]
    return pl.pallas_call(
        _mm_kernel,
        grid=(m // tm,),
        in_specs=[pl.BlockSpec((tm, k), lambda i: (i, 0)),
                  pl.BlockSpec((k, n), lambda i: (0, 0), pipeline_mode=pl.Buffered(1))],
        out_specs=pl.BlockSpec((tm, n), lambda i: (i, 0)),
        out_shape=jax.ShapeDtypeStruct((m, n), out_dtype),
        compiler_params=_params("parallel"),
        name="proj",
    )(x, w)


def _memattn_kernel(q_ref, km_ref, vm_ref, o_ref):
    q = q_ref[0].astype(_BF)
    km = km_ref[0]
    vm = vm_ref[0]
    lane = lax.broadcasted_iota(jnp.int32, (1, XATTN_W), 1)
    acc = jnp.zeros(q.shape, _F32)
    for h in range(XATTN_HEADS):
        head = (lane >= h * XATTN_HD) & (lane < (h + 1) * XATTN_HD)
        qh = jnp.where(head, q, jnp.zeros_like(q))
        s = lax.dot_general(qh, km, _NT, preferred_element_type=_F32) * (XATTN_HD ** -0.5)
        p = jnp.exp(s - jnp.max(s, axis=-1, keepdims=True))
        l = jnp.sum(p, axis=-1, keepdims=True)
        vh = jnp.where(head, vm, jnp.zeros_like(vm))
        acc = acc + jnp.dot(p.astype(_BF), vh, preferred_element_type=_F32) / l
    o_ref[0] = acc.astype(o_ref.dtype)


def _memory_attention(p3, kv, q_col, tm):
    b, s, _ = p3.shape
    m = kv.shape[1]
    return pl.pallas_call(
        _memattn_kernel,
        grid=(b, s // tm),
        in_specs=[pl.BlockSpec((1, tm, XATTN_W), lambda bi, i: (bi, i, q_col)),
                  pl.BlockSpec((1, m, XATTN_W), lambda bi, i: (bi, 0, 0)),
                  pl.BlockSpec((1, m, XATTN_W), lambda bi, i: (bi, 0, 1))],
        out_specs=pl.BlockSpec((1, tm, XATTN_W), lambda bi, i: (bi, i, 0)),
        out_shape=jax.ShapeDtypeStruct((b, s, XATTN_W), _BF),
        compiler_params=_params("parallel", "parallel"),
        name="memattn",
    )(p3, kv, kv)


def _out_ln_kernel(x_ref, mix_ref, rec_ref, wo_ref, g_ref, b_ref, o_ref):
    y = jnp.dot(mix_ref[...], wo_ref[:MIX_W, :], preferred_element_type=_F32)
    y = y + jnp.dot(rec_ref[...], wo_ref[MIX_W:, :], preferred_element_type=_F32)
    o_ref[...] = _layer_norm(ALPHA * x_ref[...] + y, g_ref[...], b_ref[...])


def _out_ln(x2, mixed, recalled, wo, g, b, tm):
    n = x2.shape[0]
    return pl.pallas_call(
        _out_ln_kernel,
        grid=(n // tm,),
        in_specs=[pl.BlockSpec((tm, D_MODEL), lambda i: (i, 0)),
                  pl.BlockSpec((tm, MIX_W), lambda i: (i, 0)),
                  pl.BlockSpec((tm, XATTN_W), lambda i: (i, 0)),
                  pl.BlockSpec((D_MODEL, D_MODEL), lambda i: (0, 0)),
                  pl.BlockSpec((1, D_MODEL), lambda i: (0, 0)),
                  pl.BlockSpec((1, D_MODEL), lambda i: (0, 0))],
        out_specs=pl.BlockSpec((tm, D_MODEL), lambda i: (i, 0)),
        out_shape=jax.ShapeDtypeStruct((n, D_MODEL), _F32),
        compiler_params=_params("parallel"),
        name="out_ln",
    )(x2, mixed, recalled, wo, g, b)


def _dwconv_rows(h, w):
    n = h.shape[0]
    return (pltpu.roll(h, 1, 0) * w[0:1] + h * w[1:2] + pltpu.roll(h, n - 1, 0) * w[2:3])


def _ffn_kernel(x_ref, xp_ref, xn_ref, wu_ref, wc_ref, wd_ref, g_ref, b_ref, o_ref, gated_ref,
                *, ts, halo, fb):
    i = pl.program_id(1)
    prev = jnp.where(i == 0, 0.0, xp_ref[0])
    nxt = jnp.where(i == pl.num_programs(1) - 1, 0.0, xn_ref[0])
    xb = jnp.concatenate([prev, x_ref[0], nxt], axis=0).astype(_BF)
    for c in range(D_FF // fb):
        ca = slice(c * fb, (c + 1) * fb)
        cv = slice(D_FF + c * fb, D_FF + (c + 1) * fb)
        ha = jnp.dot(xb, wu_ref[:, ca], preferred_element_type=_F32)
        hv = jnp.dot(xb, wu_ref[:, cv], preferred_element_type=_F32)
        a = _dwconv_rows(ha, wc_ref[:, ca])[halo:halo + ts]
        v = _dwconv_rows(hv, wc_ref[:, cv])[halo:halo + ts]
        gated_ref[:, ca] = (_silu(a) * v).astype(_BF)
    f = jnp.dot(gated_ref[...], wd_ref[...], preferred_element_type=_F32)
    o_ref[0] = _layer_norm(ALPHA * x_ref[0] + f, g_ref[...], b_ref[...])


def _conv_ffn_ln(x3, w_up, w_conv, w_down, g, b, ts, fb):
    bsz, s, d = x3.shape
    halo = SUBLANES_BF16
    hb = ts // halo
    last = s // halo - 1
    kern = functools.partial(_ffn_kernel, ts=ts, halo=halo, fb=fb)
    resident = lambda shape: pl.BlockSpec(shape, lambda bi, i: (0, 0),
                                          pipeline_mode=pl.Buffered(1))
    return pl.pallas_call(
        kern,
        grid=(bsz, s // ts),
        in_specs=[pl.BlockSpec((1, ts, d), lambda bi, i: (bi, i, 0)),
                  pl.BlockSpec((1, halo, d), lambda bi, i: (bi, jnp.maximum(i * hb - 1, 0), 0)),
                  pl.BlockSpec((1, halo, d), lambda bi, i: (bi, jnp.minimum((i + 1) * hb, last), 0)),
                  resident((d, 2 * D_FF)),
                  resident((3, 2 * D_FF)),
                  resident((D_FF, d)),
                  pl.BlockSpec((1, d), lambda bi, i: (0, 0)),
                  pl.BlockSpec((1, d), lambda bi, i: (0, 0))],
        out_specs=pl.BlockSpec((1, ts, d), lambda bi, i: (bi, i, 0)),
        out_shape=jax.ShapeDtypeStruct((bsz, s, d), _F32),
        scratch_shapes=[pltpu.VMEM((ts, D_FF), _BF)],
        compiler_params=_params("parallel", "parallel"),
        name="conv_ffn",
    )(x3, x3, x3, w_up, w_conv, w_down, g, b)


def _fold_rows(x, op, group=SUBLANES_F32):
    acc = x[:group]
    for g in range(1, x.shape[0] // group):
        acc = op(acc, x[g * group:(g + 1) * group])
    while acc.shape[0] > SUBLANES_F32:
        half = acc.shape[0] // 2
        acc = op(acc[:half], acc[half:])
    return acc


def _attn_kernel(lam_ref, far_ref, q_ref, k_ref, vt_ref, bias_ref, subln_ref, o_ref,
                 s_a, s_b, acc_ref, *, tq, tk, out_scale):
    heads = q_ref.shape[2] // DA_W
    streams = [(hh, mi) for hh in range(heads) for mi in range(2)]
    n_str = len(streams)
    hp = pl.program_id(1)
    nkv = k_ref.shape[1] // tk
    nq = q_ref.shape[1] // tq
    n_chunks = tk // ATTN_ROW_CHUNK
    lane = lax.broadcasted_iota(jnp.int32, (1, heads * DA_W), 1)

    unit, u_min, u_max = _near_offsets(tq, tk)
    consts = [[far_ref[3 * (heads * hp + hh) + c] for c in range(3)] for hh in range(heads)]

    def stream_queries(i):
        q = q_ref[0, pl.ds(pl.multiple_of(i * tq, tq), tq), :]
        zero = jnp.zeros_like(q)
        return [jnp.where((lane >= hh * DA_W + mi * DA_HD) & (lane < hh * DA_W + (mi + 1) * DA_HD),
                          q, zero).T for hh, mi in streams]

    def side_select(i, j, left, mid, right):
        u = (tk * j - tq * i) // unit
        return jnp.where(u < u_min, left, jnp.where(u > u_max, right, mid))

    def max_bound(i, j, hh):
        return side_select(i, j, consts[hh][0], consts[hh][2], consts[hh][1])

    def score_rows(qs, j, r, s_out, st):
        k0 = pl.multiple_of(j * tk + r * ATTN_ROW_CHUNK, ATTN_ROW_CHUNK)
        s = jnp.dot(k_ref[0, pl.ds(k0, ATTN_ROW_CHUNK), :], qs[st], preferred_element_type=_F32)
        s_out[st, r * ATTN_ROW_CHUNK:(r + 1) * ATTN_ROW_CHUNK, :] = s
        return _fold_rows(s, jnp.maximum)

    def add_near_bias(i, j, s_out):
        u = (tk * j - tq * i) // unit

        @pl.when((u >= u_min) & (u <= u_max))
        def _():
            for st, (hh, _) in enumerate(streams):
                s_out[st] = s_out[st] + bias_ref[u - u_min, hh]

    def half_step(i, j, carry, s_cur, s_nxt, nxt):
        ni, nj, nqs = nxt
        m_eff = [carry[st][0] - side_select(i, j, consts[hh][0], 0.0, consts[hh][1])
                 for st, (hh, _) in enumerate(streams)]
        tmax = [None] * n_str
        pv = [None] * n_str
        for r in range(n_chunks):
            rows = slice(r * ATTN_ROW_CHUNK, (r + 1) * ATTN_ROW_CHUNK)
            for st in range(n_str):
                t = score_rows(nqs, nj, r, s_nxt, st)
                tmax[st] = t if r == 0 else jnp.maximum(tmax[st], t)
            v0 = pl.multiple_of(j * tk + r * ATTN_ROW_CHUNK, ATTN_ROW_CHUNK)
            for st, (hh, _) in enumerate(streams):
                p = jnp.exp2(s_cur[st, rows, :] - m_eff[st]).astype(_BF)
                d = jnp.dot(vt_ref[0, hh, :, pl.ds(v0, ATTN_ROW_CHUNK)], p,
                            preferred_element_type=_F32)
                pv[st] = d if r == 0 else pv[st] + d
        new, tile_max = [], []
        for st, (hh, _) in enumerate(streams):
            m, alpha = carry[st]
            acc_ref[st] = alpha * acc_ref[st] + pv[st]
            cand = jnp.max(tmax[st], axis=0, keepdims=True) + max_bound(ni, nj, hh)
            m_nxt = jnp.maximum(m, cand)
            new.append((m_nxt, jnp.exp2(m - m_nxt)))
            tile_max.append(cand)
        add_near_bias(ni, nj, s_nxt)
        return tuple(new), tuple(tile_max)

    qs0 = stream_queries(0)
    first = []
    for st, (hh, _) in enumerate(streams):
        parts = [score_rows(qs0, 0, r, s_a, st) for r in range(n_chunks)]
        first.append(jnp.max(functools.reduce(jnp.maximum, parts), axis=0, keepdims=True)
                     + max_bound(0, 0, hh))
    add_near_bias(0, 0, s_a)
    lam = lam_ref[0]

    def query_tile(i, m_first):
        qs = stream_queries(i)
        acc_ref[...] = jnp.zeros_like(acc_ref)
        carry = tuple((m_first[st], jnp.zeros((1, tq), _F32)) for st in range(n_str))

        def pair(jj, carry):
            j = 2 * jj
            carry, _ = half_step(i, j, carry, s_a, s_b, (i, j + 1, qs))
            carry, _ = half_step(i, j + 1, carry, s_b, s_a, (i, j + 2, qs))
            return carry

        carry = lax.fori_loop(0, nkv // 2 - 1, pair, carry)
        carry, _ = half_step(i, nkv - 2, carry, s_a, s_b, (i, nkv - 1, qs))
        i_next = jnp.minimum(i + 1, nq - 1)
        _, m_next = half_step(i, nkv - 1, carry, s_b, s_a, (i_next, 0, stream_queries(i_next)))

        q0 = pl.multiple_of(i * tq, tq)
        for hh in range(heads):
            o0 = acc_ref[2 * hh, :DA_W] / acc_ref[2 * hh, DA_W:DA_W + 1]
            o1 = acc_ref[2 * hh + 1, :DA_W] / acc_ref[2 * hh + 1, DA_W:DA_W + 1]
            o = (o0 - lam * o1).T
            ms = jnp.mean(o * o, axis=-1, keepdims=True)
            o_ref[0, pl.ds(q0, tq), hh * DA_W:(hh + 1) * DA_W] = (
                o * lax.rsqrt(ms + LN_EPS) * subln_ref[...] * out_scale).astype(o_ref.dtype)
        return m_next

    lax.fori_loop(0, nq, query_tile, tuple(first))


def _t5_bucket(rel):
    half = N_BUCKETS // 2
    max_exact = half // 2
    ret = jnp.where(rel > 0, half, 0)
    n = jnp.abs(rel)
    nf = jnp.maximum(n, 1).astype(_F32)
    large = max_exact + (jnp.log(nf / max_exact) / math.log(MAX_DISTANCE / max_exact)
                         * (half - max_exact)).astype(jnp.int32)
    large = jnp.minimum(large, half - 1)
    return ret + jnp.where(n < max_exact, n, large)


def _near_offsets(tq, tk):
    unit = math.gcd(tq, tk)
    first = -((tk + MAX_DISTANCE - 2) // unit)
    last = (tq + MAX_DISTANCE - 2) // unit
    return unit, first, last


def _bias_tiles(rel_bias, tq, tk):
    unit, u_min, u_max = _near_offsets(tq, tk)
    n_near = u_max - u_min + 1
    c = jnp.arange(tk, dtype=jnp.int32)[:, None]
    r = jnp.arange(tq, dtype=jnp.int32)[None, :]
    u = jnp.arange(u_min, u_max + 1, dtype=jnp.int32)[:, None, None]
    table = rel_bias.astype(_F32) * LOG2E
    bucket = _t5_bucket(u * unit + c - r)[:, None]
    tiles = jnp.zeros((n_near, DA_HEADS, tk, tq), _F32)
    for b in range(N_BUCKETS):
        tiles = jnp.where(bucket == b, table[b][None, :, None, None], tiles)
    far =table[_t5_bucket(jnp.array([-MAX_DISTANCE, MAX_DISTANCE], jnp.int32))]
    consts = jnp.concatenate([far, jnp.max(table, axis=0, keepdims=True)], axis=0)
    return tiles, consts.T.reshape(-1)


def _diff_attention(p3, vt, lam, subln, bias, far, layer, tq, tk):
    b, s, _ = p3.shape
    vrows = vt.shape[2]
    assert s % (2 * tk) == 0
    lam_init = 0.8 - 0.6 * math.exp(-0.3 * layer)
    kern = functools.partial(_attn_kernel, tq=tq, tk=tk, out_scale=1.0 - lam_init)
    nh = ATTN_HEADS_PER_STEP
    groups = DA_HEADS // nh
    return pl.pallas_call(
        kern,
        grid=(b, groups),
        in_specs=[pl.BlockSpec(memory_space=pltpu.SMEM),
                  pl.BlockSpec(memory_space=pltpu.SMEM),
                  pl.BlockSpec((1, s, nh * DA_W), lambda bi, h: (bi, 0, h)),
                  pl.BlockSpec((1, s, nh * DA_W), lambda bi, h: (bi, 0, groups + h)),
                  pl.BlockSpec((1, nh, vrows, s), lambda bi, h: (bi, h, 0, 0)),
                  pl.BlockSpec((bias.shape[0], nh, tk, tq), lambda bi, h: (0, h, 0, 0)),
                  pl.BlockSpec((1, DA_W), lambda bi, h: (0, 0))],
        out_specs=pl.BlockSpec((1, s, nh * DA_W), lambda bi, h: (bi, 0, h)),
        out_shape=jax.ShapeDtypeStruct((b, s, MIX_W), _BF),
        scratch_shapes=[pltpu.VMEM((2 * nh, tk, tq), _F32), pltpu.VMEM((2 * nh, tk, tq), _F32),
                        pltpu.VMEM((2 * nh, vrows, tq), _F32)],
        compiler_params=_params("parallel", "parallel"),
        name="diff_attn",
    )(lam, far, p3, p3, vt, bias, subln)


def _hgrn_chunk_stages(z, q, v, lb, st, tri, rev):
    c_rows = z.shape[0]
    f = lb + (1.0 - lb) * _sigmoid(z)
    kk = (1.0 - lb) * _sigmoid(-z)
    logf = jnp.log(f)
    hi = logf.astype(_BF)
    lo = (logf - hi.astype(_F32)).astype(_BF)
    cum = (jnp.dot(tri, hi, preferred_element_type=_F32)
           + jnp.dot(tri, lo, preferred_element_type=_F32))
    qh = _silu(q) * (HG_DK ** -0.5)
    vb = v.astype(_BF)
    yield None
    total = cum[0:1, :] if rev else cum[c_rows - 1:c_rows, :]
    o_inter = lax.dot_general((qh * jnp.exp(cum)).astype(_BF), st.astype(_BF), _NT,
                              preferred_element_type=_F32)
    ks = (kk * jnp.exp(total - cum)).astype(_BF)
    st_new = st * jnp.exp(total) + jnp.dot(v.T.astype(_BF), ks, preferred_element_type=_F32)
    blocks = []
    for b in range(c_rows // HG_SUB):
        lo_r, hi_r = b * HG_SUB, (b + 1) * HG_SUB
        if rev:
            cols = slice(lo_r, c_rows)
            base = cum[hi_r:hi_r + 1, :] if hi_r < c_rows else jnp.zeros_like(total)
        else:
            cols = slice(0, hi_r)
            base = cum[lo_r - 1:lo_r, :] if b else jnp.zeros_like(total)
        qq = (qh[lo_r:hi_r] * jnp.exp(cum[lo_r:hi_r] - base)).astype(_BF)
        kt = (kk[cols] * jnp.exp(base - cum[cols])).astype(_BF)
        a = lax.dot_general(qq, kt, _NT, preferred_element_type=_F32)
        blocks.append((lo_r, hi_r, cols, a))
    yield None
    parts = []
    for lo_r, hi_r, cols, a in blocks:
        sub_row = lax.broadcasted_iota(jnp.int32, a.shape, 0) + lo_r
        sub_col = lax.broadcasted_iota(jnp.int32, a.shape, 1) + cols.start
        seen = (sub_col >= sub_row) if rev else (sub_col <= sub_row)
        a = jnp.where(seen, a, 0.0)
        parts.append(o_inter[lo_r:hi_r]
                     + jnp.dot(a.astype(_BF), vb[cols], preferred_element_type=_F32))
    yield jnp.concatenate(parts, axis=0), st_new


def _hgrn_kernel(qf_ref, vf_ref, zf_ref, qb_ref, vb_ref, zb_ref, lb_ref, of_ref, ob_ref, st_ref,
                 *, rows):
    @pl.when(pl.program_id(1) == 0)
    def _():
        st_ref[...] = jnp.zeros_like(st_ref)

    c_rows = HG_CHUNK
    n_chunks = rows // c_rows
    row = lax.broadcasted_iota(jnp.int32, (c_rows, c_rows), 0)
    col = lax.broadcasted_iota(jnp.int32, (c_rows, c_rows), 1)
    tris = ((col <= row).astype(_BF), (col >= row).astype(_BF))
    dirs = ((qf_ref, vf_ref, zf_ref, of_ref), (qb_ref, vb_ref, zb_ref, ob_ref))

    def chunk(c, carry):
        starts = (pl.multiple_of(c * c_rows, c_rows),
                  pl.multiple_of((n_chunks - 1 - c) * c_rows, c_rows))
        for h0 in range(0, HG_HEADS, HG_HEAD_GROUP):
            chains = []
            for h in range(h0, h0 + HG_HEAD_GROUP):
                cs = slice(h * HG_DK, (h + 1) * HG_DK)
                for d, (q_ref, v_ref, z_ref, o_ref) in enumerate(dirs):
                    rs = pl.ds(starts[d], c_rows)
                    gen = _hgrn_chunk_stages(z_ref[0, rs, cs], q_ref[0, rs, cs], v_ref[0, rs, cs],
                                             lb_ref[d, :, cs], st_ref[d, h], tris[d], rev=bool(d))
                    chains.append((gen, o_ref, rs, cs, d, h))
            for _ in range(2):
                for chain in chains:
                    next(chain[0])
            for gen, o_ref, rs, cs, d, h in chains:
                o, st = next(gen)
                o_ref[0, rs, cs] = o
                st_ref[d, h] = st
        return carry

    lax.fori_loop(0, n_chunks, chunk, 0)


def _hgrn_scan(p3, lb, rows):
    b, s, _ = p3.shape
    w = MIX_W
    t = s // rows
    fwd = lambda c: pl.BlockSpec((1, rows, w), lambda bi, i: (bi, i, c))
    bwd = lambda c: pl.BlockSpec((1, rows, w), lambda bi, i: (bi, t - 1 - i, c))
    out = jax.ShapeDtypeStruct((b, s, w), _F32)
    return pl.pallas_call(
        functools.partial(_hgrn_kernel, rows=rows),
        grid=(b, t),
        in_specs=[fwd(0), fwd(1), fwd(3), bwd(0), bwd(1), bwd(4),
                  pl.BlockSpec((2, 1, w), lambda bi, i: (0, 0, 0))],
        out_specs=[fwd(0), bwd(0)],
        out_shape=[out, out],
        scratch_shapes=[pltpu.VMEM((2, HG_HEADS, HG_DK, HG_DK), _F32)],
        compiler_params=_params("parallel", "arbitrary"),
        name="hgrn_scan",
    )(p3, p3, p3, p3, p3, p3, lb)


def _hgrn_finish_kernel(of_ref, ob_ref, g_ref, w_ref, o_ref):
    o = of_ref[...] + ob_ref[...]
    gate = _silu(g_ref[...])
    w = w_ref[...]
    for h in range(HG_HEADS):
        cs = slice(h * HG_DK, (h + 1) * HG_DK)
        oh = o[:, cs]
        ms = jnp.mean(oh * oh, axis=-1, keepdims=True)
        o_ref[:, cs] = (oh * lax.rsqrt(ms + LN_EPS) * w[:, cs] * gate[:, cs]).astype(o_ref.dtype)


def _hgrn_finish(o_fw, o_bw, p2, g_col, norm_w, tm):
    n = o_fw.shape[0]
    return pl.pallas_call(
        _hgrn_finish_kernel,
        grid=(n // tm,),
        in_specs=[pl.BlockSpec((tm, MIX_W), lambda i: (i, 0)),
                  pl.BlockSpec((tm, MIX_W), lambda i: (i, 0)),
                  pl.BlockSpec((tm, MIX_W), lambda i: (i, g_col)),
                  pl.BlockSpec((1, MIX_W), lambda i: (0, 0))],
        out_specs=pl.BlockSpec((tm, MIX_W), lambda i: (i, 0)),
        out_shape=jax.ShapeDtypeStruct((n, MIX_W), _BF),
        compiler_params=_params("parallel"),
        name="hgrn_finish",
    )(o_fw, o_bw, p2, norm_w)


def _sconv_kernel(gb_ref, gc_ref, h_ref, gcp_ref, hp_ref, gcn_ref, hn_ref, w_ref, o_ref, *, halo):
    i = pl.program_id(1)
    u = gc_ref[0].astype(_F32) * h_ref[0].astype(_F32)
    n = u.shape[0]
    u_prev = (gcp_ref[0].astype(_F32) * hp_ref[0].astype(_F32))[halo - 1:halo]
    u_next = (gcn_ref[0].astype(_F32) * hn_ref[0].astype(_F32))[0:1]
    u_prev = jnp.where(i == 0, 0.0, u_prev)
    u_next = jnp.where(i == pl.num_programs(1) - 1, 0.0, u_next)
    row = lax.broadcasted_iota(jnp.int32, u.shape, 0)
    down = jnp.where(row == 0, u_prev, pltpu.roll(u, 1, 0))
    up = jnp.where(row == n - 1, u_next, pltpu.roll(u, n - 1, 0))
    w = w_ref[...]
    y = down * w[0:1] + u * w[1:2] + up * w[2:3]
    o_ref[0] = (gb_ref[0].astype(_F32) * y).astype(o_ref.dtype)


def _short_conv(p3, conv_w, ts):
    b, s, _ = p3.shape
    halo = SUBLANES_BF16
    hb = ts // halo
    last = s // halo - 1
    main = lambda c: pl.BlockSpec((1, ts, MIX_W), lambda bi, i: (bi, i, c))
    prev = lambda c: pl.BlockSpec((1, halo, MIX_W), lambda bi, i: (bi, jnp.maximum(i * hb - 1, 0), c))
    nxt = lambda c: pl.BlockSpec((1, halo, MIX_W), lambda bi, i: (bi, jnp.minimum((i + 1) * hb, last), c))
    return pl.pallas_call(
        functools.partial(_sconv_kernel, halo=halo),
        grid=(b, s // ts),
        in_specs=[main(0), main(1), main(2), prev(1), prev(2), nxt(1), nxt(2),
                  pl.BlockSpec((3, MIX_W), lambda bi, i: (0, 0))],
        out_specs=pl.BlockSpec((1, ts, MIX_W), lambda bi, i: (bi, i, 0)),
        out_shape=jax.ShapeDtypeStruct((b, s, MIX_W), _BF),
        compiler_params=_params("parallel", "parallel"),
        name="short_conv",
    )(p3, p3, p3, p3, p3, p3, p3, conv_w)


def _tile(n, pref):
    return pref if n % pref == 0 else n


def kernel(x, mem, rel_bias, attn_w_in, attn_lambda, attn_subln, hgrn_w_in, hgrn_lower_bound,
           hgrn_norm, conv_w_in, conv_w, mem_w_kv, w_o, ln_gain, ln_bias, ffn_w_up, ffn_conv,
           ffn_w_down):
    b, s, d = x.shape
    n = b * s
    m = mem.shape[1]
    tm = _tile(s, ROW_TILE)
    hg_rows = _tile(s, HG_ROWS)

    bias, far = _bias_tiles(rel_bias, ATTN_Q_TILE, ATTN_K_TILE)
    mem2 = mem.reshape(b * m, d)
    lbw = jax.nn.softmax(hgrn_lower_bound.astype(_F32), axis=0)
    lb_all = jnp.cumsum(lbw, axis=0) - lbw[0]

    for layer in range(DEPTH):
        kind, j = layer % N_MIXERS, layer // N_MIXERS
        x2 = x.reshape(n, d)
        kv = _matmul(mem2, mem_w_kv[layer].astype(_BF), _BF, _tile(b * m, ROW_TILE))
        kv = kv.reshape(b, m, 2 * XATTN_W)
        if kind == 0:
            q_scale = jnp.where(jnp.arange(attn_w_in.shape[2]) < MIX_W, DA_HD ** -0.5 * LOG2E, 1.0)
            w_in = (attn_w_in[j] * q_scale).astype(_BF)
            p = _matmul(x2, w_in, _BF, tm).reshape(b, s, -1)
            lp = attn_lambda[j].astype(_F32)
            lam_init = 0.8 - 0.6 * math.exp(-0.3 * layer)
            lam = jnp.exp(jnp.sum(lp[0] * lp[1])) - jnp.exp(jnp.sum(lp[2] * lp[3])) + lam_init
            vt = jnp.swapaxes(p[..., 2 * MIX_W:3 * MIX_W], 1, 2).reshape(b, DA_HEADS, DA_W, s)
            vt = jnp.concatenate([vt, jnp.ones((b, DA_HEADS, SUBLANES_BF16, s), _BF)], axis=2)
            mixed = _diff_attention(p, vt, lam.reshape(1),
                                    attn_subln[j].astype(_F32).reshape(1, DA_W),
                                    bias, far, layer, ATTN_Q_TILE, ATTN_K_TILE)
            q_col = 3 * MIX_W // XATTN_W
        elif kind == 1:
            w_in = hgrn_w_in[j].astype(_BF)
            p = _matmul(x2, w_in, _F32, tm).reshape(b, s, -1)
            o_fw, o_bw = _hgrn_scan(p, lb_all[layer].reshape(2, 1, MIX_W), hg_rows)
            mixed = _hgrn_finish(o_fw.reshape(n, MIX_W), o_bw.reshape(n, MIX_W),
                                 p.reshape(n, -1), 2,
                                 hgrn_norm[j].astype(_F32).reshape(1, MIX_W), tm)
            q_col = 5 * MIX_W // XATTN_W
        else:
            w_in = conv_w_in[j].astype(_BF)
            p = _matmul(x2, w_in, _BF, tm).reshape(b, s, -1)
            mixed = _short_conv(p, conv_w[j].astype(_F32), tm)
            q_col = 3 * MIX_W // XATTN_W
        recalled = _memory_attention(p, kv, q_col, tm)
        x2 = _out_ln(x2, mixed.reshape(n, MIX_W), recalled.reshape(n, XATTN_W),
                     w_o[layer].astype(_BF), ln_gain[layer, 0].reshape(1, d),
                     ln_bias[layer, 0].reshape(1, d), tm)
        x = _conv_ffn_ln(x2.reshape(b, s, d), ffn_w_up[layer].astype(_BF),
                         ffn_conv[layer].astype(_F32), ffn_w_down[layer].astype(_BF),
                         ln_gain[layer, 1].reshape(1, d), ln_bias[layer, 1].reshape(1, d),
                         tm, FFN_TILE)
    return x
```

```python
import functools
import math

import jax
import jax.numpy as jnp
from jax import lax
from jax.experimental import pallas as pl
from jax.experimental.pallas import tpu as pltpu

D_MODEL = 1024
DEPTH = 4
N_MIXERS = 3
MIX_W = 3 * D_MODEL // 4
XATTN_HEADS = 4
XATTN_W = D_MODEL - MIX_W
XATTN_HD = XATTN_W // XATTN_HEADS
DA_HD = 64
DA_HEADS = MIX_W // (2 * DA_HD)
DA_W = 2 * DA_HD
N_BUCKETS = 32
MAX_DISTANCE = 128
HG_DK = 128
HG_HEADS = MIX_W // HG_DK
HG_CHUNK = 64
HG_SUB = 16
D_FF = ((8 * D_MODEL // 3 + 255) // 256) * 256
LN_EPS = 1e-5
ALPHA = (2 * DEPTH) ** 0.25
LOG2E = math.log2(math.e)

LANES = 128
SUBLANES_F32 = 8
SUBLANES_BF16 = 16
VMEM_LIMIT = 52 * 1024 * 1024

ROW_TILE = 512
ATTN_Q_TILE = 256
ATTN_K_TILE = 1024
ATTN_HEADS_PER_STEP = 1
ATTN_ROW_CHUNK = 256
MXU_WIDTH = 256
FFN_TILE = MXU_WIDTH
HG_ROWS = 512
HG_HEAD_GROUP = 3

_NEG = -1e30
_BF = jnp.bfloat16
_F32 = jnp.float32
_NT = (((1,), (1,)), ((), ()))


def _params(*sem):
    return pltpu.CompilerParams(dimension_semantics=sem, vmem_limit_bytes=VMEM_LIMIT)


def _sigmoid(z):
    e = jnp.exp(-jnp.abs(z))
    r = 1.0 / (1.0 + e)
    return jnp.where(z >= 0, r, e * r)


def _silu(z):
    return z * _sigmoid(z)


def _layer_norm(r, g, b):
    mu = jnp.mean(r, axis=-1, keepdims=True)
    rc = r - mu
    var = jnp.mean(rc * rc, axis=-1, keepdims=True)
    return rc * lax.rsqrt(var + LN_EPS) * g + b


def _mm_kernel(x_ref, w_ref, o_ref):
    o_ref[...] = jnp.dot(x_ref[...].astype(_BF), w_ref[...],
                         preferred_element_type=_F32).astype(o_ref.dtype)


def _matmul(x, w, out_dtype, tm):
    m, k = x.shape
    n = w.shape[1]
    return pl.pallas_call(
        _mm_kernel,
        grid=(m // tm,),
        in_specs=[pl.BlockSpec((tm, k), lambda i: (i, 0)),
                  pl.BlockSpec((k, n), lambda i: (0, 0), pipeline_mode=pl.Buffered(1))],
        out_specs=pl.BlockSpec((tm, n), lambda i: (i, 0)),
        out_shape=jax.ShapeDtypeStruct((m, n), out_dtype),
        compiler_params=_params("parallel"),
        name="proj",
    )(x, w)


def _memattn_kernel(q_ref, km_ref, vm_ref, o_ref):
    q = q_ref[0].astype(_BF)
    km = km_ref[0]
    vm = vm_ref[0]
    lane = lax.broadcasted_iota(jnp.int32, (1, XATTN_W), 1)
    acc = jnp.zeros(q.shape, _F32)
    for h in range(XATTN_HEADS):
        head = (lane >= h * XATTN_HD) & (lane < (h + 1) * XATTN_HD)
        qh = jnp.where(head, q, jnp.zeros_like(q))
        s = lax.dot_general(qh, km, _NT, preferred_element_type=_F32) * (XATTN_HD ** -0.5)
        p = jnp.exp(s - jnp.max(s, axis=-1, keepdims=True))
        l = jnp.sum(p, axis=-1, keepdims=True)
        vh = jnp.where(head, vm, jnp.zeros_like(vm))
        acc = acc + jnp.dot(p.astype(_BF), vh, preferred_element_type=_F32) / l
    o_ref[0] = acc.astype(o_ref.dtype)


def _memory_attention(p3, kv, q_col, tm):
    b, s, _ = p3.shape
    m = kv.shape[1]
    return pl.pallas_call(
        _memattn_kernel,
        grid=(b, s // tm),
        in_specs=[pl.BlockSpec((1, tm, XATTN_W), lambda bi, i: (bi, i, q_col)),
                  pl.BlockSpec((1, m, XATTN_W), lambda bi, i: (bi, 0, 0)),
                  pl.BlockSpec((1, m, XATTN_W), lambda bi, i: (bi, 0, 1))],
        out_specs=pl.BlockSpec((1, tm, XATTN_W), lambda bi, i: (bi, i, 0)),
        out_shape=jax.ShapeDtypeStruct((b, s, XATTN_W), _BF),
        compiler_params=_params("parallel", "parallel"),
        name="memattn",
    )(p3, kv, kv)


def _out_ln_kernel(x_ref, mix_ref, rec_ref, wo_ref, g_ref, b_ref, o_ref):
    y = jnp.dot(mix_ref[...], wo_ref[:MIX_W, :], preferred_element_type=_F32)
    y = y + jnp.dot(rec_ref[...], wo_ref[MIX_W:, :], preferred_element_type=_F32)
    o_ref[...] = _layer_norm(ALPHA * x_ref[...] + y, g_ref[...], b_ref[...])


def _out_ln(x2, mixed, recalled, wo, g, b, tm):
    n = x2.shape[0]
    return pl.pallas_call(
        _out_ln_kernel,
        grid=(n // tm,),
        in_specs=[pl.BlockSpec((tm, D_MODEL), lambda i: (i, 0)),
                  pl.BlockSpec((tm, MIX_W), lambda i: (i, 0)),
                  pl.BlockSpec((tm, XATTN_W), lambda i: (i, 0)),
                  pl.BlockSpec((D_MODEL, D_MODEL), lambda i: (0, 0)),
                  pl.BlockSpec((1, D_MODEL), lambda i: (0, 0)),
                  pl.BlockSpec((1, D_MODEL), lambda i: (0, 0))],
        out_specs=pl.BlockSpec((tm, D_MODEL), lambda i: (i, 0)),
        out_shape=jax.ShapeDtypeStruct((n, D_MODEL), _F32),
        compiler_params=_params("parallel"),
        name="out_ln",
    )(x2, mixed, recalled, wo, g, b)


def _dwconv_rows(h, w):
    n = h.shape[0]
    return (pltpu.roll(h, 1, 0) * w[0:1] + h * w[1:2] + pltpu.roll(h, n - 1, 0) * w[2:3])


def _ffn_kernel(x_ref, xp_ref, xn_ref, wu_ref, wc_ref, wd_ref, g_ref, b_ref, o_ref, gated_ref,
                *, ts, halo, fb):
    i = pl.program_id(1)
    prev = jnp.where(i == 0, 0.0, xp_ref[0])
    nxt = jnp.where(i == pl.num_programs(1) - 1, 0.0, xn_ref[0])
    xb = jnp.concatenate([prev, x_ref[0], nxt], axis=0).astype(_BF)
    for c in range(D_FF // fb):
        ca = slice(c * fb, (c + 1) * fb)
        cv = slice(D_FF + c * fb, D_FF + (c + 1) * fb)
        ha = jnp.dot(xb, wu_ref[:, ca], preferred_element_type=_F32)
        hv = jnp.dot(xb, wu_ref[:, cv], preferred_element_type=_F32)
        a = _dwconv_rows(ha, wc_ref[:, ca])[halo:halo + ts]
        v = _dwconv_rows(hv, wc_ref[:, cv])[halo:halo + ts]
        gated_ref[:, ca] = (_silu(a) * v).astype(_BF)
    f = jnp.dot(gated_ref[...], wd_ref[...], preferred_element_type=_F32)
    o_ref[0] = _layer_norm(ALPHA * x_ref[0] + f, g_ref[...], b_ref[...])


def _conv_ffn_ln(x3, w_up, w_conv, w_down, g, b, ts, fb):
    bsz, s, d = x3.shape
    halo = SUBLANES_BF16
    hb = ts // halo
    last = s // halo - 1
    kern = functools.partial(_ffn_kernel, ts=ts, halo=halo, fb=fb)
    resident = lambda shape: pl.BlockSpec(shape, lambda bi, i: (0, 0),
                                          pipeline_mode=pl.Buffered(1))
    return pl.pallas_call(
        kern,
        grid=(bsz, s // ts),
        in_specs=[pl.BlockSpec((1, ts, d), lambda bi, i: (bi, i, 0)),
                  pl.BlockSpec((1, halo, d), lambda bi, i: (bi, jnp.maximum(i * hb - 1, 0), 0)),
                  pl.BlockSpec((1, halo, d), lambda bi, i: (bi, jnp.minimum((i + 1) * hb, last), 0)),
                  resident((d, 2 * D_FF)),
                  resident((3, 2 * D_FF)),
                  resident((D_FF, d)),
                  pl.BlockSpec((1, d), lambda bi, i: (0, 0)),
                  pl.BlockSpec((1, d), lambda bi, i: (0, 0))],
        out_specs=pl.BlockSpec((1, ts, d), lambda bi, i: (bi, i, 0)),
        out_shape=jax.ShapeDtypeStruct((bsz, s, d), _F32),
        scratch_shapes=[pltpu.VMEM((ts, D_FF), _BF)],
        compiler_params=_params("parallel", "parallel"),
        name="conv_ffn",
    )(x3, x3, x3, w_up, w_conv, w_down, g, b)


def _fold_rows(x, op, group=SUBLANES_F32):
    acc = x[:group]
    for g in range(1, x.shape[0] // group):
        acc = op(acc, x[g * group:(g + 1) * group])
    while acc.shape[0] > SUBLANES_F32:
        half = acc.shape[0] // 2
        acc = op(acc[:half], acc[half:])
    return acc


def _attn_kernel(lam_ref, far_ref, q_ref, k_ref, vt_ref, bias_ref, subln_ref, o_ref,
                 s_a, s_b, acc_ref, *, tq, tk, out_scale):
    heads = q_ref.shape[2] // DA_W
    streams = [(hh, mi) for hh in range(heads) for mi in range(2)]
    n_str = len(streams)
    hp = pl.program_id(1)
    nkv = k_ref.shape[1] // tk
    nq = q_ref.shape[1] // tq
    n_chunks = tk // ATTN_ROW_CHUNK
    lane = lax.broadcasted_iota(jnp.int32, (1, heads * DA_W), 1)

    unit, u_min, u_max = _near_offsets(tq, tk)
    consts = [[far_ref[3 * (heads * hp + hh) + c] for c in range(3)] for hh in range(heads)]

    def stream_queries(i):
        q = q_ref[0, pl.ds(pl.multiple_of(i * tq, tq), tq), :]
        zero = jnp.zeros_like(q)
        return [jnp.where((lane >= hh * DA_W + mi * DA_HD) & (lane < hh * DA_W + (mi + 1) * DA_HD),
                          q, zero).T for hh, mi in streams]

    def side_select(i, j, left, mid, right):
        u = (tk * j - tq * i) // unit
        return jnp.where(u < u_min, left, jnp.where(u > u_max, right, mid))

    def max_bound(i, j, hh):
        return side_select(i, j, consts[hh][0], consts[hh][2], consts[hh][1])

    def score_rows(qs, j, r, s_out, st):
        k0 = pl.multiple_of(j * tk + r * ATTN_ROW_CHUNK, ATTN_ROW_CHUNK)
        s = jnp.dot(k_ref[0, pl.ds(k0, ATTN_ROW_CHUNK), :], qs[st], preferred_element_type=_F32)
        s_out[st, r * ATTN_ROW_CHUNK:(r + 1) * ATTN_ROW_CHUNK, :] = s
        return _fold_rows(s, jnp.maximum)

    def add_near_bias(i, j, s_out):
        u = (tk * j - tq * i) // unit

        @pl.when((u >= u_min) & (u <= u_max))
        def _():
            for st, (hh, _) in enumerate(streams):
                s_out[st] = s_out[st] + bias_ref[u - u_min, hh]

    def half_step(i, j, carry, s_cur, s_nxt, nxt):
        ni, nj, nqs = nxt
        m_eff = [carry[st][0] - side_select(i, j, consts[hh][0], 0.0, consts[hh][1])
                 for st, (hh, _) in enumerate(streams)]
        tmax = [None] * n_str
        pv = [None] * n_str
        psum = [None] * n_str
        for r in range(n_chunks):
            rows = slice(r * ATTN_ROW_CHUNK, (r + 1) * ATTN_ROW_CHUNK)
            for st in range(n_str):
                t = score_rows(nqs, nj, r, s_nxt, st)
                tmax[st] = t if r == 0 else jnp.maximum(tmax[st], t)
            v0 = pl.multiple_of(j * tk + r * ATTN_ROW_CHUNK, ATTN_ROW_CHUNK)
            for st, (hh, _) in enumerate(streams):
                p = jnp.exp2(s_cur[st, rows, :] - m_eff[st])
                t = _fold_rows(p, jnp.add)
                psum[st] = t if r == 0 else psum[st] + t
                d = jnp.dot(vt_ref[0, hh, :, pl.ds(v0, ATTN_ROW_CHUNK)], p.astype(_BF),
                            preferred_element_type=_F32)
                pv[st] = d if r == 0 else pv[st] + d
        new, tile_max = [], []
        for st, (hh, _) in enumerate(streams):
            m, alpha, l = carry[st]
            acc_ref[st] = alpha * acc_ref[st] + pv[st]
            l = alpha * l + jnp.sum(psum[st], axis=0, keepdims=True)
            cand = jnp.max(tmax[st], axis=0, keepdims=True) + max_bound(ni, nj, hh)
            m_nxt = jnp.maximum(m, cand)
            new.append((m_nxt, jnp.exp2(m - m_nxt), l))
            tile_max.append(cand)
        add_near_bias(ni, nj, s_nxt)
        return tuple(new), tuple(tile_max)

    qs0 = stream_queries(0)
    first = []
    for st, (hh, _) in enumerate(streams):
        parts = [score_rows(qs0, 0, r, s_a, st) for r in range(n_chunks)]
        first.append(jnp.max(functools.reduce(jnp.maximum, parts), axis=0, keepdims=True)
                     + max_bound(0, 0, hh))
    add_near_bias(0, 0, s_a)
    lam = lam_ref[0]

    def query_tile(i, m_first):
        qs = stream_queries(i)
        acc_ref[...] = jnp.zeros_like(acc_ref)
        zeros = jnp.zeros((1, tq), _F32)
        carry = tuple((m_first[st], zeros, zeros) for st in range(n_str))

        def pair(jj, carry):
            j = 2 * jj
            carry, _ = half_step(i, j, carry, s_a, s_b, (i, j + 1, qs))
            carry, _ = half_step(i, j + 1, carry, s_b, s_a, (i, j + 2, qs))
            return carry

        carry = lax.fori_loop(0, nkv // 2 - 1, pair, carry)
        carry, _ = half_step(i, nkv - 2, carry, s_a, s_b, (i, nkv - 1, qs))
        i_next = jnp.minimum(i + 1, nq - 1)
        carry, m_next = half_step(i, nkv - 1, carry, s_b, s_a,
                                  (i_next, 0, stream_queries(i_next)))

        q0 = pl.multiple_of(i * tq, tq)
        for hh in range(heads):
            o0 = acc_ref[2 * hh] / carry[2 * hh][2]
            o1 = acc_ref[2 * hh + 1] / carry[2 * hh + 1][2]
            o = (o0 - lam * o1).T
            ms = jnp.mean(o * o, axis=-1, keepdims=True)
            o_ref[0, pl.ds(q0, tq), hh * DA_W:(hh + 1) * DA_W] = (
                o * lax.rsqrt(ms + LN_EPS) * subln_ref[...] * out_scale).astype(o_ref.dtype)
        return m_next

    lax.fori_loop(0, nq, query_tile, tuple(first))


def _t5_bucket(rel):
    half = N_BUCKETS // 2
    max_exact = half // 2
    ret = jnp.where(rel > 0, half, 0)
    n = jnp.abs(rel)
    nf = jnp.maximum(n, 1).astype(_F32)
    large = max_exact + (jnp.log(nf / max_exact) / math.log(MAX_DISTANCE / max_exact)
                         * (half - max_exact)).astype(jnp.int32)
    large = jnp.minimum(large, half - 1)
    return ret + jnp.where(n < max_exact, n, large)


def _near_offsets(tq, tk):
    unit = math.gcd(tq, tk)
    first = -((tk + MAX_DISTANCE - 2) // unit)
    last = (tq + MAX_DISTANCE - 2) // unit
    return unit, first, last


def _bias_tiles(rel_bias, tq, tk):
    unit, u_min, u_max = _near_offsets(tq, tk)
    n_near = u_max - u_min + 1
    c = jnp.arange(tk, dtype=jnp.int32)[:, None]
    r = jnp.arange(tq, dtype=jnp.int32)[None, :]
    u = jnp.arange(u_min, u_max + 1, dtype=jnp.int32)[:, None, None]
    table = rel_bias.astype(_F32) * LOG2E
    bucket = _t5_bucket(u * unit + c - r)[:, None]
    tiles = jnp.zeros((n_near, DA_HEADS, tk, tq), _F32)
    for b in range(N_BUCKETS):
        tiles = jnp.where(bucket == b, table[b][None, :, None, None], tiles)
    far =table[_t5_bucket(jnp.array([-MAX_DISTANCE, MAX_DISTANCE], jnp.int32))]
    consts = jnp.concatenate([far, jnp.max(table, axis=0, keepdims=True)], axis=0)
    return tiles, consts.T.reshape(-1)


def _diff_attention(p3, vt, lam, subln, bias, far, layer, tq, tk):
    b, s, _ = p3.shape
    vrows = vt.shape[2]
    assert s % (2 * tk) == 0
    lam_init = 0.8 - 0.6 * math.exp(-0.3 * layer)
    kern = functools.partial(_attn_kernel, tq=tq, tk=tk, out_scale=1.0 - lam_init)
    nh = ATTN_HEADS_PER_STEP
    groups = DA_HEADS // nh
    return pl.pallas_call(
        kern,
        grid=(b, groups),
        in_specs=[pl.BlockSpec(memory_space=pltpu.SMEM),
                  pl.BlockSpec(memory_space=pltpu.SMEM),
                  pl.BlockSpec((1, s, nh * DA_W), lambda bi, h: (bi, 0, h)),
                  pl.BlockSpec((1, s, nh * DA_W), lambda bi, h: (bi, 0, groups + h)),
                  pl.BlockSpec((1, nh, vrows, s), lambda bi, h: (bi, h, 0, 0)),
                  pl.BlockSpec((bias.shape[0], nh, tk, tq), lambda bi, h: (0, h, 0, 0)),
                  pl.BlockSpec((1, DA_W), lambda bi, h: (0, 0))],
        out_specs=pl.BlockSpec((1, s, nh * DA_W), lambda bi, h: (bi, 0, h)),
        out_shape=jax.ShapeDtypeStruct((b, s, MIX_W), _BF),
        scratch_shapes=[pltpu.VMEM((2 * nh, tk, tq), _F32), pltpu.VMEM((2 * nh, tk, tq), _F32),
                        pltpu.VMEM((2 * nh, vrows, tq), _F32)],
        compiler_params=_params("parallel", "parallel"),
        name="diff_attn",
    )(lam, far, p3, p3, vt, bias, subln)


def _hgrn_chunk_stages(z, q, v, lb, st, tri, rev):
    c_rows = z.shape[0]
    f = lb + (1.0 - lb) * _sigmoid(z)
    kk = (1.0 - lb) * _sigmoid(-z)
    logf = jnp.log(f)
    hi = logf.astype(_BF)
    lo = (logf - hi.astype(_F32)).astype(_BF)
    cum = (jnp.dot(tri, hi, preferred_element_type=_F32)
           + jnp.dot(tri, lo, preferred_element_type=_F32))
    qh = _silu(q) * (HG_DK ** -0.5)
    vb = v.astype(_BF)
    yield None
    total = cum[0:1, :] if rev else cum[c_rows - 1:c_rows, :]
    o_inter = lax.dot_general((qh * jnp.exp(cum)).astype(_BF), st.astype(_BF), _NT,
                              preferred_element_type=_F32)
    ks = (kk * jnp.exp(total - cum)).astype(_BF)
    st_new = st * jnp.exp(total) + jnp.dot(v.T.astype(_BF), ks, preferred_element_type=_F32)
    blocks = []
    for b in range(c_rows // HG_SUB):
        lo_r, hi_r = b * HG_SUB, (b + 1) * HG_SUB
        if rev:
            cols = slice(lo_r, c_rows)
            base = cum[hi_r:hi_r + 1, :] if hi_r < c_rows else jnp.zeros_like(total)
        else:
            cols = slice(0, hi_r)
            base = cum[lo_r - 1:lo_r, :] if b else jnp.zeros_like(total)
        qq = (qh[lo_r:hi_r] * jnp.exp(cum[lo_r:hi_r] - base)).astype(_BF)
        kt = (kk[cols] * jnp.exp(base - cum[cols])).astype(_BF)
        a = lax.dot_general(qq, kt, _NT, preferred_element_type=_F32)
        blocks.append((lo_r, hi_r, cols, a))
    yield None
    parts = []
    for lo_r, hi_r, cols, a in blocks:
        sub_row = lax.broadcasted_iota(jnp.int32, a.shape, 0) + lo_r
        sub_col = lax.broadcasted_iota(jnp.int32, a.shape, 1) + cols.start
        seen = (sub_col >= sub_row) if rev else (sub_col <= sub_row)
        a = jnp.where(seen, a, 0.0)
        parts.append(o_inter[lo_r:hi_r]
                     + jnp.dot(a.astype(_BF), vb[cols], preferred_element_type=_F32))
    yield jnp.concatenate(parts, axis=0), st_new


def _hgrn_kernel(qf_ref, vf_ref, zf_ref, qb_ref, vb_ref, zb_ref, lb_ref, of_ref, ob_ref, st_ref,
                 *, rows):
    @pl.when(pl.program_id(1) == 0)
    def _():
        st_ref[...] = jnp.zeros_like(st_ref)

    c_rows = HG_CHUNK
    n_chunks = rows // c_rows
    row = lax.broadcasted_iota(jnp.int32, (c_rows, c_rows), 0)
    col = lax.broadcasted_iota(jnp.int32, (c_rows, c_rows), 1)
    tris = ((col <= row).astype(_BF), (col >= row).astype(_BF))
    dirs = ((qf_ref, vf_ref, zf_ref, of_ref), (qb_ref, vb_ref, zb_ref, ob_ref))

    def chunk(c, carry):
        starts = (pl.multiple_of(c * c_rows, c_rows),
                  pl.multiple_of((n_chunks - 1 - c) * c_rows, c_rows))
        for h0 in range(0, HG_HEADS, HG_HEAD_GROUP):
            chains = []
            for h in range(h0, h0 + HG_HEAD_GROUP):
                cs = slice(h * HG_DK, (h + 1) * HG_DK)
                for d, (q_ref, v_ref, z_ref, o_ref) in enumerate(dirs):
                    rs = pl.ds(starts[d], c_rows)
                    gen = _hgrn_chunk_stages(z_ref[0, rs, cs], q_ref[0, rs, cs], v_ref[0, rs, cs],
                                             lb_ref[d, :, cs], st_ref[d, h], tris[d], rev=bool(d))
                    chains.append((gen, o_ref, rs, cs, d, h))
            for _ in range(2):
                for chain in chains:
                    next(chain[0])
            for gen, o_ref, rs, cs, d, h in chains:
                o, st = next(gen)
                o_ref[0, rs, cs] = o
                st_ref[d, h] = st
        return carry

    lax.fori_loop(0, n_chunks, chunk, 0)


def _hgrn_scan(p3, lb, rows):
    b, s, _ = p3.shape
    w = MIX_W
    t = s // rows
    fwd = lambda c: pl.BlockSpec((1, rows, w), lambda bi, i: (bi, i, c))
    bwd = lambda c: pl.BlockSpec((1, rows, w), lambda bi, i: (bi, t - 1 - i, c))
    out = jax.ShapeDtypeStruct((b, s, w), _F32)
    return pl.pallas_call(
        functools.partial(_hgrn_kernel, rows=rows),
        grid=(b, t),
        in_specs=[fwd(0), fwd(1), fwd(3), bwd(0), bwd(1), bwd(4),
                  pl.BlockSpec((2, 1, w), lambda bi, i: (0, 0, 0))],
        out_specs=[fwd(0), bwd(0)],
        out_shape=[out, out],
        scratch_shapes=[pltpu.VMEM((2, HG_HEADS, HG_DK, HG_DK), _F32)],
        compiler_params=_params("parallel", "arbitrary"),
        name="hgrn_scan",
    )(p3, p3, p3, p3, p3, p3, lb)


def _hgrn_finish_kernel(of_ref, ob_ref, g_ref, w_ref, o_ref):
    o = of_ref[...] + ob_ref[...]
    gate = _silu(g_ref[...])
    w = w_ref[...]
    for h in range(HG_HEADS):
        cs = slice(h * HG_DK, (h + 1) * HG_DK)
        oh = o[:, cs]
        ms = jnp.mean(oh * oh, axis=-1, keepdims=True)
        o_ref[:, cs] = (oh * lax.rsqrt(ms + LN_EPS) * w[:, cs] * gate[:, cs]).astype(o_ref.dtype)


def _hgrn_finish(o_fw, o_bw, p2, g_col, norm_w, tm):
    n = o_fw.shape[0]
    return pl.pallas_call(
        _hgrn_finish_kernel,
        grid=(n // tm,),
        in_specs=[pl.BlockSpec((tm, MIX_W), lambda i: (i, 0)),
                  pl.BlockSpec((tm, MIX_W), lambda i: (i, 0)),
                  pl.BlockSpec((tm, MIX_W), lambda i: (i, g_col)),
                  pl.BlockSpec((1, MIX_W), lambda i: (0, 0))],
        out_specs=pl.BlockSpec((tm, MIX_W), lambda i: (i, 0)),
        out_shape=jax.ShapeDtypeStruct((n, MIX_W), _BF),
        compiler_params=_params("parallel"),
        name="hgrn_finish",
    )(o_fw, o_bw, p2, norm_w)


def _sconv_kernel(gb_ref, gc_ref, h_ref, gcp_ref, hp_ref, gcn_ref, hn_ref, w_ref, o_ref, *, halo):
    i = pl.program_id(1)
    u = gc_ref[0].astype(_F32) * h_ref[0].astype(_F32)
    n = u.shape[0]
    u_prev = (gcp_ref[0].astype(_F32) * hp_ref[0].astype(_F32))[halo - 1:halo]
    u_next = (gcn_ref[0].astype(_F32) * hn_ref[0].astype(_F32))[0:1]
    u_prev = jnp.where(i == 0, 0.0, u_prev)
    u_next = jnp.where(i == pl.num_programs(1) - 1, 0.0, u_next)
    row = lax.broadcasted_iota(jnp.int32, u.shape, 0)
    down = jnp.where(row == 0, u_prev, pltpu.roll(u, 1, 0))
    up = jnp.where(row == n - 1, u_next, pltpu.roll(u, n - 1, 0))
    w = w_ref[...]
    y = down * w[0:1] + u * w[1:2] + up * w[2:3]
    o_ref[0] = (gb_ref[0].astype(_F32) * y).astype(o_ref.dtype)


def _short_conv(p3, conv_w, ts):
    b, s, _ = p3.shape
    halo = SUBLANES_BF16
    hb = ts // halo
    last = s // halo - 1
    main = lambda c: pl.BlockSpec((1, ts, MIX_W), lambda bi, i: (bi, i, c))
    prev = lambda c: pl.BlockSpec((1, halo, MIX_W), lambda bi, i: (bi, jnp.maximum(i * hb - 1, 0), c))
    nxt = lambda c: pl.BlockSpec((1, halo, MIX_W), lambda bi, i: (bi, jnp.minimum((i + 1) * hb, last), c))
    return pl.pallas_call(
        functools.partial(_sconv_kernel, halo=halo),
        grid=(b, s // ts),
        in_specs=[main(0), main(1), main(2), prev(1), prev(2), nxt(1), nxt(2),
                  pl.BlockSpec((3, MIX_W), lambda bi, i: (0, 0))],
        out_specs=pl.BlockSpec((1, ts, MIX_W), lambda bi, i: (bi, i, 0)),
        out_shape=jax.ShapeDtypeStruct((b, s, MIX_W), _BF),
        compiler_params=_params("parallel", "parallel"),
        name="short_conv",
    )(p3, p3, p3, p3, p3, p3, p3, conv_w)


def _tile(n, pref):
    return pref if n % pref == 0 else n


def kernel(x, mem, rel_bias, attn_w_in, attn_lambda, attn_subln, hgrn_w_in, hgrn_lower_bound,
           hgrn_norm, conv_w_in, conv_w, mem_w_kv, w_o, ln_gain, ln_bias, ffn_w_up, ffn_conv,
           ffn_w_down):
    b, s, d = x.shape
    n = b * s
    m = mem.shape[1]
    tm = _tile(s, ROW_TILE)
    hg_rows = _tile(s, HG_ROWS)

    bias, far = _bias_tiles(rel_bias, ATTN_Q_TILE, ATTN_K_TILE)
    mem2 = mem.reshape(b * m, d)
    lbw = jax.nn.softmax(hgrn_lower_bound.astype(_F32), axis=0)
    lb_all = jnp.cumsum(lbw, axis=0) - lbw[0]

    for layer in range(DEPTH):
        kind, j = layer % N_MIXERS, layer // N_MIXERS
        x2 = x.reshape(n, d)
        kv = _matmul(mem2, mem_w_kv[layer].astype(_BF), _BF, _tile(b * m, ROW_TILE))
        kv = kv.reshape(b, m, 2 * XATTN_W)
        if kind == 0:
            q_scale = jnp.where(jnp.arange(attn_w_in.shape[2]) < MIX_W, DA_HD ** -0.5 * LOG2E, 1.0)
            w_in = (attn_w_in[j] * q_scale).astype(_BF)
            p = _matmul(x2, w_in, _BF, tm).reshape(b, s, -1)
            lp = attn_lambda[j].astype(_F32)
            lam_init = 0.8 - 0.6 * math.exp(-0.3 * layer)
            lam = jnp.exp(jnp.sum(lp[0] * lp[1])) - jnp.exp(jnp.sum(lp[2] * lp[3])) + lam_init
            vt = jnp.swapaxes(p[..., 2 * MIX_W:3 * MIX_W], 1, 2).reshape(b, DA_HEADS, DA_W, s)
            mixed = _diff_attention(p, vt, lam.reshape(1),
                                    attn_subln[j].astype(_F32).reshape(1, DA_W),
                                    bias, far, layer, ATTN_Q_TILE, ATTN_K_TILE)
            q_col = 3 * MIX_W // XATTN_W
        elif kind == 1:
            w_in = hgrn_w_in[j].astype(_BF)
            p = _matmul(x2, w_in, _F32, tm).reshape(b, s, -1)
            o_fw, o_bw = _hgrn_scan(p, lb_all[layer].reshape(2, 1, MIX_W), hg_rows)
            mixed = _hgrn_finish(o_fw.reshape(n, MIX_W), o_bw.reshape(n, MIX_W),
                                 p.reshape(n, -1), 2,
                                 hgrn_norm[j].astype(_F32).reshape(1, MIX_W), tm)
            q_col = 5 * MIX_W // XATTN_W
        else:
            w_in = conv_w_in[j].astype(_BF)
            p = _matmul(x2, w_in, _BF, tm).reshape(b, s, -1)
            mixed = _short_conv(p, conv_w[j].astype(_F32), tm)
            q_col = 3 * MIX_W // XATTN_W
        recalled = _memory_attention(p, kv, q_col, tm)
        x2 = _out_ln(x2, mixed.reshape(n, MIX_W), recalled.reshape(n, XATTN_W),
                     w_o[layer].astype(_BF), ln_gain[layer, 0].reshape(1, d),
                     ln_bias[layer, 0].reshape(1, d), tm)
        x = _conv_ffn_ln(x2.reshape(b, s, d), ffn_w_up[layer].astype(_BF),
                         ffn_conv[layer].astype(_F32), ffn_w_down[layer].astype(_BF),
                         ln_gain[layer, 1].reshape(1, d), ln_bias[layer, 1].reshape(1, d),
                         tm, FFN_TILE)
    return x
```

```python
import functools
import math

import jax
import jax.numpy as jnp
from jax import lax
from jax.experimental import pallas as pl
from jax.experimental.pallas import tpu as pltpu

D_MODEL = 1024
DEPTH = 4
N_MIXERS = 3
MIX_W = 3 * D_MODEL // 4
XATTN_HEADS = 4
XATTN_W = D_MODEL - MIX_W
XATTN_HD = XATTN_W // XATTN_HEADS
DA_HD = 64
DA_HEADS = MIX_W // (2 * DA_HD)
DA_W = 2 * DA_HD
N_BUCKETS = 32
MAX_DISTANCE = 128
HG_DK = 128
HG_HEADS = MIX_W // HG_DK
HG_CHUNK = 64
HG_SUB = 16
D_FF = ((8 * D_MODEL // 3 + 255) // 256) * 256
LN_EPS = 1e-5
ALPHA = (2 * DEPTH) ** 0.25
LOG2E = math.log2(math.e)

LANES = 128
SUBLANES_F32 = 8
SUBLANES_BF16 = 16
VMEM_LIMIT = 52 * 1024 * 1024

ROW_TILE = 512
ATTN_Q_TILE = 256
ATTN_K_TILE = 1024
ATTN_HEADS_PER_STEP = 1
ATTN_ROW_CHUNK = 256
MXU_WIDTH = 256
FFN_TILE = MXU_WIDTH
FFN_OUT_ROWS = 256
HG_ROWS = 512
HG_HEAD_GROUP = 6

_NEG = -1e30
_BF = jnp.bfloat16
_F32 = jnp.float32
_NT = (((1,), (1,)), ((), ()))


def _params(*sem):
    return pltpu.CompilerParams(dimension_semantics=sem, vmem_limit_bytes=VMEM_LIMIT)


def _sigmoid(z):
    e = jnp.exp(-jnp.abs(z))
    r = 1.0 / (1.0 + e)
    return jnp.where(z >= 0, r, e * r)


def _silu(z):
    return z * _sigmoid(z)


def _layer_norm(r, g, b):
    mu = jnp.mean(r, axis=-1, keepdims=True)
    rc = r - mu
    var = jnp.mean(rc * rc, axis=-1, keepdims=True)
    return rc * lax.rsqrt(var + LN_EPS) * g + b


def _mm_kernel(x_ref, w_ref, o_ref):
    o_ref[...] = jnp.dot(x_ref[...].astype(_BF), w_ref[...],
                         preferred_element_type=_F32).astype(o_ref.dtype)


def _matmul(x, w, out_dtype, tm):
    m, k = x.shape
    n = w.shape[1]
    return pl.pallas_call(
        _mm_kernel,
        grid=(m // tm,),
        in_specs=[pl.BlockSpec((tm, k), lambda i: (i, 0)),
                  pl.BlockSpec((k, n), lambda i: (0, 0), pipeline_mode=pl.Buffered(1))],
        out_specs=pl.BlockSpec((tm, n), lambda i: (i, 0)),
        out_shape=jax.ShapeDtypeStruct((m, n), out_dtype),
        compiler_params=_params("parallel"),
        name="proj",
    )(x, w)


def _memattn_kernel(q_ref, km_ref, vm_ref, o_ref):
    q = q_ref[0].astype(_BF)
    km = km_ref[0]
    vm = vm_ref[0]
    lane = lax.broadcasted_iota(jnp.int32, (1, XATTN_W), 1)
    acc = jnp.zeros(q.shape, _F32)
    for h in range(XATTN_HEADS):
        head = (lane >= h * XATTN_HD) & (lane < (h + 1) * XATTN_HD)
        qh = jnp.where(head, q, jnp.zeros_like(q))
        s = lax.dot_general(qh, km, _NT, preferred_element_type=_F32) * (XATTN_HD ** -0.5)
        p = jnp.exp(s - jnp.max(s, axis=-1, keepdims=True))
        l = jnp.sum(p, axis=-1, keepdims=True)
        vh = jnp.where(head, vm, jnp.zeros_like(vm))
        acc = acc + jnp.dot(p.astype(_BF), vh, preferred_element_type=_F32) / l
    o_ref[0] = acc.astype(o_ref.dtype)


def _memory_attention(p3, kv, q_col, tm):
    b, s, _ = p3.shape
    m = kv.shape[1]
    return pl.pallas_call(
        _memattn_kernel,
        grid=(b, s // tm),
        in_specs=[pl.BlockSpec((1, tm, XATTN_W), lambda bi, i: (bi, i, q_col)),
                  pl.BlockSpec((1, m, XATTN_W), lambda bi, i: (bi, 0, 0)),
                  pl.BlockSpec((1, m, XATTN_W), lambda bi, i: (bi, 0, 1))],
        out_specs=pl.BlockSpec((1, tm, XATTN_W), lambda bi, i: (bi, i, 0)),
        out_shape=jax.ShapeDtypeStruct((b, s, XATTN_W), _BF),
        compiler_params=_params("parallel", "parallel"),
        name="memattn",
    )(p3, kv, kv)


def _out_ln_kernel(x_ref, mix_ref, rec_ref, wo_ref, g_ref, b_ref, o_ref):
    y = jnp.dot(mix_ref[...], wo_ref[:MIX_W, :], preferred_element_type=_F32)
    y = y + jnp.dot(rec_ref[...], wo_ref[MIX_W:, :], preferred_element_type=_F32)
    o_ref[...] = _layer_norm(ALPHA * x_ref[...] + y, g_ref[...], b_ref[...])


def _out_ln(x2, mixed, recalled, wo, g, b, tm):
    n = x2.shape[0]
    return pl.pallas_call(
        _out_ln_kernel,
        grid=(n // tm,),
        in_specs=[pl.BlockSpec((tm, D_MODEL), lambda i: (i, 0)),
                  pl.BlockSpec((tm, MIX_W), lambda i: (i, 0)),
                  pl.BlockSpec((tm, XATTN_W), lambda i: (i, 0)),
                  pl.BlockSpec((D_MODEL, D_MODEL), lambda i: (0, 0)),
                  pl.BlockSpec((1, D_MODEL), lambda i: (0, 0)),
                  pl.BlockSpec((1, D_MODEL), lambda i: (0, 0))],
        out_specs=pl.BlockSpec((tm, D_MODEL), lambda i: (i, 0)),
        out_shape=jax.ShapeDtypeStruct((n, D_MODEL), _F32),
        compiler_params=_params("parallel"),
        name="out_ln",
    )(x2, mixed, recalled, wo, g, b)


def _dwconv_rows(h, w):
    n = h.shape[0]
    return (pltpu.roll(h, 1, 0) * w[0:1] + h * w[1:2] + pltpu.roll(h, n - 1, 0) * w[2:3])


def _ffn_kernel(x_ref, xp_ref, xn_ref, wu_ref, wc_ref, wd_ref, g_ref, b_ref, o_ref, gated_ref,
                *, ts, halo, fb):
    i = pl.program_id(1)
    prev = jnp.where(i == 0, 0.0, xp_ref[0])
    nxt = jnp.where(i == pl.num_programs(1) - 1, 0.0, xn_ref[0])
    xb = jnp.concatenate([prev, x_ref[0], nxt], axis=0).astype(_BF)
    for c in range(D_FF // fb):
        ca = slice(c * fb, (c + 1) * fb)
        cv = slice(D_FF + c * fb, D_FF + (c + 1) * fb)
        ha = jnp.dot(xb, wu_ref[:, ca], preferred_element_type=_F32)
        hv = jnp.dot(xb, wu_ref[:, cv], preferred_element_type=_F32)
        a = _dwconv_rows(ha, wc_ref[:, ca])[halo:halo + ts]
        v = _dwconv_rows(hv, wc_ref[:, cv])[halo:halo + ts]
        gated_ref[:, ca] = (_silu(a) * v).astype(_BF)
    for r0 in range(0, ts, FFN_OUT_ROWS):
        rows = slice(r0, r0 + FFN_OUT_ROWS)
        f = jnp.dot(gated_ref[rows, :], wd_ref[...], preferred_element_type=_F32)
        o_ref[0, rows, :] = _layer_norm(ALPHA * x_ref[0, rows, :] + f, g_ref[...], b_ref[...])


def _conv_ffn_ln(x3, w_up, w_conv, w_down, g, b, ts, fb):
    bsz, s, d = x3.shape
    halo = SUBLANES_BF16
    hb = ts // halo
    last = s // halo - 1
    kern = functools.partial(_ffn_kernel, ts=ts, halo=halo, fb=fb)
    resident = lambda shape: pl.BlockSpec(shape, lambda bi, i: (0, 0),
                                          pipeline_mode=pl.Buffered(1))
    return pl.pallas_call(
        kern,
        grid=(bsz, s // ts),
        in_specs=[pl.BlockSpec((1, ts, d), lambda bi, i: (bi, i, 0)),
                  pl.BlockSpec((1, halo, d), lambda bi, i: (bi, jnp.maximum(i * hb - 1, 0), 0)),
                  pl.BlockSpec((1, halo, d), lambda bi, i: (bi, jnp.minimum((i + 1) * hb, last), 0)),
                  resident((d, 2 * D_FF)),
                  resident((3, 2 * D_FF)),
                  resident((D_FF, d)),
                  pl.BlockSpec((1, d), lambda bi, i: (0, 0)),
                  pl.BlockSpec((1, d), lambda bi, i: (0, 0))],
        out_specs=pl.BlockSpec((1, ts, d), lambda bi, i: (bi, i, 0)),
        out_shape=jax.ShapeDtypeStruct((bsz, s, d), _F32),
        scratch_shapes=[pltpu.VMEM((ts, D_FF), _BF)],
        compiler_params=_params("parallel", "parallel"),
        name="conv_ffn",
    )(x3, x3, x3, w_up, w_conv, w_down, g, b)


def _fold_rows(x, op, group=SUBLANES_F32):
    acc = x[:group]
    for g in range(1, x.shape[0] // group):
        acc = op(acc, x[g * group:(g + 1) * group])
    while acc.shape[0] > SUBLANES_F32:
        half = acc.shape[0] // 2
        acc = op(acc[:half], acc[half:])
    return acc


def _attn_kernel(lam_ref, far_ref, q_ref, k_ref, vt_ref, bias_ref, subln_ref, o_ref,
                 s_a, s_b, acc_ref, *, tq, tk, out_scale):
    heads = q_ref.shape[2] // DA_W
    streams = [(hh, mi) for hh in range(heads) for mi in range(2)]
    n_str = len(streams)
    hp = pl.program_id(1)
    nkv = k_ref.shape[1] // tk
    nq = q_ref.shape[1] // tq
    n_chunks = tk // ATTN_ROW_CHUNK
    lane = lax.broadcasted_iota(jnp.int32, (1, heads * DA_W), 1)

    unit, u_min, u_max = _near_offsets(tq, tk)
    consts = [[far_ref[3 * (heads * hp + hh) + c] for c in range(3)] for hh in range(heads)]

    def stream_queries(i):
        q = q_ref[0, pl.ds(pl.multiple_of(i * tq, tq), tq), :]
        zero = jnp.zeros_like(q)
        return [jnp.where((lane >= hh * DA_W + mi * DA_HD) & (lane < hh * DA_W + (mi + 1) * DA_HD),
                          q, zero).T for hh, mi in streams]

    def side_select(i, j, left, mid, right):
        u = (tk * j - tq * i) // unit
        return jnp.where(u < u_min, left, jnp.where(u > u_max, right, mid))

    def max_bound(i, j, hh):
        return side_select(i, j, consts[hh][0], consts[hh][2], consts[hh][1])

    def score_rows(qs, j, r, s_out, st):
        k0 = pl.multiple_of(j * tk + r * ATTN_ROW_CHUNK, ATTN_ROW_CHUNK)
        s = jnp.dot(k_ref[0, pl.ds(k0, ATTN_ROW_CHUNK), :], qs[st], preferred_element_type=_F32)
        s_out[st, r * ATTN_ROW_CHUNK:(r + 1) * ATTN_ROW_CHUNK, :] = s
        return _fold_rows(s, jnp.maximum)

    def add_near_bias(i, j, s_out):
        u = (tk * j - tq * i) // unit

        @pl.when((u >= u_min) & (u <= u_max))
        def _():
            for st, (hh, _) in enumerate(streams):
                s_out[st] = s_out[st] + bias_ref[u - u_min, hh]

    def half_step(i, j, carry, s_cur, s_nxt, nxt):
        ni, nj, nqs = nxt
        m_eff = [carry[st][0] - side_select(i, j, consts[hh][0], 0.0, consts[hh][1])
                 for st, (hh, _) in enumerate(streams)]
        tmax = [None] * n_str
        pv = [None] * n_str
        for r in range(n_chunks):
            rows = slice(r * ATTN_ROW_CHUNK, (r + 1) * ATTN_ROW_CHUNK)
            for st in range(n_str):
                t = score_rows(nqs, nj, r, s_nxt, st)
                tmax[st] = t if r == 0 else jnp.maximum(tmax[st], t)
            v0 = pl.multiple_of(j * tk + r * ATTN_ROW_CHUNK, ATTN_ROW_CHUNK)
            for st, (hh, _) in enumerate(streams):
                p = jnp.exp2(s_cur[st, rows, :] - m_eff[st]).astype(_BF)
                d = jnp.dot(vt_ref[0, hh, :, pl.ds(v0, ATTN_ROW_CHUNK)], p,
                            preferred_element_type=_F32)
                pv[st] = d if r == 0 else pv[st] + d
        new, tile_max = [], []
        for st, (hh, _) in enumerate(streams):
            m, alpha = carry[st]
            acc_ref[st] = alpha * acc_ref[st] + pv[st]
            cand = jnp.max(tmax[st], axis=0, keepdims=True) + max_bound(ni, nj, hh)
            m_nxt = jnp.maximum(m, cand)
            new.append((m_nxt, jnp.exp2(m - m_nxt)))
            tile_max.append(cand)
        add_near_bias(ni, nj, s_nxt)
        return tuple(new), tuple(tile_max)

    qs0 = stream_queries(0)
    first = []
    for st, (hh, _) in enumerate(streams):
        parts = [score_rows(qs0, 0, r, s_a, st) for r in range(n_chunks)]
        first.append(jnp.max(functools.reduce(jnp.maximum, parts), axis=0, keepdims=True)
                     + max_bound(0, 0, hh))
    add_near_bias(0, 0, s_a)
    lam = lam_ref[0]

    def query_tile(i, m_first):
        qs = stream_queries(i)
        acc_ref[...] = jnp.zeros_like(acc_ref)
        carry = tuple((m_first[st], jnp.zeros((1, tq), _F32)) for st in range(n_str))

        def pair(jj, carry):
            j = 2 * jj
            carry, _ = half_step(i, j, carry, s_a, s_b, (i, j + 1, qs))
            carry, _ = half_step(i, j + 1, carry, s_b, s_a, (i, j + 2, qs))
            return carry

        carry = lax.fori_loop(0, nkv // 2 - 1, pair, carry)
        carry, _ = half_step(i, nkv - 2, carry, s_a, s_b, (i, nkv - 1, qs))
        i_next = jnp.minimum(i + 1, nq - 1)
        _, m_next = half_step(i, nkv - 1, carry, s_b, s_a, (i_next, 0, stream_queries(i_next)))

        q0 = pl.multiple_of(i * tq, tq)
        for hh in range(heads):
            o0 = acc_ref[2 * hh, :DA_W] / acc_ref[2 * hh, DA_W:DA_W + 1]
            o1 = acc_ref[2 * hh + 1, :DA_W] / acc_ref[2 * hh + 1, DA_W:DA_W + 1]
            o = (o0 - lam * o1).T
            ms = jnp.mean(o * o, axis=-1, keepdims=True)
            o_ref[0, pl.ds(q0, tq), hh * DA_W:(hh + 1) * DA_W] = (
                o * lax.rsqrt(ms + LN_EPS) * subln_ref[...] * out_scale).astype(o_ref.dtype)
        return m_next

    lax.fori_loop(0, nq, query_tile, tuple(first))


def _t5_bucket(rel):
    half = N_BUCKETS // 2
    max_exact = half // 2
    ret = jnp.where(rel > 0, half, 0)
    n = jnp.abs(rel)
    nf = jnp.maximum(n, 1).astype(_F32)
    large = max_exact + (jnp.log(nf / max_exact) / math.log(MAX_DISTANCE / max_exact)
                         * (half - max_exact)).astype(jnp.int32)
    large = jnp.minimum(large, half - 1)
    return ret + jnp.where(n < max_exact, n, large)


def _near_offsets(tq, tk):
    unit = math.gcd(tq, tk)
    first = -((tk + MAX_DISTANCE - 2) // unit)
    last = (tq + MAX_DISTANCE - 2) // unit
    return unit, first, last


def _bias_tiles(rel_bias, tq, tk):
    unit, u_min, u_max = _near_offsets(tq, tk)
    n_near = u_max - u_min + 1
    c = jnp.arange(tk, dtype=jnp.int32)[:, None]
    r = jnp.arange(tq, dtype=jnp.int32)[None, :]
    u = jnp.arange(u_min, u_max + 1, dtype=jnp.int32)[:, None, None]
    table = rel_bias.astype(_F32) * LOG2E
    bucket = _t5_bucket(u * unit + c - r)[:, None]
    tiles = jnp.zeros((n_near, DA_HEADS, tk, tq), _F32)
    for b in range(N_BUCKETS):
        tiles = jnp.where(bucket == b, table[b][None, :, None, None], tiles)
    far =table[_t5_bucket(jnp.array([-MAX_DISTANCE, MAX_DISTANCE], jnp.int32))]
    consts = jnp.concatenate([far, jnp.max(table, axis=0, keepdims=True)], axis=0)
    return tiles, consts.T.reshape(-1)


def _diff_attention(p3, vt, lam, subln, bias, far, layer, tq, tk):
    b, s, _ = p3.shape
    vrows = vt.shape[2]
    assert s % (2 * tk) == 0
    lam_init = 0.8 - 0.6 * math.exp(-0.3 * layer)
    kern = functools.partial(_attn_kernel, tq=tq, tk=tk, out_scale=1.0 - lam_init)
    nh = ATTN_HEADS_PER_STEP
    groups = DA_HEADS // nh
    return pl.pallas_call(
        kern,
        grid=(b, groups),
        in_specs=[pl.BlockSpec(memory_space=pltpu.SMEM),
                  pl.BlockSpec(memory_space=pltpu.SMEM),
                  pl.BlockSpec((1, s, nh * DA_W), lambda bi, h: (bi, 0, h)),
                  pl.BlockSpec((1, s, nh * DA_W), lambda bi, h: (bi, 0, groups + h)),
                  pl.BlockSpec((1, nh, vrows, s), lambda bi, h: (bi, h, 0, 0)),
                  pl.BlockSpec((bias.shape[0], nh, tk, tq), lambda bi, h: (0, h, 0, 0)),
                  pl.BlockSpec((1, DA_W), lambda bi, h: (0, 0))],
        out_specs=pl.BlockSpec((1, s, nh * DA_W), lambda bi, h: (bi, 0, h)),
        out_shape=jax.ShapeDtypeStruct((b, s, MIX_W), _BF),
        scratch_shapes=[pltpu.VMEM((2 * nh, tk, tq), _F32), pltpu.VMEM((2 * nh, tk, tq), _F32),
                        pltpu.VMEM((2 * nh, vrows, tq), _F32)],
        compiler_params=_params("parallel", "parallel"),
        name="diff_attn",
    )(lam, far, p3, p3, vt, bias, subln)


def _hgrn_chunk_stages(z, q, v, lb, st, tri, rev):
    c_rows = z.shape[0]
    e = jnp.exp(-jnp.abs(z))
    r = 1.0 / (1.0 + e)
    sig_pos, sig_neg = jnp.where(z >= 0, r, e * r), jnp.where(z >= 0, e * r, r)
    f = lb + (1.0 - lb) * sig_pos
    kk = (1.0 - lb) * sig_neg
    logf = jnp.log(f)
    hi = logf.astype(_BF)
    lo = (logf - hi.astype(_F32)).astype(_BF)
    cum = (jnp.dot(tri, hi, preferred_element_type=_F32)
           + jnp.dot(tri, lo, preferred_element_type=_F32))
    qh = _silu(q) * (HG_DK ** -0.5)
    vb = v.astype(_BF)
    yield None
    total = cum[0:1, :] if rev else cum[c_rows - 1:c_rows, :]
    o_inter = lax.dot_general((qh * jnp.exp(cum)).astype(_BF), st.astype(_BF), _NT,
                              preferred_element_type=_F32)
    ks = (kk * jnp.exp(total - cum)).astype(_BF)
    st_new = st * jnp.exp(total) + jnp.dot(v.T.astype(_BF), ks, preferred_element_type=_F32)
    blocks = []
    for b in range(c_rows // HG_SUB):
        lo_r, hi_r = b * HG_SUB, (b + 1) * HG_SUB
        if rev:
            cols = slice(lo_r, c_rows)
            base = cum[hi_r:hi_r + 1, :] if hi_r < c_rows else jnp.zeros_like(total)
        else:
            cols = slice(0, hi_r)
            base = cum[lo_r - 1:lo_r, :] if b else jnp.zeros_like(total)
        qq = (qh[lo_r:hi_r] * jnp.exp(cum[lo_r:hi_r] - base)).astype(_BF)
        kt = (kk[cols] * jnp.exp(base - cum[cols])).astype(_BF)
        a = lax.dot_general(qq, kt, _NT, preferred_element_type=_F32)
        blocks.append((lo_r, hi_r, cols, a))
    yield None
    parts = []
    for lo_r, hi_r, cols, a in blocks:
        sub_row = lax.broadcasted_iota(jnp.int32, a.shape, 0) + lo_r
        sub_col = lax.broadcasted_iota(jnp.int32, a.shape, 1) + cols.start
        seen = (sub_col >= sub_row) if rev else (sub_col <= sub_row)
        a = jnp.where(seen, a, 0.0)
        parts.append(o_inter[lo_r:hi_r]
                     + jnp.dot(a.astype(_BF), vb[cols], preferred_element_type=_F32))
    yield jnp.concatenate(parts, axis=0), st_new


def _hgrn_kernel(qf_ref, vf_ref, zf_ref, qb_ref, vb_ref, zb_ref, lb_ref, of_ref, ob_ref, st_ref,
                 *, rows):
    @pl.when(pl.program_id(1) == 0)
    def _():
        st_ref[...] = jnp.zeros_like(st_ref)

    c_rows = HG_CHUNK
    n_chunks = rows // c_rows
    row = lax.broadcasted_iota(jnp.int32, (c_rows, c_rows), 0)
    col = lax.broadcasted_iota(jnp.int32, (c_rows, c_rows), 1)
    tris = ((col <= row).astype(_BF), (col >= row).astype(_BF))
    dirs = ((qf_ref, vf_ref, zf_ref, of_ref), (qb_ref, vb_ref, zb_ref, ob_ref))

    def chunk(c, carry):
        starts = (pl.multiple_of(c * c_rows, c_rows),
                  pl.multiple_of((n_chunks - 1 - c) * c_rows, c_rows))
        for h0 in range(0, HG_HEADS, HG_HEAD_GROUP):
            chains = []
            for h in range(h0, h0 + HG_HEAD_GROUP):
                cs = slice(h * HG_DK, (h + 1) * HG_DK)
                for d, (q_ref, v_ref, z_ref, o_ref) in enumerate(dirs):
                    rs = pl.ds(starts[d], c_rows)
                    gen = _hgrn_chunk_stages(z_ref[0, rs, cs], q_ref[0, rs, cs], v_ref[0, rs, cs],
                                             lb_ref[d, :, cs], st_ref[d, h], tris[d], rev=bool(d))
                    chains.append((gen, o_ref, rs, cs, d, h))
            for _ in range(2):
                for chain in chains:
                    next(chain[0])
            for gen, o_ref, rs, cs, d, h in chains:
                o, st = next(gen)
                o_ref[0, rs, cs] = o
                st_ref[d, h] = st
        return carry

    lax.fori_loop(0, n_chunks, chunk, 0)


def _hgrn_scan(p3, lb, rows):
    b, s, _ = p3.shape
    w = MIX_W
    t = s // rows
    fwd = lambda c: pl.BlockSpec((1, rows, w), lambda bi, i: (bi, i, c))
    bwd = lambda c: pl.BlockSpec((1, rows, w), lambda bi, i: (bi, t - 1 - i, c))
    out = jax.ShapeDtypeStruct((b, s, w), _F32)
    return pl.pallas_call(
        functools.partial(_hgrn_kernel, rows=rows),
        grid=(b, t),
        in_specs=[fwd(0), fwd(1), fwd(3), bwd(0), bwd(1), bwd(4),
                  pl.BlockSpec((2, 1, w), lambda bi, i: (0, 0, 0))],
        out_specs=[fwd(0), bwd(0)],
        out_shape=[out, out],
        scratch_shapes=[pltpu.VMEM((2, HG_HEADS, HG_DK, HG_DK), _F32)],
        compiler_params=_params("parallel", "arbitrary"),
        name="hgrn_scan",
    )(p3, p3, p3, p3, p3, p3, lb)


def _hgrn_finish_kernel(of_ref, ob_ref, g_ref, w_ref, o_ref):
    o = of_ref[...] + ob_ref[...]
    gate = _silu(g_ref[...])
    w = w_ref[...]
    for h in range(HG_HEADS):
        cs = slice(h * HG_DK, (h + 1) * HG_DK)
        oh = o[:, cs]
        ms = jnp.mean(oh * oh, axis=-1, keepdims=True)
        o_ref[:, cs] = (oh * lax.rsqrt(ms + LN_EPS) * w[:, cs] * gate[:, cs]).astype(o_ref.dtype)


def _hgrn_finish(o_fw, o_bw, p2, g_col, norm_w, tm):
    n = o_fw.shape[0]
    return pl.pallas_call(
        _hgrn_finish_kernel,
        grid=(n // tm,),
        in_specs=[pl.BlockSpec((tm, MIX_W), lambda i: (i, 0)),
                  pl.BlockSpec((tm, MIX_W), lambda i: (i, 0)),
                  pl.BlockSpec((tm, MIX_W), lambda i: (i, g_col)),
                  pl.BlockSpec((1, MIX_W), lambda i: (0, 0))],
        out_specs=pl.BlockSpec((tm, MIX_W), lambda i: (i, 0)),
        out_shape=jax.ShapeDtypeStruct((n, MIX_W), _BF),
        compiler_params=_params("parallel"),
        name="hgrn_finish",
    )(o_fw, o_bw, p2, norm_w)


def _sconv_kernel(gb_ref, gc_ref, h_ref, gcp_ref, hp_ref, gcn_ref, hn_ref, w_ref, o_ref, *, halo):
    i = pl.program_id(1)
    u = gc_ref[0].astype(_F32) * h_ref[0].astype(_F32)
    n = u.shape[0]
    u_prev = (gcp_ref[0].astype(_F32) * hp_ref[0].astype(_F32))[halo - 1:halo]
    u_next = (gcn_ref[0].astype(_F32) * hn_ref[0].astype(_F32))[0:1]
    u_prev = jnp.where(i == 0, 0.0, u_prev)
    u_next = jnp.where(i == pl.num_programs(1) - 1, 0.0, u_next)
    row = lax.broadcasted_iota(jnp.int32, u.shape, 0)
    down = jnp.where(row == 0, u_prev, pltpu.roll(u, 1, 0))
    up = jnp.where(row == n - 1, u_next, pltpu.roll(u, n - 1, 0))
    w = w_ref[...]
    y = down * w[0:1] + u * w[1:2] + up * w[2:3]
    o_ref[0] = (gb_ref[0].astype(_F32) * y).astype(o_ref.dtype)


def _short_conv(p3, conv_w, ts):
    b, s, _ = p3.shape
    halo = SUBLANES_BF16
    hb = ts // halo
    last = s // halo - 1
    main = lambda c: pl.BlockSpec((1, ts, MIX_W), lambda bi, i: (bi, i, c))
    prev = lambda c: pl.BlockSpec((1, halo, MIX_W), lambda bi, i: (bi, jnp.maximum(i * hb - 1, 0), c))
    nxt = lambda c: pl.BlockSpec((1, halo, MIX_W), lambda bi, i: (bi, jnp.minimum((i + 1) * hb, last), c))
    return pl.pallas_call(
        functools.partial(_sconv_kernel, halo=halo),
        grid=(b, s // ts),
        in_specs=[main(0), main(1), main(2), prev(1), prev(2), nxt(1), nxt(2),
                  pl.BlockSpec((3, MIX_W), lambda bi, i: (0, 0))],
        out_specs=pl.BlockSpec((1, ts, MIX_W), lambda bi, i: (bi, i, 0)),
        out_shape=jax.ShapeDtypeStruct((b, s, MIX_W), _BF),
        compiler_params=_params("parallel", "parallel"),
        name="short_conv",
    )(p3, p3, p3, p3, p3, p3, p3, conv_w)


def _tile(n, pref):
    return pref if n % pref == 0 else n


def kernel(x, mem, rel_bias, attn_w_in, attn_lambda, attn_subln, hgrn_w_in, hgrn_lower_bound,
           hgrn_norm, conv_w_in, conv_w, mem_w_kv, w_o, ln_gain, ln_bias, ffn_w_up, ffn_conv,
           ffn_w_down):
    b, s, d = x.shape
    n = b * s
    m = mem.shape[1]
    tm = _tile(s, ROW_TILE)
    hg_rows = _tile(s, HG_ROWS)

    bias, far = _bias_tiles(rel_bias, ATTN_Q_TILE, ATTN_K_TILE)
    mem2 = mem.reshape(b * m, d)
    lbw = jax.nn.softmax(hgrn_lower_bound.astype(_F32), axis=0)
    lb_all = jnp.cumsum(lbw, axis=0) - lbw[0]

    for layer in range(DEPTH):
        kind, j = layer % N_MIXERS, layer // N_MIXERS
        x2 = x.reshape(n, d)
        kv = _matmul(mem2, mem_w_kv[layer].astype(_BF), _BF, _tile(b * m, ROW_TILE))
        kv = kv.reshape(b, m, 2 * XATTN_W)
        if kind == 0:
            q_scale = jnp.where(jnp.arange(attn_w_in.shape[2]) < MIX_W, DA_HD ** -0.5 * LOG2E, 1.0)
            w_in = (attn_w_in[j] * q_scale).astype(_BF)
            p = _matmul(x2, w_in, _BF, tm).reshape(b, s, -1)
            lp = attn_lambda[j].astype(_F32)
            lam_init = 0.8 - 0.6 * math.exp(-0.3 * layer)
            lam = jnp.exp(jnp.sum(lp[0] * lp[1])) - jnp.exp(jnp.sum(lp[2] * lp[3])) + lam_init
            vt = jnp.swapaxes(p[..., 2 * MIX_W:3 * MIX_W], 1, 2).reshape(b, DA_HEADS, DA_W, s)
            vt = jnp.concatenate([vt, jnp.ones((b, DA_HEADS, SUBLANES_BF16, s), _BF)], axis=2)
            mixed = _diff_attention(p, vt, lam.reshape(1),
                                    attn_subln[j].astype(_F32).reshape(1, DA_W),
                                    bias, far, layer, ATTN_Q_TILE, ATTN_K_TILE)
            q_col = 3 * MIX_W // XATTN_W
        elif kind == 1:
            w_in = hgrn_w_in[j].astype(_BF)
            p = _matmul(x2, w_in, _F32, tm).reshape(b, s, -1)
            o_fw, o_bw = _hgrn_scan(p, lb_all[layer].reshape(2, 1, MIX_W), hg_rows)
            mixed = _hgrn_finish(o_fw.reshape(n, MIX_W), o_bw.reshape(n, MIX_W),
                                 p.reshape(n, -1), 2,
                                 hgrn_norm[j].astype(_F32).reshape(1, MIX_W), tm)
            q_col = 5 * MIX_W // XATTN_W
        else:
            w_in = conv_w_in[j].astype(_BF)
            p = _matmul(x2, w_in, _BF, tm).reshape(b, s, -1)
            mixed = _short_conv(p, conv_w[j].astype(_F32), tm)
            q_col = 3 * MIX_W // XATTN_W
        recalled = _memory_attention(p, kv, q_col, tm)
        x2 = _out_ln(x2, mixed.reshape(n, MIX_W), recalled.reshape(n, XATTN_W),
                     w_o[layer].astype(_BF), ln_gain[layer, 0].reshape(1, d),
                     ln_bias[layer, 0].reshape(1, d), tm)
        x = _conv_ffn_ln(x2.reshape(b, s, d), ffn_w_up[layer].astype(_BF),
                         ffn_conv[layer].astype(_F32), ffn_w_down[layer].astype(_BF),
                         ln_gain[layer, 1].reshape(1, d), ln_bias[layer, 1].reshape(1, d),
                         tm, FFN_TILE)
    return x
```

```python
import functools
import math

import jax
import jax.numpy as jnp
from jax import lax
from jax.experimental import pallas as pl
from jax.experimental.pallas import tpu as pltpu

D_MODEL = 1024
DEPTH = 4
N_MIXERS = 3
MIX_W = 3 * D_MODEL // 4
XATTN_HEADS = 4
XATTN_W = D_MODEL - MIX_W
XATTN_HD = XATTN_W // XATTN_HEADS
DA_HD = 64
DA_HEADS = MIX_W // (2 * DA_HD)
DA_W = 2 * DA_HD
N_BUCKETS = 32
MAX_DISTANCE = 128
HG_DK = 128
HG_HEADS = MIX_W // HG_DK
HG_CHUNK = 64
HG_SUB = 16
D_FF = ((8 * D_MODEL // 3 + 255) // 256) * 256
LN_EPS = 1e-5
ALPHA = (2 * DEPTH) ** 0.25
LOG2E = math.log2(math.e)

LANES = 128
SUBLANES_F32 = 8
SUBLANES_BF16 = 16
VMEM_LIMIT = 52 * 1024 * 1024

ROW_TILE = 512
ATTN_Q_TILE = 256
ATTN_K_TILE = 1024
ATTN_HEADS_PER_STEP = 1
ATTN_ROW_CHUNK = 256
MXU_WIDTH = 256
FFN_TILE = MXU_WIDTH
FFN_OUT_ROWS = 256
HG_ROWS = 512
HG_HEAD_GROUP = 6

_NEG = -1e30
_BF = jnp.bfloat16
_F32 = jnp.float32
_NT = (((1,), (1,)), ((), ()))


def _params(*sem):
    return pltpu.CompilerParams(dimension_semantics=sem, vmem_limit_bytes=VMEM_LIMIT)


def _sigmoid(z):
    e = jnp.exp(-jnp.abs(z))
    r = 1.0 / (1.0 + e)
    return jnp.where(z >= 0, r, e * r)


def _silu(z):
    return z * _sigmoid(z)


def _layer_norm(r, g, b):
    mu = jnp.mean(r, axis=-1, keepdims=True)
    rc = r - mu
    var = jnp.mean(rc * rc, axis=-1, keepdims=True)
    return rc * lax.rsqrt(var + LN_EPS) * g + b


def _mm_kernel(x_ref, w_ref, o_ref):
    o_ref[...] = jnp.dot(x_ref[...].astype(_BF), w_ref[...],
                         preferred_element_type=_F32).astype(o_ref.dtype)


def _matmul(x, w, out_dtype, tm):
    m, k = x.shape
    n = w.shape[1]
    return pl.pallas_call(
        _mm_kernel,
        grid=(m // tm,),
        in_specs=[pl.BlockSpec((tm, k), lambda i: (i, 0)),
                  pl.BlockSpec((k, n), lambda i: (0, 0), pipeline_mode=pl.Buffered(1))],
        out_specs=pl.BlockSpec((tm, n), lambda i: (i, 0)),
        out_shape=jax.ShapeDtypeStruct((m, n), out_dtype),
        compiler_params=_params("parallel"),
        name="proj",
    )(x, w)


def _memattn_kernel(q_ref, km_ref, vm_ref, o_ref):
    q = q_ref[0].astype(_BF)
    km = km_ref[0]
    vm = vm_ref[0]
    lane = lax.broadcasted_iota(jnp.int32, (1, XATTN_W), 1)
    acc = jnp.zeros(q.shape, _F32)
    for h in range(XATTN_HEADS):
        head = (lane >= h * XATTN_HD) & (lane < (h + 1) * XATTN_HD)
        qh = jnp.where(head, q, jnp.zeros_like(q))
        s = lax.dot_general(qh, km, _NT, preferred_element_type=_F32) * (XATTN_HD ** -0.5)
        p = jnp.exp(s - jnp.max(s, axis=-1, keepdims=True))
        l = jnp.sum(p, axis=-1, keepdims=True)
        vh = jnp.where(head, vm, jnp.zeros_like(vm))
        acc = acc + jnp.dot(p.astype(_BF), vh, preferred_element_type=_F32) / l
    o_ref[0] = acc.astype(o_ref.dtype)


def _memory_attention(p3, kv, q_col, tm):
    b, s, _ = p3.shape
    m = kv.shape[1]
    return pl.pallas_call(
        _memattn_kernel,
        grid=(b, s // tm),
        in_specs=[pl.BlockSpec((1, tm, XATTN_W), lambda bi, i: (bi, i, q_col)),
                  pl.BlockSpec((1, m, XATTN_W), lambda bi, i: (bi, 0, 0)),
                  pl.BlockSpec((1, m, XATTN_W), lambda bi, i: (bi, 0, 1))],
        out_specs=pl.BlockSpec((1, tm, XATTN_W), lambda bi, i: (bi, i, 0)),
        out_shape=jax.ShapeDtypeStruct((b, s, XATTN_W), _BF),
        compiler_params=_params("parallel", "parallel"),
        name="memattn",
    )(p3, kv, kv)


def _out_ln_kernel(x_ref, mix_ref, rec_ref, wo_ref, g_ref, b_ref, o_ref):
    y = jnp.dot(mix_ref[...], wo_ref[:MIX_W, :], preferred_element_type=_F32)
    y = y + jnp.dot(rec_ref[...], wo_ref[MIX_W:, :], preferred_element_type=_F32)
    o_ref[...] = _layer_norm(ALPHA * x_ref[...] + y, g_ref[...], b_ref[...])


def _out_ln(x2, mixed, recalled, wo, g, b, tm):
    n = x2.shape[0]
    return pl.pallas_call(
        _out_ln_kernel,
        grid=(n // tm,),
        in_specs=[pl.BlockSpec((tm, D_MODEL), lambda i: (i, 0)),
                  pl.BlockSpec((tm, MIX_W), lambda i: (i, 0)),
                  pl.BlockSpec((tm, XATTN_W), lambda i: (i, 0)),
                  pl.BlockSpec((D_MODEL, D_MODEL), lambda i: (0, 0)),
                  pl.BlockSpec((1, D_MODEL), lambda i: (0, 0)),
                  pl.BlockSpec((1, D_MODEL), lambda i: (0, 0))],
        out_specs=pl.BlockSpec((tm, D_MODEL), lambda i: (i, 0)),
        out_shape=jax.ShapeDtypeStruct((n, D_MODEL), _F32),
        compiler_params=_params("parallel"),
        name="out_ln",
    )(x2, mixed, recalled, wo, g, b)


def _dwconv_rows(h, w):
    n = h.shape[0]
    return (pltpu.roll(h, 1, 0) * w[0:1] + h * w[1:2] + pltpu.roll(h, n - 1, 0) * w[2:3])


def _ffn_kernel(x_ref, xp_ref, xn_ref, wu_ref, wc_ref, wd_ref, g_ref, b_ref, o_ref, gated_ref,
                *, ts, halo, fb):
    i = pl.program_id(1)
    prev = jnp.where(i == 0, 0.0, xp_ref[0])
    nxt = jnp.where(i == pl.num_programs(1) - 1, 0.0, xn_ref[0])
    xb = jnp.concatenate([prev, x_ref[0], nxt], axis=0).astype(_BF)
    for c in range(D_FF // fb):
        ca = slice(c * fb, (c + 1) * fb)
        cv = slice(D_FF + c * fb, D_FF + (c + 1) * fb)
        ha = jnp.dot(xb, wu_ref[:, ca], preferred_element_type=_F32)
        hv = jnp.dot(xb, wu_ref[:, cv], preferred_element_type=_F32)
        a = _dwconv_rows(ha, wc_ref[:, ca])[halo:halo + ts]
        v = _dwconv_rows(hv, wc_ref[:, cv])[halo:halo + ts]
        gated_ref[:, ca] = (_silu(a) * v).astype(_BF)
    for r0 in range(0, ts, FFN_OUT_ROWS):
        rows = slice(r0, r0 + FFN_OUT_ROWS)
        f = jnp.dot(gated_ref[rows, :], wd_ref[...], preferred_element_type=_F32)
        o_ref[0, rows, :] = _layer_norm(ALPHA * x_ref[0, rows, :] + f, g_ref[...], b_ref[...])


def _conv_ffn_ln(x3, w_up, w_conv, w_down, g, b, ts, fb):
    bsz, s, d = x3.shape
    halo = SUBLANES_BF16
    hb = ts // halo
    last = s // halo - 1
    kern = functools.partial(_ffn_kernel, ts=ts, halo=halo, fb=fb)
    resident = lambda shape: pl.BlockSpec(shape, lambda bi, i: (0, 0),
                                          pipeline_mode=pl.Buffered(1))
    return pl.pallas_call(
        kern,
        grid=(bsz, s // ts),
        in_specs=[pl.BlockSpec((1, ts, d), lambda bi, i: (bi, i, 0)),
                  pl.BlockSpec((1, halo, d), lambda bi, i: (bi, jnp.maximum(i * hb - 1, 0), 0)),
                  pl.BlockSpec((1, halo, d), lambda bi, i: (bi, jnp.minimum((i + 1) * hb, last), 0)),
                  resident((d, 2 * D_FF)),
                  resident((3, 2 * D_FF)),
                  resident((D_FF, d)),
                  pl.BlockSpec((1, d), lambda bi, i: (0, 0)),
                  pl.BlockSpec((1, d), lambda bi, i: (0, 0))],
        out_specs=pl.BlockSpec((1, ts, d), lambda bi, i: (bi, i, 0)),
        out_shape=jax.ShapeDtypeStruct((bsz, s, d), _F32),
        scratch_shapes=[pltpu.VMEM((ts, D_FF), _BF)],
        compiler_params=_params("parallel", "parallel"),
        name="conv_ffn",
    )(x3, x3, x3, w_up, w_conv, w_down, g, b)


def _fold_rows(x, op, group=SUBLANES_F32):
    acc = x[:group]
    for g in range(1, x.shape[0] // group):
        acc = op(acc, x[g * group:(g + 1) * group])
    while acc.shape[0] > SUBLANES_F32:
        half = acc.shape[0] // 2
        acc = op(acc[:half], acc[half:])
    return acc


def _attn_kernel(lam_ref, far_ref, q_ref, k_ref, vt_ref, bias_ref, subln_ref, o_ref,
                 s_a, s_b, acc_ref, *, tq, tk, out_scale):
    heads = q_ref.shape[2] // DA_W
    streams = [(hh, mi) for hh in range(heads) for mi in range(2)]
    n_str = len(streams)
    hp = pl.program_id(1)
    nkv = k_ref.shape[1] // tk
    nq = q_ref.shape[1] // tq
    n_chunks = tk // ATTN_ROW_CHUNK
    lane = lax.broadcasted_iota(jnp.int32, (1, heads * DA_W), 1)

    unit, u_min, u_max = _near_offsets(tq, ATTN_ROW_CHUNK)
    consts = [[far_ref[3 * (heads * hp + hh) + c] for c in range(3)] for hh in range(heads)]

    def stream_queries(i):
        q = q_ref[0, pl.ds(pl.multiple_of(i * tq, tq), tq), :]
        zero = jnp.zeros_like(q)
        return [jnp.where((lane >= hh * DA_W + mi * DA_HD) & (lane < hh * DA_W + (mi + 1) * DA_HD),
                          q, zero).T for hh, mi in streams]

    def side_select(i, j, r, left, mid, right):
        u = (tk * j + ATTN_ROW_CHUNK * r - tq * i) // unit
        return jnp.where(u < u_min, left, jnp.where(u > u_max, right, mid))

    def max_bound(i, j, r, hh):
        return side_select(i, j, r, consts[hh][0], consts[hh][2], consts[hh][1])

    def score_rows(qs, i, j, r, s_out, st):
        k0 = pl.multiple_of(j * tk + r * ATTN_ROW_CHUNK, ATTN_ROW_CHUNK)
        s = jnp.dot(k_ref[0, pl.ds(k0, ATTN_ROW_CHUNK), :], qs[st], preferred_element_type=_F32)
        s_out[st, r * ATTN_ROW_CHUNK:(r + 1) * ATTN_ROW_CHUNK, :] = s
        return _fold_rows(s, jnp.maximum) + max_bound(i, j, r, streams[st][0])

    def add_near_bias(i, j, s_out):
        for r in range(n_chunks):
            u = (tk * j + ATTN_ROW_CHUNK * r - tq * i) // unit
            rows = slice(r * ATTN_ROW_CHUNK, (r + 1) * ATTN_ROW_CHUNK)

            @pl.when((u >= u_min) & (u <= u_max))
            def _():
                for st, (hh, _) in enumerate(streams):
                    s_out[st, rows, :] = s_out[st, rows, :] + bias_ref[u - u_min, hh]

    def half_step(i, j, carry, s_cur, s_nxt, nxt):
        ni, nj, nqs = nxt
        tmax = [None] * n_str
        pv = [None] * n_str
        for r in range(n_chunks):
            rows = slice(r * ATTN_ROW_CHUNK, (r + 1) * ATTN_ROW_CHUNK)
            for st in range(n_str):
                t = score_rows(nqs, ni, nj, r, s_nxt, st)
                tmax[st] = t if r == 0 else jnp.maximum(tmax[st], t)
            v0 = pl.multiple_of(j * tk + r * ATTN_ROW_CHUNK, ATTN_ROW_CHUNK)
            for st, (hh, _) in enumerate(streams):
                shift = side_select(i, j, r, consts[hh][0], 0.0, consts[hh][1])
                p = jnp.exp2(s_cur[st, rows, :] - (carry[st][0] - shift)).astype(_BF)
                d = jnp.dot(vt_ref[0, hh, :, pl.ds(v0, ATTN_ROW_CHUNK)], p,
                            preferred_element_type=_F32)
                pv[st] = d if r == 0 else pv[st] + d
        new, tile_max = [], []
        for st in range(n_str):
            m, alpha = carry[st]
            acc_ref[st] = alpha * acc_ref[st] + pv[st]
            cand = jnp.max(tmax[st], axis=0, keepdims=True)
            m_nxt = jnp.maximum(m, cand)
            new.append((m_nxt, jnp.exp2(m - m_nxt)))
            tile_max.append(cand)
        add_near_bias(ni, nj, s_nxt)
        return tuple(new), tuple(tile_max)

    qs0 = stream_queries(0)
    first = []
    for st in range(n_str):
        parts = [score_rows(qs0, 0, 0, r, s_a, st) for r in range(n_chunks)]
        first.append(jnp.max(functools.reduce(jnp.maximum, parts), axis=0, keepdims=True))
    add_near_bias(0, 0, s_a)
    lam = lam_ref[0]

    def query_tile(i, m_first):
        qs = stream_queries(i)
        acc_ref[...] = jnp.zeros_like(acc_ref)
        carry = tuple((m_first[st], jnp.zeros((1, tq), _F32)) for st in range(n_str))

        def pair(jj, carry):
            j = 2 * jj
            carry, _ = half_step(i, j, carry, s_a, s_b, (i, j + 1, qs))
            carry, _ = half_step(i, j + 1, carry, s_b, s_a, (i, j + 2, qs))
            return carry

        carry = lax.fori_loop(0, nkv // 2 - 1, pair, carry)
        carry, _ = half_step(i, nkv - 2, carry, s_a, s_b, (i, nkv - 1, qs))
        i_next = jnp.minimum(i + 1, nq - 1)
        _, m_next = half_step(i, nkv - 1, carry, s_b, s_a, (i_next, 0, stream_queries(i_next)))

        q0 = pl.multiple_of(i * tq, tq)
        for hh in range(heads):
            o0 = acc_ref[2 * hh, :DA_W] / acc_ref[2 * hh, DA_W:DA_W + 1]
            o1 = acc_ref[2 * hh + 1, :DA_W] / acc_ref[2 * hh + 1, DA_W:DA_W + 1]
            o = (o0 - lam * o1).T
            ms = jnp.mean(o * o, axis=-1, keepdims=True)
            o_ref[0, pl.ds(q0, tq), hh * DA_W:(hh + 1) * DA_W] = (
                o * lax.rsqrt(ms + LN_EPS) * subln_ref[...] * out_scale).astype(o_ref.dtype)
        return m_next

    lax.fori_loop(0, nq, query_tile, tuple(first))


def _t5_bucket(rel):
    half = N_BUCKETS // 2
    max_exact = half // 2
    ret = jnp.where(rel > 0, half, 0)
    n = jnp.abs(rel)
    nf = jnp.maximum(n, 1).astype(_F32)
    large = max_exact + (jnp.log(nf / max_exact) / math.log(MAX_DISTANCE / max_exact)
                         * (half - max_exact)).astype(jnp.int32)
    large = jnp.minimum(large, half - 1)
    return ret + jnp.where(n < max_exact, n, large)


def _near_offsets(tq, tk):
    unit = math.gcd(tq, tk)
    first = -((tk + MAX_DISTANCE - 2) // unit)
    last = (tq + MAX_DISTANCE - 2) // unit
    return unit, first, last


def _bias_tiles(rel_bias, tq, tk):
    unit, u_min, u_max = _near_offsets(tq, tk)
    n_near = u_max - u_min + 1
    c = jnp.arange(tk, dtype=jnp.int32)[:, None]
    r = jnp.arange(tq, dtype=jnp.int32)[None, :]
    u = jnp.arange(u_min, u_max + 1, dtype=jnp.int32)[:, None, None]
    table = rel_bias.astype(_F32) * LOG2E
    bucket = _t5_bucket(u * unit + c - r)[:, None]
    tiles = jnp.zeros((n_near, DA_HEADS, tk, tq), _F32)
    for b in range(N_BUCKETS):
        tiles = jnp.where(bucket == b, table[b][None, :, None, None], tiles)
    far =table[_t5_bucket(jnp.array([-MAX_DISTANCE, MAX_DISTANCE], jnp.int32))]
    consts = jnp.concatenate([far, jnp.max(table, axis=0, keepdims=True)], axis=0)
    return tiles, consts.T.reshape(-1)


def _diff_attention(p3, vt, lam, subln, bias, far, layer, tq, tk):
    b, s, _ = p3.shape
    vrows = vt.shape[2]
    assert s % (2 * tk) == 0
    lam_init = 0.8 - 0.6 * math.exp(-0.3 * layer)
    kern = functools.partial(_attn_kernel, tq=tq, tk=tk, out_scale=1.0 - lam_init)
    nh = ATTN_HEADS_PER_STEP
    groups = DA_HEADS // nh
    return pl.pallas_call(
        kern,
        grid=(b, groups),
        in_specs=[pl.BlockSpec(memory_space=pltpu.SMEM),
                  pl.BlockSpec(memory_space=pltpu.SMEM),
                  pl.BlockSpec((1, s, nh * DA_W), lambda bi, h: (bi, 0, h)),
                  pl.BlockSpec((1, s, nh * DA_W), lambda bi, h: (bi, 0, groups + h)),
                  pl.BlockSpec((1, nh, vrows, s), lambda bi, h: (bi, h, 0, 0)),
                  pl.BlockSpec((bias.shape[0], nh) + bias.shape[2:], lambda bi, h: (0, h, 0, 0)),
                  pl.BlockSpec((1, DA_W), lambda bi, h: (0, 0))],
        out_specs=pl.BlockSpec((1, s, nh * DA_W), lambda bi, h: (bi, 0, h)),
        out_shape=jax.ShapeDtypeStruct((b, s, MIX_W), _BF),
        scratch_shapes=[pltpu.VMEM((2 * nh, tk, tq), _F32), pltpu.VMEM((2 * nh, tk, tq), _F32),
                        pltpu.VMEM((2 * nh, vrows, tq), _F32)],
        compiler_params=_params("parallel", "parallel"),
        name="diff_attn",
    )(lam, far, p3, p3, vt, bias, subln)


def _hgrn_chunk_stages(z, q, v, lb, st, tri, rev):
    c_rows = z.shape[0]
    e = jnp.exp(-jnp.abs(z))
    r = 1.0 / (1.0 + e)
    sig_pos, sig_neg = jnp.where(z >= 0, r, e * r), jnp.where(z >= 0, e * r, r)
    f = lb + (1.0 - lb) * sig_pos
    kk = (1.0 - lb) * sig_neg
    logf = jnp.log(f)
    hi = logf.astype(_BF)
    lo = (logf - hi.astype(_F32)).astype(_BF)
    cum = (jnp.dot(tri, hi, preferred_element_type=_F32)
           + jnp.dot(tri, lo, preferred_element_type=_F32))
    qh = _silu(q) * (HG_DK ** -0.5)
    vb = v.astype(_BF)
    yield None
    total = cum[0:1, :] if rev else cum[c_rows - 1:c_rows, :]
    o_inter = lax.dot_general((qh * jnp.exp(cum)).astype(_BF), st.astype(_BF), _NT,
                              preferred_element_type=_F32)
    ks = (kk * jnp.exp(total - cum)).astype(_BF)
    st_new = st * jnp.exp(total) + jnp.dot(v.T.astype(_BF), ks, preferred_element_type=_F32)
    blocks = []
    for b in range(c_rows // HG_SUB):
        lo_r, hi_r = b * HG_SUB, (b + 1) * HG_SUB
        if rev:
            cols = slice(lo_r, c_rows)
            base = cum[hi_r:hi_r + 1, :] if hi_r < c_rows else jnp.zeros_like(total)
        else:
            cols = slice(0, hi_r)
            base = cum[lo_r - 1:lo_r, :] if b else jnp.zeros_like(total)
        qq = (qh[lo_r:hi_r] * jnp.exp(cum[lo_r:hi_r] - base)).astype(_BF)
        kt = (kk[cols] * jnp.exp(base - cum[cols])).astype(_BF)
        a = lax.dot_general(qq, kt, _NT, preferred_element_type=_F32)
        blocks.append((lo_r, hi_r, cols, a))
    yield None
    parts = []
    for lo_r, hi_r, cols, a in blocks:
        sub_row = lax.broadcasted_iota(jnp.int32, a.shape, 0) + lo_r
        sub_col = lax.broadcasted_iota(jnp.int32, a.shape, 1) + cols.start
        seen = (sub_col >= sub_row) if rev else (sub_col <= sub_row)
        a = jnp.where(seen, a, 0.0)
        parts.append(o_inter[lo_r:hi_r]
                     + jnp.dot(a.astype(_BF), vb[cols], preferred_element_type=_F32))
    yield jnp.concatenate(parts, axis=0), st_new


def _hgrn_kernel(qf_ref, vf_ref, zf_ref, qb_ref, vb_ref, zb_ref, lb_ref, of_ref, ob_ref, st_ref,
                 *, rows):
    @pl.when(pl.program_id(1) == 0)
    def _():
        st_ref[...] = jnp.zeros_like(st_ref)

    c_rows = HG_CHUNK
    n_chunks = rows // c_rows
    row = lax.broadcasted_iota(jnp.int32, (c_rows, c_rows), 0)
    col = lax.broadcasted_iota(jnp.int32, (c_rows, c_rows), 1)
    tris = ((col <= row).astype(_BF), (col >= row).astype(_BF))
    dirs = ((qf_ref, vf_ref, zf_ref, of_ref), (qb_ref, vb_ref, zb_ref, ob_ref))

    def chunk(c, carry):
        starts = (pl.multiple_of(c * c_rows, c_rows),
                  pl.multiple_of((n_chunks - 1 - c) * c_rows, c_rows))
        for h0 in range(0, HG_HEADS, HG_HEAD_GROUP):
            chains = []
            for h in range(h0, h0 + HG_HEAD_GROUP):
                cs = slice(h * HG_DK, (h + 1) * HG_DK)
                for d, (q_ref, v_ref, z_ref, o_ref) in enumerate(dirs):
                    rs = pl.ds(starts[d], c_rows)
                    gen = _hgrn_chunk_stages(z_ref[0, rs, cs], q_ref[0, rs, cs], v_ref[0, rs, cs],
                                             lb_ref[d, :, cs], st_ref[d, h], tris[d], rev=bool(d))
                    chains.append((gen, o_ref, rs, cs, d, h))
            for _ in range(2):
                for chain in chains:
                    next(chain[0])
            for gen, o_ref, rs, cs, d, h in chains:
                o, st = next(gen)
                o_ref[0, rs, cs] = o
                st_ref[d, h] = st
        return carry

    lax.fori_loop(0, n_chunks, chunk, 0)


def _hgrn_scan(p3, lb, rows):
    b, s, _ = p3.shape
    w = MIX_W
    t = s // rows
    fwd = lambda c: pl.BlockSpec((1, rows, w), lambda bi, i: (bi, i, c))
    bwd = lambda c: pl.BlockSpec((1, rows, w), lambda bi, i: (bi, t - 1 - i, c))
    out = jax.ShapeDtypeStruct((b, s, w), _F32)
    return pl.pallas_call(
        functools.partial(_hgrn_kernel, rows=rows),
        grid=(b, t),
        in_specs=[fwd(0), fwd(1), fwd(3), bwd(0), bwd(1), bwd(4),
                  pl.BlockSpec((2, 1, w), lambda bi, i: (0, 0, 0))],
        out_specs=[fwd(0), bwd(0)],
        out_shape=[out, out],
        scratch_shapes=[pltpu.VMEM((2, HG_HEADS, HG_DK, HG_DK), _F32)],
        compiler_params=_params("parallel", "arbitrary"),
        name="hgrn_scan",
    )(p3, p3, p3, p3, p3, p3, lb)


def _hgrn_finish_kernel(of_ref, ob_ref, g_ref, w_ref, o_ref):
    o = of_ref[...] + ob_ref[...]
    gate = _silu(g_ref[...])
    w = w_ref[...]
    for h in range(HG_HEADS):
        cs = slice(h * HG_DK, (h + 1) * HG_DK)
        oh = o[:, cs]
        ms = jnp.mean(oh * oh, axis=-1, keepdims=True)
        o_ref[:, cs] = (oh * lax.rsqrt(ms + LN_EPS) * w[:, cs] * gate[:, cs]).astype(o_ref.dtype)


def _hgrn_finish(o_fw, o_bw, p2, g_col, norm_w, tm):
    n = o_fw.shape[0]
    return pl.pallas_call(
        _hgrn_finish_kernel,
        grid=(n // tm,),
        in_specs=[pl.BlockSpec((tm, MIX_W), lambda i: (i, 0)),
                  pl.BlockSpec((tm, MIX_W), lambda i: (i, 0)),
                  pl.BlockSpec((tm, MIX_W), lambda i: (i, g_col)),
                  pl.BlockSpec((1, MIX_W), lambda i: (0, 0))],
        out_specs=pl.BlockSpec((tm, MIX_W), lambda i: (i, 0)),
        out_shape=jax.ShapeDtypeStruct((n, MIX_W), _BF),
        compiler_params=_params("parallel"),
        name="hgrn_finish",
    )(o_fw, o_bw, p2, norm_w)


def _sconv_kernel(gb_ref, gc_ref, h_ref, gcp_ref, hp_ref, gcn_ref, hn_ref, w_ref, o_ref, *, halo):
    i = pl.program_id(1)
    u = gc_ref[0].astype(_F32) * h_ref[0].astype(_F32)
    n = u.shape[0]
    u_prev = (gcp_ref[0].astype(_F32) * hp_ref[0].astype(_F32))[halo - 1:halo]
    u_next = (gcn_ref[0].astype(_F32) * hn_ref[0].astype(_F32))[0:1]
    u_prev = jnp.where(i == 0, 0.0, u_prev)
    u_next = jnp.where(i == pl.num_programs(1) - 1, 0.0, u_next)
    row = lax.broadcasted_iota(jnp.int32, u.shape, 0)
    down = jnp.where(row == 0, u_prev, pltpu.roll(u, 1, 0))
    up = jnp.where(row == n - 1, u_next, pltpu.roll(u, n - 1, 0))
    w = w_ref[...]
    y = down * w[0:1] + u * w[1:2] + up * w[2:3]
    o_ref[0] = (gb_ref[0].astype(_F32) * y).astype(o_ref.dtype)


def _short_conv(p3, conv_w, ts):
    b, s, _ = p3.shape
    halo = SUBLANES_BF16
    hb = ts // halo
    last = s // halo - 1
    main = lambda c: pl.BlockSpec((1, ts, MIX_W), lambda bi, i: (bi, i, c))
    prev = lambda c: pl.BlockSpec((1, halo, MIX_W), lambda bi, i: (bi, jnp.maximum(i * hb - 1, 0), c))
    nxt = lambda c: pl.BlockSpec((1, halo, MIX_W), lambda bi, i: (bi, jnp.minimum((i + 1) * hb, last), c))
    return pl.pallas_call(
        functools.partial(_sconv_kernel, halo=halo),
        grid=(b, s // ts),
        in_specs=[main(0), main(1), main(2), prev(1), prev(2), nxt(1), nxt(2),
                  pl.BlockSpec((3, MIX_W), lambda bi, i: (0, 0))],
        out_specs=pl.BlockSpec((1, ts, MIX_W), lambda bi, i: (bi, i, 0)),
        out_shape=jax.ShapeDtypeStruct((b, s, MIX_W), _BF),
        compiler_params=_params("parallel", "parallel"),
        name="short_conv",
    )(p3, p3, p3, p3, p3, p3, p3, conv_w)


def _tile(n, pref):
    return pref if n % pref == 0 else n


def kernel(x, mem, rel_bias, attn_w_in, attn_lambda, attn_subln, hgrn_w_in, hgrn_lower_bound,
           hgrn_norm, conv_w_in, conv_w, mem_w_kv, w_o, ln_gain, ln_bias, ffn_w_up, ffn_conv,
           ffn_w_down):
    b, s, d = x.shape
    n = b * s
    m = mem.shape[1]
    tm = _tile(s, ROW_TILE)
    hg_rows = _tile(s, HG_ROWS)

    bias, far = _bias_tiles(rel_bias, ATTN_Q_TILE, ATTN_ROW_CHUNK)
    mem2 = mem.reshape(b * m, d)
    lbw = jax.nn.softmax(hgrn_lower_bound.astype(_F32), axis=0)
    lb_all = jnp.cumsum(lbw, axis=0) - lbw[0]

    for layer in range(DEPTH):
        kind, j = layer % N_MIXERS, layer // N_MIXERS
        x2 = x.reshape(n, d)
        kv = _matmul(mem2, mem_w_kv[layer].astype(_BF), _BF, _tile(b * m, ROW_TILE))
        kv = kv.reshape(b, m, 2 * XATTN_W)
        if kind == 0:
            q_scale = jnp.where(jnp.arange(attn_w_in.shape[2]) < MIX_W, DA_HD ** -0.5 * LOG2E, 1.0)
            w_in = (attn_w_in[j] * q_scale).astype(_BF)
            p = _matmul(x2, w_in, _BF, tm).reshape(b, s, -1)
            lp = attn_lambda[j].astype(_F32)
            lam_init = 0.8 - 0.6 * math.exp(-0.3 * layer)
            lam = jnp.exp(jnp.sum(lp[0] * lp[1])) - jnp.exp(jnp.sum(lp[2] * lp[3])) + lam_init
            vt = jnp.swapaxes(p[..., 2 * MIX_W:3 * MIX_W], 1, 2).reshape(b, DA_HEADS, DA_W, s)
            vt = jnp.concatenate([vt, jnp.ones((b, DA_HEADS, SUBLANES_BF16, s), _BF)], axis=2)
            mixed = _diff_attention(p, vt, lam.reshape(1),
                                    attn_subln[j].astype(_F32).reshape(1, DA_W),
                                    bias, far, layer, ATTN_Q_TILE, ATTN_K_TILE)
            q_col = 3 * MIX_W // XATTN_W
        elif kind == 1:
            w_in = hgrn_w_in[j].astype(_BF)
            p = _matmul(x2, w_in, _F32, tm).reshape(b, s, -1)
            o_fw, o_bw = _hgrn_scan(p, lb_all[layer].reshape(2, 1, MIX_W), hg_rows)
            mixed = _hgrn_finish(o_fw.reshape(n, MIX_W), o_bw.reshape(n, MIX_W),
                                 p.reshape(n, -1), 2,
                                 hgrn_norm[j].astype(_F32).reshape(1, MIX_W), tm)
            q_col = 5 * MIX_W // XATTN_W
        else:
            w_in = conv_w_in[j].astype(_BF)
            p = _matmul(x2, w_in, _BF, tm).reshape(b, s, -1)
            mixed = _short_conv(p, conv_w[j].astype(_F32), tm)
            q_col = 3 * MIX_W // XATTN_W
        recalled = _memory_attention(p, kv, q_col, tm)
        x2 = _out_ln(x2, mixed.reshape(n, MIX_W), recalled.reshape(n, XATTN_W),
                     w_o[layer].astype(_BF), ln_gain[layer, 0].reshape(1, d),
                     ln_bias[layer, 0].reshape(1, d), tm)
        x = _conv_ffn_ln(x2.reshape(b, s, d), ffn_w_up[layer].astype(_BF),
                         ffn_conv[layer].astype(_F32), ffn_w_down[layer].astype(_BF),
                         ln_gain[layer, 1].reshape(1, d), ln_bias[layer, 1].reshape(1, d),
                         tm, FFN_TILE)
    return x
```

```python
import functools
import math

import jax
import jax.numpy as jnp
from jax import lax
from jax.experimental import pallas as pl
from jax.experimental.pallas import tpu as pltpu

D_MODEL = 1024
DEPTH = 4
N_MIXERS = 3
MIX_W = 3 * D_MODEL // 4
XATTN_HEADS = 4
XATTN_W = D_MODEL - MIX_W
XATTN_HD = XATTN_W // XATTN_HEADS
DA_HD = 64
DA_HEADS = MIX_W // (2 * DA_HD)
DA_W = 2 * DA_HD
N_BUCKETS = 32
MAX_DISTANCE = 128
HG_DK = 128
HG_HEADS = MIX_W // HG_DK
HG_CHUNK = 64
HG_SUB = 16
D_FF = ((8 * D_MODEL // 3 + 255) // 256) * 256
LN_EPS = 1e-5
ALPHA = (2 * DEPTH) ** 0.25
LOG2E = math.log2(math.e)

LANES = 128
SUBLANES_F32 = 8
SUBLANES_BF16 = 16
VMEM_LIMIT = 52 * 1024 * 1024

ROW_TILE = 512
ATTN_Q_TILE = 256
ATTN_K_TILE = 1024
ATTN_HEADS_PER_STEP = 1
ATTN_ROW_CHUNK = 256
MXU_WIDTH = 256
FFN_TILE = MXU_WIDTH
FFN_OUT_ROWS = 256
HG_ROWS = 512
HG_HEAD_GROUP = 6

_NEG = -1e30
_BF = jnp.bfloat16
_F32 = jnp.float32
_NT = (((1,), (1,)), ((), ()))


def _params(*sem):
    return pltpu.CompilerParams(dimension_semantics=sem, vmem_limit_bytes=VMEM_LIMIT)


def _sigmoid(z):
    e = jnp.exp(-jnp.abs(z))
    r = 1.0 / (1.0 + e)
    return jnp.where(z >= 0, r, e * r)


def _silu(z):
    return z * _sigmoid(z)


def _layer_norm(r, g, b):
    mu = jnp.mean(r, axis=-1, keepdims=True)
    rc = r - mu
    var = jnp.mean(rc * rc, axis=-1, keepdims=True)
    return rc * lax.rsqrt(var + LN_EPS) * g + b


def _mm_kernel(x_ref, w_ref, o_ref):
    o_ref[...] = jnp.dot(x_ref[...].astype(_BF), w_ref[...],
                         preferred_element_type=_F32).astype(o_ref.dtype)


def _matmul(x, w, out_dtype, tm):
    m, k = x.shape
    n = w.shape[1]
    return pl.pallas_call(
        _mm_kernel,
        grid=(m // tm,),
        in_specs=[pl.BlockSpec((tm, k), lambda i: (i, 0)),
                  pl.BlockSpec((k, n), lambda i: (0, 0), pipeline_mode=pl.Buffered(1))],
        out_specs=pl.BlockSpec((tm, n), lambda i: (i, 0)),
        out_shape=jax.ShapeDtypeStruct((m, n), out_dtype),
        compiler_params=_params("parallel"),
        name="proj",
    )(x, w)


def _memattn_kernel(q_ref, km_ref, vm_ref, o_ref):
    q = q_ref[0].astype(_BF)
    km = km_ref[0]
    vm = vm_ref[0]
    lane = lax.broadcasted_iota(jnp.int32, (1, XATTN_W), 1)
    acc = jnp.zeros(q.shape, _F32)
    for h in range(XATTN_HEADS):
        head = (lane >= h * XATTN_HD) & (lane < (h + 1) * XATTN_HD)
        qh = jnp.where(head, q, jnp.zeros_like(q))
        s = lax.dot_general(qh, km, _NT, preferred_element_type=_F32) * (XATTN_HD ** -0.5)
        p = jnp.exp(s - jnp.max(s, axis=-1, keepdims=True))
        l = jnp.sum(p, axis=-1, keepdims=True)
        vh = jnp.where(head, vm, jnp.zeros_like(vm))
        acc = acc + jnp.dot(p.astype(_BF), vh, preferred_element_type=_F32) / l
    o_ref[0] = acc.astype(o_ref.dtype)


def _memory_attention(p3, kv, q_col, tm):
    b, s, _ = p3.shape
    m = kv.shape[1]
    return pl.pallas_call(
        _memattn_kernel,
        grid=(b, s // tm),
        in_specs=[pl.BlockSpec((1, tm, XATTN_W), lambda bi, i: (bi, i, q_col)),
                  pl.BlockSpec((1, m, XATTN_W), lambda bi, i: (bi, 0, 0)),
                  pl.BlockSpec((1, m, XATTN_W), lambda bi, i: (bi, 0, 1))],
        out_specs=pl.BlockSpec((1, tm, XATTN_W), lambda bi, i: (bi, i, 0)),
        out_shape=jax.ShapeDtypeStruct((b, s, XATTN_W), _BF),
        compiler_params=_params("parallel", "parallel"),
        name="memattn",
    )(p3, kv, kv)


def _out_ln_kernel(x_ref, mix_ref, rec_ref, wo_ref, g_ref, b_ref, o_ref):
    y = jnp.dot(mix_ref[...], wo_ref[:MIX_W, :], preferred_element_type=_F32)
    y = y + jnp.dot(rec_ref[...], wo_ref[MIX_W:, :], preferred_element_type=_F32)
    o_ref[...] = _layer_norm(ALPHA * x_ref[...] + y, g_ref[...], b_ref[...])


def _out_ln(x2, mixed, recalled, wo, g, b, tm):
    n = x2.shape[0]
    return pl.pallas_call(
        _out_ln_kernel,
        grid=(n // tm,),
        in_specs=[pl.BlockSpec((tm, D_MODEL), lambda i: (i, 0)),
                  pl.BlockSpec((tm, MIX_W), lambda i: (i, 0)),
                  pl.BlockSpec((tm, XATTN_W), lambda i: (i, 0)),
                  pl.BlockSpec((D_MODEL, D_MODEL), lambda i: (0, 0)),
                  pl.BlockSpec((1, D_MODEL), lambda i: (0, 0)),
                  pl.BlockSpec((1, D_MODEL), lambda i: (0, 0))],
        out_specs=pl.BlockSpec((tm, D_MODEL), lambda i: (i, 0)),
        out_shape=jax.ShapeDtypeStruct((n, D_MODEL), _F32),
        compiler_params=_params("parallel"),
        name="out_ln",
    )(x2, mixed, recalled, wo, g, b)


def _dwconv_rows(h, w):
    n = h.shape[0]
    return (pltpu.roll(h, 1, 0) * w[0:1] + h * w[1:2] + pltpu.roll(h, n - 1, 0) * w[2:3])


def _ffn_kernel(x_ref, xp_ref, xn_ref, wu_ref, wc_ref, wd_ref, g_ref, b_ref, o_ref, gated_ref,
                *, ts, halo, fb):
    i = pl.program_id(1)
    prev = jnp.where(i == 0, 0.0, xp_ref[0])
    nxt = jnp.where(i == pl.num_programs(1) - 1, 0.0, xn_ref[0])
    xb = jnp.concatenate([prev, x_ref[0], nxt], axis=0).astype(_BF)
    for c in range(D_FF // fb):
        ca = slice(c * fb, (c + 1) * fb)
        cv = slice(D_FF + c * fb, D_FF + (c + 1) * fb)
        ha = jnp.dot(xb, wu_ref[:, ca], preferred_element_type=_F32)
        hv = jnp.dot(xb, wu_ref[:, cv], preferred_element_type=_F32)
        a = _dwconv_rows(ha, wc_ref[:, ca])[halo:halo + ts]
        v = _dwconv_rows(hv, wc_ref[:, cv])[halo:halo + ts]
        gated_ref[:, ca] = (_silu(a) * v).astype(_BF)
    for r0 in range(0, ts, FFN_OUT_ROWS):
        rows = slice(r0, r0 + FFN_OUT_ROWS)
        f = jnp.dot(gated_ref[rows, :], wd_ref[...], preferred_element_type=_F32)
        o_ref[0, rows, :] = _layer_norm(ALPHA * x_ref[0, rows, :] + f, g_ref[...], b_ref[...])


def _conv_ffn_ln(x3, w_up, w_conv, w_down, g, b, ts, fb):
    bsz, s, d = x3.shape
    halo = SUBLANES_BF16
    hb = ts // halo
    last = s // halo - 1
    kern = functools.partial(_ffn_kernel, ts=ts, halo=halo, fb=fb)
    resident = lambda shape: pl.BlockSpec(shape, lambda bi, i: (0, 0),
                                          pipeline_mode=pl.Buffered(1))
    return pl.pallas_call(
        kern,
        grid=(bsz, s // ts),
        in_specs=[pl.BlockSpec((1, ts, d), lambda bi, i: (bi, i, 0)),
                  pl.BlockSpec((1, halo, d), lambda bi, i: (bi, jnp.maximum(i * hb - 1, 0), 0)),
                  pl.BlockSpec((1, halo, d), lambda bi, i: (bi, jnp.minimum((i + 1) * hb, last), 0)),
                  resident((d, 2 * D_FF)),
                  resident((3, 2 * D_FF)),
                  resident((D_FF, d)),
                  pl.BlockSpec((1, d), lambda bi, i: (0, 0)),
                  pl.BlockSpec((1, d), lambda bi, i: (0, 0))],
        out_specs=pl.BlockSpec((1, ts, d), lambda bi, i: (bi, i, 0)),
        out_shape=jax.ShapeDtypeStruct((bsz, s, d), _F32),
        scratch_shapes=[pltpu.VMEM((ts, D_FF), _BF)],
        compiler_params=_params("parallel", "parallel"),
        name="conv_ffn",
    )(x3, x3, x3, w_up, w_conv, w_down, g, b)


def _fold_rows(x, op, group=SUBLANES_F32):
    acc = x[:group]
    for g in range(1, x.shape[0] // group):
        acc = op(acc, x[g * group:(g + 1) * group])
    while acc.shape[0] > SUBLANES_F32:
        half = acc.shape[0] // 2
        acc = op(acc[:half], acc[half:])
    return acc


def _attn_kernel(lam_ref, far_ref, q_ref, k_ref, vt_ref, bias_ref, subln_ref, o_ref,
                 s_a, s_b, acc_ref, *, tq, tk, out_scale):
    heads = q_ref.shape[2] // DA_W
    streams = [(hh, mi) for hh in range(heads) for mi in range(2)]
    n_str = len(streams)
    hp = pl.program_id(1)
    nkv = k_ref.shape[1] // tk
    nq = q_ref.shape[1] // tq
    n_chunks = tk // ATTN_ROW_CHUNK
    lane = lax.broadcasted_iota(jnp.int32, (1, heads * DA_W), 1)

    unit, u_min, u_max = _near_offsets(tq, ATTN_ROW_CHUNK)
    consts = [[far_ref[3 * (heads * hp + hh) + c] for c in range(3)] for hh in range(heads)]

    def stream_queries(i):
        q = q_ref[0, pl.ds(pl.multiple_of(i * tq, tq), tq), :]
        zero = jnp.zeros_like(q)
        return [jnp.where((lane >= hh * DA_W + mi * DA_HD) & (lane < hh * DA_W + (mi + 1) * DA_HD),
                          q, zero).T for hh, mi in streams]

    def side_select(i, j, r, left, mid, right):
        u = (tk * j + ATTN_ROW_CHUNK * r - tq * i) // unit
        return jnp.where(u < u_min, left, jnp.where(u > u_max, right, mid))

    def max_bound(i, j, r, hh):
        return side_select(i, j, r, consts[hh][0], consts[hh][2], consts[hh][1])

    def score_rows(qs, i, j, r, s_out, st):
        k0 = pl.multiple_of(j * tk + r * ATTN_ROW_CHUNK, ATTN_ROW_CHUNK)
        s = jnp.dot(k_ref[0, pl.ds(k0, ATTN_ROW_CHUNK), :], qs[st], preferred_element_type=_F32)
        s_out[st, r * ATTN_ROW_CHUNK:(r + 1) * ATTN_ROW_CHUNK, :] = s
        return _fold_rows(s, jnp.maximum) + max_bound(i, j, r, streams[st][0])

    def add_near_bias(i, j, s_out):
        u_first = (tk * j - tq * i) // unit
        u_last = (tk * j + ATTN_ROW_CHUNK * (n_chunks - 1) - tq * i) // unit

        @pl.when((u_last >= u_min) & (u_first <= u_max))
        def _():
            for r in range(n_chunks):
                u = (tk * j + ATTN_ROW_CHUNK * r - tq * i) // unit
                rows = slice(r * ATTN_ROW_CHUNK, (r + 1) * ATTN_ROW_CHUNK)

                @pl.when((u >= u_min) & (u <= u_max))
                def _():
                    for st, (hh, _) in enumerate(streams):
                        s_out[st, rows, :] = s_out[st, rows, :] + bias_ref[u - u_min, hh]

    def half_step(i, j, carry, s_cur, s_nxt, nxt):
        ni, nj, nqs = nxt
        tmax = [None] * n_str
        pv = [None] * n_str
        for r in range(n_chunks):
            rows = slice(r * ATTN_ROW_CHUNK, (r + 1) * ATTN_ROW_CHUNK)
            for st in range(n_str):
                t = score_rows(nqs, ni, nj, r, s_nxt, st)
                tmax[st] = t if r == 0 else jnp.maximum(tmax[st], t)
            v0 = pl.multiple_of(j * tk + r * ATTN_ROW_CHUNK, ATTN_ROW_CHUNK)
            for st, (hh, _) in enumerate(streams):
                shift = side_select(i, j, r, consts[hh][0], 0.0, consts[hh][1])
                p = jnp.exp2(s_cur[st, rows, :] - (carry[st][0] - shift)).astype(_BF)
                d = jnp.dot(vt_ref[0, hh, :, pl.ds(v0, ATTN_ROW_CHUNK)], p,
                            preferred_element_type=_F32)
                pv[st] = d if r == 0 else pv[st] + d
        new, tile_max = [], []
        for st in range(n_str):
            m, alpha = carry[st]
            acc_ref[st] = alpha * acc_ref[st] + pv[st]
            cand = jnp.max(tmax[st], axis=0, keepdims=True)
            m_nxt = jnp.maximum(m, cand)
            new.append((m_nxt, jnp.exp2(m - m_nxt)))
            tile_max.append(cand)
        add_near_bias(ni, nj, s_nxt)
        return tuple(new), tuple(tile_max)

    qs0 = stream_queries(0)
    first = []
    for st in range(n_str):
        parts = [score_rows(qs0, 0, 0, r, s_a, st) for r in range(n_chunks)]
        first.append(jnp.max(functools.reduce(jnp.maximum, parts), axis=0, keepdims=True))
    add_near_bias(0, 0, s_a)
    lam = lam_ref[0]

    def query_tile(i, m_first):
        qs = stream_queries(i)
        acc_ref[...] = jnp.zeros_like(acc_ref)
        carry = tuple((m_first[st], jnp.zeros((1, tq), _F32)) for st in range(n_str))

        def pair(jj, carry):
            j = 2 * jj
            carry, _ = half_step(i, j, carry, s_a, s_b, (i, j + 1, qs))
            carry, _ = half_step(i, j + 1, carry, s_b, s_a, (i, j + 2, qs))
            return carry

        carry = lax.fori_loop(0, nkv // 2 - 1, pair, carry)
        carry, _ = half_step(i, nkv - 2, carry, s_a, s_b, (i, nkv - 1, qs))
        i_next = jnp.minimum(i + 1, nq - 1)
        _, m_next = half_step(i, nkv - 1, carry, s_b, s_a, (i_next, 0, stream_queries(i_next)))

        q0 = pl.multiple_of(i * tq, tq)
        for hh in range(heads):
            o0 = acc_ref[2 * hh, :DA_W] / acc_ref[2 * hh, DA_W:DA_W + 1]
            o1 = acc_ref[2 * hh + 1, :DA_W] / acc_ref[2 * hh + 1, DA_W:DA_W + 1]
            o = (o0 - lam * o1).T
            ms = jnp.mean(o * o, axis=-1, keepdims=True)
            o_ref[0, pl.ds(q0, tq), hh * DA_W:(hh + 1) * DA_W] = (
                o * lax.rsqrt(ms + LN_EPS) * subln_ref[...] * out_scale).astype(o_ref.dtype)
        return m_next

    lax.fori_loop(0, nq, query_tile, tuple(first))


def _t5_bucket(rel):
    half = N_BUCKETS // 2
    max_exact = half // 2
    ret = jnp.where(rel > 0, half, 0)
    n = jnp.abs(rel)
    nf = jnp.maximum(n, 1).astype(_F32)
    large = max_exact + (jnp.log(nf / max_exact) / math.log(MAX_DISTANCE / max_exact)
                         * (half - max_exact)).astype(jnp.int32)
    large = jnp.minimum(large, half - 1)
    return ret + jnp.where(n < max_exact, n, large)


def _near_offsets(tq, tk):
    unit = math.gcd(tq, tk)
    first = -((tk + MAX_DISTANCE - 2) // unit)
    last = (tq + MAX_DISTANCE - 2) // unit
    return unit, first, last


def _bias_tiles(rel_bias, tq, tk):
    unit, u_min, u_max = _near_offsets(tq, tk)
    n_near = u_max - u_min + 1
    c = jnp.arange(tk, dtype=jnp.int32)[:, None]
    r = jnp.arange(tq, dtype=jnp.int32)[None, :]
    u = jnp.arange(u_min, u_max + 1, dtype=jnp.int32)[:, None, None]
    table = rel_bias.astype(_F32) * LOG2E
    bucket = _t5_bucket(u * unit + c - r)[:, None]
    tiles = jnp.zeros((n_near, DA_HEADS, tk, tq), _F32)
    for b in range(N_BUCKETS):
        tiles = jnp.where(bucket == b, table[b][None, :, None, None], tiles)
    far =table[_t5_bucket(jnp.array([-MAX_DISTANCE, MAX_DISTANCE], jnp.int32))]
    consts = jnp.concatenate([far, jnp.max(table, axis=0, keepdims=True)], axis=0)
    return tiles, consts.T.reshape(-1)


def _diff_attention(p3, vt, lam, subln, bias, far, layer, tq, tk):
    b, s, _ = p3.shape
    vrows = vt.shape[2]
    assert s % (2 * tk) == 0
    lam_init = 0.8 - 0.6 * math.exp(-0.3 * layer)
    kern = functools.partial(_attn_kernel, tq=tq, tk=tk, out_scale=1.0 - lam_init)
    nh = ATTN_HEADS_PER_STEP
    groups = DA_HEADS // nh
    return pl.pallas_call(
        kern,
        grid=(b, groups),
        in_specs=[pl.BlockSpec(memory_space=pltpu.SMEM),
                  pl.BlockSpec(memory_space=pltpu.SMEM),
                  pl.BlockSpec((1, s, nh * DA_W), lambda bi, h: (bi, 0, h)),
                  pl.BlockSpec((1, s, nh * DA_W), lambda bi, h: (bi, 0, groups + h)),
                  pl.BlockSpec((1, nh, vrows, s), lambda bi, h: (bi, h, 0, 0)),
                  pl.BlockSpec((bias.shape[0], nh) + bias.shape[2:], lambda bi, h: (0, h, 0, 0)),
                  pl.BlockSpec((1, DA_W), lambda bi, h: (0, 0))],
        out_specs=pl.BlockSpec((1, s, nh * DA_W), lambda bi, h: (bi, 0, h)),
        out_shape=jax.ShapeDtypeStruct((b, s, MIX_W), _BF),
        scratch_shapes=[pltpu.VMEM((2 * nh, tk, tq), _F32), pltpu.VMEM((2 * nh, tk, tq), _F32),
                        pltpu.VMEM((2 * nh, vrows, tq), _F32)],
        compiler_params=_params("parallel", "parallel"),
        name="diff_attn",
    )(lam, far, p3, p3, vt, bias, subln)


def _hgrn_chunk_stages(z, q, v, lb, st, tri, rev):
    c_rows = z.shape[0]
    e = jnp.exp(-jnp.abs(z))
    r = 1.0 / (1.0 + e)
    sig_pos, sig_neg = jnp.where(z >= 0, r, e * r), jnp.where(z >= 0, e * r, r)
    f = lb + (1.0 - lb) * sig_pos
    kk = (1.0 - lb) * sig_neg
    logf = jnp.log(f)
    hi = logf.astype(_BF)
    lo = (logf - hi.astype(_F32)).astype(_BF)
    cum = (jnp.dot(tri, hi, preferred_element_type=_F32)
           + jnp.dot(tri, lo, preferred_element_type=_F32))
    qh = _silu(q) * (HG_DK ** -0.5)
    vb = v.astype(_BF)
    yield None
    total = cum[0:1, :] if rev else cum[c_rows - 1:c_rows, :]
    o_inter = lax.dot_general((qh * jnp.exp(cum)).astype(_BF), st.astype(_BF), _NT,
                              preferred_element_type=_F32)
    ks = (kk * jnp.exp(total - cum)).astype(_BF)
    st_new = st * jnp.exp(total) + jnp.dot(v.T.astype(_BF), ks, preferred_element_type=_F32)
    blocks = []
    for b in range(c_rows // HG_SUB):
        lo_r, hi_r = b * HG_SUB, (b + 1) * HG_SUB
        if rev:
            cols = slice(lo_r, c_rows)
            base = cum[hi_r:hi_r + 1, :] if hi_r < c_rows else jnp.zeros_like(total)
        else:
            cols = slice(0, hi_r)
            base = cum[lo_r - 1:lo_r, :] if b else jnp.zeros_like(total)
        qq = (qh[lo_r:hi_r] * jnp.exp(cum[lo_r:hi_r] - base)).astype(_BF)
        kt = (kk[cols] * jnp.exp(base - cum[cols])).astype(_BF)
        a = lax.dot_general(qq, kt, _NT, preferred_element_type=_F32)
        blocks.append((lo_r, hi_r, cols, a))
    yield None
    parts = []
    for lo_r, hi_r, cols, a in blocks:
        sub_row = lax.broadcasted_iota(jnp.int32, a.shape, 0) + lo_r
        sub_col = lax.broadcasted_iota(jnp.int32, a.shape, 1) + cols.start
        seen = (sub_col >= sub_row) if rev else (sub_col <= sub_row)
        a = jnp.where(seen, a, 0.0)
        parts.append(o_inter[lo_r:hi_r]
                     + jnp.dot(a.astype(_BF), vb[cols], preferred_element_type=_F32))
    yield jnp.concatenate(parts, axis=0), st_new


def _hgrn_kernel(qf_ref, vf_ref, zf_ref, qb_ref, vb_ref, zb_ref, lb_ref, of_ref, ob_ref, st_ref,
                 *, rows):
    @pl.when(pl.program_id(1) == 0)
    def _():
        st_ref[...] = jnp.zeros_like(st_ref)

    c_rows = HG_CHUNK
    n_chunks = rows // c_rows
    row = lax.broadcasted_iota(jnp.int32, (c_rows, c_rows), 0)
    col = lax.broadcasted_iota(jnp.int32, (c_rows, c_rows), 1)
    tris = ((col <= row).astype(_BF), (col >= row).astype(_BF))
    dirs = ((qf_ref, vf_ref, zf_ref, of_ref), (qb_ref, vb_ref, zb_ref, ob_ref))

    def chunk(c, carry):
        starts = (pl.multiple_of(c * c_rows, c_rows),
                  pl.multiple_of((n_chunks - 1 - c) * c_rows, c_rows))
        for h0 in range(0, HG_HEADS, HG_HEAD_GROUP):
            chains = []
            for h in range(h0, h0 + HG_HEAD_GROUP):
                cs = slice(h * HG_DK, (h + 1) * HG_DK)
                for d, (q_ref, v_ref, z_ref, o_ref) in enumerate(dirs):
                    rs = pl.ds(starts[d], c_rows)
                    gen = _hgrn_chunk_stages(z_ref[0, rs, cs], q_ref[0, rs, cs], v_ref[0, rs, cs],
                                             lb_ref[d, :, cs], st_ref[d, h], tris[d], rev=bool(d))
                    chains.append((gen, o_ref, rs, cs, d, h))
            for _ in range(2):
                for chain in chains:
                    next(chain[0])
            for gen, o_ref, rs, cs, d, h in chains:
                o, st = next(gen)
                o_ref[0, rs, cs] = o
                st_ref[d, h] = st
        return carry

    lax.fori_loop(0, n_chunks, chunk, 0)


def _hgrn_scan(p3, lb, rows):
    b, s, _ = p3.shape
    w = MIX_W
    t = s // rows
    fwd = lambda c: pl.BlockSpec((1, rows, w), lambda bi, i: (bi, i, c))
    bwd = lambda c: pl.BlockSpec((1, rows, w), lambda bi, i: (bi, t - 1 - i, c))
    out = jax.ShapeDtypeStruct((b, s, w), _F32)
    return pl.pallas_call(
        functools.partial(_hgrn_kernel, rows=rows),
        grid=(b, t),
        in_specs=[fwd(0), fwd(1), fwd(3), bwd(0), bwd(1), bwd(4),
                  pl.BlockSpec((2, 1, w), lambda bi, i: (0, 0, 0))],
        out_specs=[fwd(0), bwd(0)],
        out_shape=[out, out],
        scratch_shapes=[pltpu.VMEM((2, HG_HEADS, HG_DK, HG_DK), _F32)],
        compiler_params=_params("parallel", "arbitrary"),
        name="hgrn_scan",
    )(p3, p3, p3, p3, p3, p3, lb)


def _hgrn_finish_kernel(of_ref, ob_ref, g_ref, w_ref, o_ref):
    o = of_ref[...] + ob_ref[...]
    gate = _silu(g_ref[...])
    w = w_ref[...]
    for h in range(HG_HEADS):
        cs = slice(h * HG_DK, (h + 1) * HG_DK)
        oh = o[:, cs]
        ms = jnp.mean(oh * oh, axis=-1, keepdims=True)
        o_ref[:, cs] = (oh * lax.rsqrt(ms + LN_EPS) * w[:, cs] * gate[:, cs]).astype(o_ref.dtype)


def _hgrn_finish(o_fw, o_bw, p2, g_col, norm_w, tm):
    n = o_fw.shape[0]
    return pl.pallas_call(
        _hgrn_finish_kernel,
        grid=(n // tm,),
        in_specs=[pl.BlockSpec((tm, MIX_W), lambda i: (i, 0)),
                  pl.BlockSpec((tm, MIX_W), lambda i: (i, 0)),
                  pl.BlockSpec((tm, MIX_W), lambda i: (i, g_col)),
                  pl.BlockSpec((1, MIX_W), lambda i: (0, 0))],
        out_specs=pl.BlockSpec((tm, MIX_W), lambda i: (i, 0)),
        out_shape=jax.ShapeDtypeStruct((n, MIX_W), _BF),
        compiler_params=_params("parallel"),
        name="hgrn_finish",
    )(o_fw, o_bw, p2, norm_w)


def _sconv_kernel(gb_ref, gc_ref, h_ref, gcp_ref, hp_ref, gcn_ref, hn_ref, w_ref, o_ref, *, halo):
    i = pl.program_id(1)
    u = gc_ref[0].astype(_F32) * h_ref[0].astype(_F32)
    n = u.shape[0]
    u_prev = (gcp_ref[0].astype(_F32) * hp_ref[0].astype(_F32))[halo - 1:halo]
    u_next = (gcn_ref[0].astype(_F32) * hn_ref[0].astype(_F32))[0:1]
    u_prev = jnp.where(i == 0, 0.0, u_prev)
    u_next = jnp.where(i == pl.num_programs(1) - 1, 0.0, u_next)
    row = lax.broadcasted_iota(jnp.int32, u.shape, 0)
    down = jnp.where(row == 0, u_prev, pltpu.roll(u, 1, 0))
    up = jnp.where(row == n - 1, u_next, pltpu.roll(u, n - 1, 0))
    w = w_ref[...]
    y = down * w[0:1] + u * w[1:2] + up * w[2:3]
    o_ref[0] = (gb_ref[0].astype(_F32) * y).astype(o_ref.dtype)


def _short_conv(p3, conv_w, ts):
    b, s, _ = p3.shape
    halo = SUBLANES_BF16
    hb = ts // halo
    last = s // halo - 1
    main = lambda c: pl.BlockSpec((1, ts, MIX_W), lambda bi, i: (bi, i, c))
    prev = lambda c: pl.BlockSpec((1, halo, MIX_W), lambda bi, i: (bi, jnp.maximum(i * hb - 1, 0), c))
    nxt = lambda c: pl.BlockSpec((1, halo, MIX_W), lambda bi, i: (bi, jnp.minimum((i + 1) * hb, last), c))
    return pl.pallas_call(
        functools.partial(_sconv_kernel, halo=halo),
        grid=(b, s // ts),
        in_specs=[main(0), main(1), main(2), prev(1), prev(2), nxt(1), nxt(2),
                  pl.BlockSpec((3, MIX_W), lambda bi, i: (0, 0))],
        out_specs=pl.BlockSpec((1, ts, MIX_W), lambda bi, i: (bi, i, 0)),
        out_shape=jax.ShapeDtypeStruct((b, s, MIX_W), _BF),
        compiler_params=_params("parallel", "parallel"),
        name="short_conv",
    )(p3, p3, p3, p3, p3, p3, p3, conv_w)


def _tile(n, pref):
    return pref if n % pref == 0 else n


def kernel(x, mem, rel_bias, attn_w_in, attn_lambda, attn_subln, hgrn_w_in, hgrn_lower_bound,
           hgrn_norm, conv_w_in, conv_w, mem_w_kv, w_o, ln_gain, ln_bias, ffn_w_up, ffn_conv,
           ffn_w_down):
    b, s, d = x.shape
    n = b * s
    m = mem.shape[1]
    tm = _tile(s, ROW_TILE)
    hg_rows = _tile(s, HG_ROWS)

    bias, far = _bias_tiles(rel_bias, ATTN_Q_TILE, ATTN_ROW_CHUNK)
    mem2 = mem.reshape(b * m, d)
    lbw = jax.nn.softmax(hgrn_lower_bound.astype(_F32), axis=0)
    lb_all = jnp.cumsum(lbw, axis=0) - lbw[0]

    for layer in range(DEPTH):
        kind, j = layer % N_MIXERS, layer // N_MIXERS
        x2 = x.reshape(n, d)
        kv = _matmul(mem2, mem_w_kv[layer].astype(_BF), _BF, _tile(b * m, ROW_TILE))
        kv = kv.reshape(b, m, 2 * XATTN_W)
        if kind == 0:
            q_scale = jnp.where(jnp.arange(attn_w_in.shape[2]) < MIX_W, DA_HD ** -0.5 * LOG2E, 1.0)
            w_in = (attn_w_in[j] * q_scale).astype(_BF)
            p = _matmul(x2, w_in, _BF, tm).reshape(b, s, -1)
            lp = attn_lambda[j].astype(_F32)
            lam_init = 0.8 - 0.6 * math.exp(-0.3 * layer)
            lam = jnp.exp(jnp.sum(lp[0] * lp[1])) - jnp.exp(jnp.sum(lp[2] * lp[3])) + lam_init
            vt = jnp.swapaxes(p[..., 2 * MIX_W:3 * MIX_W], 1, 2).reshape(b, DA_HEADS, DA_W, s)
            vt = jnp.concatenate([vt, jnp.ones((b, DA_HEADS, SUBLANES_BF16, s), _BF)], axis=2)
            mixed = _diff_attention(p, vt, lam.reshape(1),
                                    attn_subln[j].astype(_F32).reshape(1, DA_W),
                                    bias, far, layer, ATTN_Q_TILE, ATTN_K_TILE)
            q_col = 3 * MIX_W // XATTN_W
        elif kind == 1:
            w_in = hgrn_w_in[j].astype(_BF)
            p = _matmul(x2, w_in, _F32, tm).reshape(b, s, -1)
            o_fw, o_bw = _hgrn_scan(p, lb_all[layer].reshape(2, 1, MIX_W), hg_rows)
            mixed = _hgrn_finish(o_fw.reshape(n, MIX_W), o_bw.reshape(n, MIX_W),
                                 p.reshape(n, -1), 2,
                                 hgrn_norm[j].astype(_F32).reshape(1, MIX_W), tm)
            q_col = 5 * MIX_W // XATTN_W
        else:
            w_in = conv_w_in[j].astype(_BF)
            p = _matmul(x2, w_in, _BF, tm).reshape(b, s, -1)
            mixed = _short_conv(p, conv_w[j].astype(_F32), tm)
            q_col = 3 * MIX_W // XATTN_W
        recalled = _memory_attention(p, kv, q_col, tm)
        x2 = _out_ln(x2, mixed.reshape(n, MIX_W), recalled.reshape(n, XATTN_W),
                     w_o[layer].astype(_BF), ln_gain[layer, 0].reshape(1, d),
                     ln_bias[layer, 0].reshape(1, d), tm)
        x = _conv_ffn_ln(x2.reshape(b, s, d), ffn_w_up[layer].astype(_BF),
                         ffn_conv[layer].astype(_F32), ffn_w_down[layer].astype(_BF),
                         ln_gain[layer, 1].reshape(1, d), ln_bias[layer, 1].reshape(1, d),
                         tm, FFN_TILE)
    return x
```

```python
import functools
import math

import jax
import jax.numpy as jnp
from jax import lax
from jax.experimental import pallas as pl
from jax.experimental.pallas import tpu as pltpu

D_MODEL = 1024
DEPTH = 4
N_MIXERS = 3
MIX_W = 3 * D_MODEL // 4
XATTN_HEADS = 4
XATTN_W = D_MODEL - MIX_W
XATTN_HD = XATTN_W // XATTN_HEADS
DA_HD = 64
DA_HEADS = MIX_W // (2 * DA_HD)
DA_W = 2 * DA_HD
N_BUCKETS = 32
MAX_DISTANCE = 128
HG_DK = 128
HG_HEADS = MIX_W // HG_DK
HG_CHUNK = 64
HG_SUB = 16
D_FF = ((8 * D_MODEL // 3 + 255) // 256) * 256
LN_EPS = 1e-5
ALPHA = (2 * DEPTH) ** 0.25
LOG2E = math.log2(math.e)

LANES = 128
SUBLANES_F32 = 8
SUBLANES_BF16 = 16
VMEM_LIMIT = 52 * 1024 * 1024

ROW_TILE = 512
ATTN_Q_TILE = 256
ATTN_K_TILE = 2048
ATTN_HEADS_PER_STEP = 1
ATTN_ROW_CHUNK = 256
MXU_WIDTH = 256
FFN_TILE = MXU_WIDTH
FFN_OUT_ROWS = 256
HG_ROWS = 512
HG_HEAD_GROUP = 6

_NEG = -1e30
_BF = jnp.bfloat16
_F32 = jnp.float32
_NT = (((1,), (1,)), ((), ()))


def _params(*sem):
    return pltpu.CompilerParams(dimension_semantics=sem, vmem_limit_bytes=VMEM_LIMIT)


def _sigmoid(z):
    e = jnp.exp(-jnp.abs(z))
    r = 1.0 / (1.0 + e)
    return jnp.where(z >= 0, r, e * r)


def _silu(z):
    return z * _sigmoid(z)


def _layer_norm(r, g, b):
    mu = jnp.mean(r, axis=-1, keepdims=True)
    rc = r - mu
    var = jnp.mean(rc * rc, axis=-1, keepdims=True)
    return rc * lax.rsqrt(var + LN_EPS) * g + b


def _mm_kernel(x_ref, w_ref, o_ref):
    o_ref[...] = jnp.dot(x_ref[...].astype(_BF), w_ref[...],
                         preferred_element_type=_F32).astype(o_ref.dtype)


def _matmul(x, w, out_dtype, tm):
    m, k = x.shape
    n = w.shape[1]
    return pl.pallas_call(
        _mm_kernel,
        grid=(m // tm,),
        in_specs=[pl.BlockSpec((tm, k), lambda i: (i, 0)),
                  pl.BlockSpec((k, n), lambda i: (0, 0), pipeline_mode=pl.Buffered(1))],
        out_specs=pl.BlockSpec((tm, n), lambda i: (i, 0)),
        out_shape=jax.ShapeDtypeStruct((m, n), out_dtype),
        compiler_params=_params("parallel"),
        name="proj",
    )(x, w)


def _memattn_kernel(q_ref, km_ref, vm_ref, o_ref):
    q = q_ref[0].astype(_BF)
    km = km_ref[0]
    vm = vm_ref[0]
    lane = lax.broadcasted_iota(jnp.int32, (1, XATTN_W), 1)
    acc = jnp.zeros(q.shape, _F32)
    for h in range(XATTN_HEADS):
        head = (lane >= h * XATTN_HD) & (lane < (h + 1) * XATTN_HD)
        qh = jnp.where(head, q, jnp.zeros_like(q))
        s = lax.dot_general(qh, km, _NT, preferred_element_type=_F32) * (XATTN_HD ** -0.5)
        p = jnp.exp(s - jnp.max(s, axis=-1, keepdims=True))
        l = jnp.sum(p, axis=-1, keepdims=True)
        vh = jnp.where(head, vm, jnp.zeros_like(vm))
        acc = acc + jnp.dot(p.astype(_BF), vh, preferred_element_type=_F32) / l
    o_ref[0] = acc.astype(o_ref.dtype)


def _memory_attention(p3, kv, q_col, tm):
    b, s, _ = p3.shape
    m = kv.shape[1]
    return pl.pallas_call(
        _memattn_kernel,
        grid=(b, s // tm),
        in_specs=[pl.BlockSpec((1, tm, XATTN_W), lambda bi, i: (bi, i, q_col)),
                  pl.BlockSpec((1, m, XATTN_W), lambda bi, i: (bi, 0, 0)),
                  pl.BlockSpec((1, m, XATTN_W), lambda bi, i: (bi, 0, 1))],
        out_specs=pl.BlockSpec((1, tm, XATTN_W), lambda bi, i: (bi, i, 0)),
        out_shape=jax.ShapeDtypeStruct((b, s, XATTN_W), _BF),
        compiler_params=_params("parallel", "parallel"),
        name="memattn",
    )(p3, kv, kv)


def _out_ln_kernel(x_ref, mix_ref, rec_ref, wo_ref, g_ref, b_ref, o_ref):
    y = jnp.dot(mix_ref[...], wo_ref[:MIX_W, :], preferred_element_type=_F32)
    y = y + jnp.dot(rec_ref[...], wo_ref[MIX_W:, :], preferred_element_type=_F32)
    o_ref[...] = _layer_norm(ALPHA * x_ref[...] + y, g_ref[...], b_ref[...])


def _out_ln(x2, mixed, recalled, wo, g, b, tm):
    n = x2.shape[0]
    return pl.pallas_call(
        _out_ln_kernel,
        grid=(n // tm,),
        in_specs=[pl.BlockSpec((tm, D_MODEL), lambda i: (i, 0)),
                  pl.BlockSpec((tm, MIX_W), lambda i: (i, 0)),
                  pl.BlockSpec((tm, XATTN_W), lambda i: (i, 0)),
                  pl.BlockSpec((D_MODEL, D_MODEL), lambda i: (0, 0)),
                  pl.BlockSpec((1, D_MODEL), lambda i: (0, 0)),
                  pl.BlockSpec((1, D_MODEL), lambda i: (0, 0))],
        out_specs=pl.BlockSpec((tm, D_MODEL), lambda i: (i, 0)),
        out_shape=jax.ShapeDtypeStruct((n, D_MODEL), _F32),
        compiler_params=_params("parallel"),
        name="out_ln",
    )(x2, mixed, recalled, wo, g, b)


def _dwconv_rows(h, w):
    n = h.shape[0]
    return (pltpu.roll(h, 1, 0) * w[0:1] + h * w[1:2] + pltpu.roll(h, n - 1, 0) * w[2:3])


def _ffn_kernel(x_ref, xp_ref, xn_ref, wu_ref, wc_ref, wd_ref, g_ref, b_ref, o_ref, gated_ref,
                *, ts, halo, fb):
    i = pl.program_id(1)
    prev = jnp.where(i == 0, 0.0, xp_ref[0])
    nxt = jnp.where(i == pl.num_programs(1) - 1, 0.0, xn_ref[0])
    xb = jnp.concatenate([prev, x_ref[0], nxt], axis=0).astype(_BF)
    for c in range(D_FF // fb):
        ca = slice(c * fb, (c + 1) * fb)
        cv = slice(D_FF + c * fb, D_FF + (c + 1) * fb)
        ha = jnp.dot(xb, wu_ref[:, ca], preferred_element_type=_F32)
        hv = jnp.dot(xb, wu_ref[:, cv], preferred_element_type=_F32)
        a = _dwconv_rows(ha, wc_ref[:, ca])[halo:halo + ts]
        v = _dwconv_rows(hv, wc_ref[:, cv])[halo:halo + ts]
        gated_ref[:, ca] = (_silu(a) * v).astype(_BF)
    for r0 in range(0, ts, FFN_OUT_ROWS):
        rows = slice(r0, r0 + FFN_OUT_ROWS)
        f = jnp.dot(gated_ref[rows, :], wd_ref[...], preferred_element_type=_F32)
        o_ref[0, rows, :] = _layer_norm(ALPHA * x_ref[0, rows, :] + f, g_ref[...], b_ref[...])


def _conv_ffn_ln(x3, w_up, w_conv, w_down, g, b, ts, fb):
    bsz, s, d = x3.shape
    halo = SUBLANES_BF16
    hb = ts // halo
    last = s // halo - 1
    kern = functools.partial(_ffn_kernel, ts=ts, halo=halo, fb=fb)
    resident = lambda shape: pl.BlockSpec(shape, lambda bi, i: (0, 0),
                                          pipeline_mode=pl.Buffered(1))
    return pl.pallas_call(
        kern,
        grid=(bsz, s // ts),
        in_specs=[pl.BlockSpec((1, ts, d), lambda bi, i: (bi, i, 0)),
                  pl.BlockSpec((1, halo, d), lambda bi, i: (bi, jnp.maximum(i * hb - 1, 0), 0)),
                  pl.BlockSpec((1, halo, d), lambda bi, i: (bi, jnp.minimum((i + 1) * hb, last), 0)),
                  resident((d, 2 * D_FF)),
                  resident((3, 2 * D_FF)),
                  resident((D_FF, d)),
                  pl.BlockSpec((1, d), lambda bi, i: (0, 0)),
                  pl.BlockSpec((1, d), lambda bi, i: (0, 0))],
        out_specs=pl.BlockSpec((1, ts, d), lambda bi, i: (bi, i, 0)),
        out_shape=jax.ShapeDtypeStruct((bsz, s, d), _F32),
        scratch_shapes=[pltpu.VMEM((ts, D_FF), _BF)],
        compiler_params=_params("parallel", "parallel"),
        name="conv_ffn",
    )(x3, x3, x3, w_up, w_conv, w_down, g, b)


def _fold_rows(x, op, group=SUBLANES_F32):
    acc = x[:group]
    for g in range(1, x.shape[0] // group):
        acc = op(acc, x[g * group:(g + 1) * group])
    while acc.shape[0] > SUBLANES_F32:
        half = acc.shape[0] // 2
        acc = op(acc[:half], acc[half:])
    return acc


def _attn_kernel(lam_ref, far_ref, q_ref, k_ref, vt_ref, bias_ref, subln_ref, o_ref,
                 s_a, s_b, acc_ref, *, tq, tk, out_scale):
    heads = q_ref.shape[2] // DA_W
    streams = [(hh, mi) for hh in range(heads) for mi in range(2)]
    n_str = len(streams)
    hp = pl.program_id(1)
    nkv = k_ref.shape[1] // tk
    nq = q_ref.shape[1] // tq
    n_chunks = tk // ATTN_ROW_CHUNK
    lane = lax.broadcasted_iota(jnp.int32, (1, heads * DA_W), 1)

    unit, u_min, u_max = _near_offsets(tq, ATTN_ROW_CHUNK)
    consts = [[far_ref[3 * (heads * hp + hh) + c] for c in range(3)] for hh in range(heads)]

    def stream_queries(i):
        q = q_ref[0, pl.ds(pl.multiple_of(i * tq, tq), tq), :]
        zero = jnp.zeros_like(q)
        return [jnp.where((lane >= hh * DA_W + mi * DA_HD) & (lane < hh * DA_W + (mi + 1) * DA_HD),
                          q, zero).T for hh, mi in streams]

    def side_select(i, j, r, left, mid, right):
        u = (tk * j + ATTN_ROW_CHUNK * r - tq * i) // unit
        return jnp.where(u < u_min, left, jnp.where(u > u_max, right, mid))

    def max_bound(i, j, r, hh):
        return side_select(i, j, r, consts[hh][0], consts[hh][2], consts[hh][1])

    def score_rows(qs, i, j, r, s_out, st):
        k0 = pl.multiple_of(j * tk + r * ATTN_ROW_CHUNK, ATTN_ROW_CHUNK)
        s = jnp.dot(k_ref[0, pl.ds(k0, ATTN_ROW_CHUNK), :], qs[st], preferred_element_type=_F32)
        s_out[st, r * ATTN_ROW_CHUNK:(r + 1) * ATTN_ROW_CHUNK, :] = s
        return _fold_rows(s, jnp.maximum) + max_bound(i, j, r, streams[st][0])

    def add_near_bias(i, j, s_out):
        u_first = (tk * j - tq * i) // unit
        u_last = (tk * j + ATTN_ROW_CHUNK * (n_chunks - 1) - tq * i) // unit

        @pl.when((u_last >= u_min) & (u_first <= u_max))
        def _():
            for r in range(n_chunks):
                u = (tk * j + ATTN_ROW_CHUNK * r - tq * i) // unit
                rows = slice(r * ATTN_ROW_CHUNK, (r + 1) * ATTN_ROW_CHUNK)

                @pl.when((u >= u_min) & (u <= u_max))
                def _():
                    for st, (hh, _) in enumerate(streams):
                        s_out[st, rows, :] = s_out[st, rows, :] + bias_ref[u - u_min, hh]

    def half_step(i, j, carry, s_cur, s_nxt, nxt):
        ni, nj, nqs = nxt
        tmax = [None] * n_str
        pv = [None] * n_str
        for r in range(n_chunks):
            rows = slice(r * ATTN_ROW_CHUNK, (r + 1) * ATTN_ROW_CHUNK)
            for st in range(n_str):
                t = score_rows(nqs, ni, nj, r, s_nxt, st)
                tmax[st] = t if r == 0 else jnp.maximum(tmax[st], t)
            v0 = pl.multiple_of(j * tk + r * ATTN_ROW_CHUNK, ATTN_ROW_CHUNK)
            for st, (hh, _) in enumerate(streams):
                shift = side_select(i, j, r, consts[hh][0], 0.0, consts[hh][1])
                p = jnp.exp2(s_cur[st, rows, :] - (carry[st][0] - shift)).astype(_BF)
                d = jnp.dot(vt_ref[0, hh, :, pl.ds(v0, ATTN_ROW_CHUNK)], p,
                            preferred_element_type=_F32)
                pv[st] = d if r == 0 else pv[st] + d
        new, tile_max = [], []
        for st in range(n_str):
            m, alpha = carry[st]
            acc_ref[st] = alpha * acc_ref[st] + pv[st]
            cand = jnp.max(tmax[st], axis=0, keepdims=True)
            m_nxt = jnp.maximum(m, cand)
            new.append((m_nxt, jnp.exp2(m - m_nxt)))
            tile_max.append(cand)
        add_near_bias(ni, nj, s_nxt)
        return tuple(new), tuple(tile_max)

    qs0 = stream_queries(0)
    first = []
    for st in range(n_str):
        parts = [score_rows(qs0, 0, 0, r, s_a, st) for r in range(n_chunks)]
        first.append(jnp.max(functools.reduce(jnp.maximum, parts), axis=0, keepdims=True))
    add_near_bias(0, 0, s_a)
    lam = lam_ref[0]

    def query_tile(i, m_first):
        qs = stream_queries(i)
        acc_ref[...] = jnp.zeros_like(acc_ref)
        carry = tuple((m_first[st], jnp.zeros((1, tq), _F32)) for st in range(n_str))

        def pair(jj, carry):
            j = 2 * jj
            carry, _ = half_step(i, j, carry, s_a, s_b, (i, j + 1, qs))
            carry, _ = half_step(i, j + 1, carry, s_b, s_a, (i, j + 2, qs))
            return carry

        carry = lax.fori_loop(0, nkv // 2 - 1, pair, carry)
        carry, _ = half_step(i, nkv - 2, carry, s_a, s_b, (i, nkv - 1, qs))
        i_next = jnp.minimum(i + 1, nq - 1)
        _, m_next = half_step(i, nkv - 1, carry, s_b, s_a, (i_next, 0, stream_queries(i_next)))

        q0 = pl.multiple_of(i * tq, tq)
        for hh in range(heads):
            o0 = acc_ref[2 * hh, :DA_W] / acc_ref[2 * hh, DA_W:DA_W + 1]
            o1 = acc_ref[2 * hh + 1, :DA_W] / acc_ref[2 * hh + 1, DA_W:DA_W + 1]
            o = (o0 - lam * o1).T
            ms = jnp.mean(o * o, axis=-1, keepdims=True)
            o_ref[0, pl.ds(q0, tq), hh * DA_W:(hh + 1) * DA_W] = (
                o * lax.rsqrt(ms + LN_EPS) * subln_ref[...] * out_scale).astype(o_ref.dtype)
        return m_next

    lax.fori_loop(0, nq, query_tile, tuple(first))


def _t5_bucket(rel):
    half = N_BUCKETS // 2
    max_exact = half // 2
    ret = jnp.where(rel > 0, half, 0)
    n = jnp.abs(rel)
    nf = jnp.maximum(n, 1).astype(_F32)
    large = max_exact + (jnp.log(nf / max_exact) / math.log(MAX_DISTANCE / max_exact)
                         * (half - max_exact)).astype(jnp.int32)
    large = jnp.minimum(large, half - 1)
    return ret + jnp.where(n < max_exact, n, large)


def _near_offsets(tq, tk):
    unit = math.gcd(tq, tk)
    first = -((tk + MAX_DISTANCE - 2) // unit)
    last = (tq + MAX_DISTANCE - 2) // unit
    return unit, first, last


def _bias_tiles(rel_bias, tq, tk):
    unit, u_min, u_max = _near_offsets(tq, tk)
    n_near = u_max - u_min + 1
    c = jnp.arange(tk, dtype=jnp.int32)[:, None]
    r = jnp.arange(tq, dtype=jnp.int32)[None, :]
    u = jnp.arange(u_min, u_max + 1, dtype=jnp.int32)[:, None, None]
    table = rel_bias.astype(_F32) * LOG2E
    bucket = _t5_bucket(u * unit + c - r)[:, None]
    tiles = jnp.zeros((n_near, DA_HEADS, tk, tq), _F32)
    for b in range(N_BUCKETS):
        tiles = jnp.where(bucket == b, table[b][None, :, None, None], tiles)
    far =table[_t5_bucket(jnp.array([-MAX_DISTANCE, MAX_DISTANCE], jnp.int32))]
    consts = jnp.concatenate([far, jnp.max(table, axis=0, keepdims=True)], axis=0)
    return tiles, consts.T.reshape(-1)


def _diff_attention(p3, vt, lam, subln, bias, far, layer, tq, tk):
    b, s, _ = p3.shape
    vrows = vt.shape[2]
    assert s % (2 * tk) == 0
    lam_init = 0.8 - 0.6 * math.exp(-0.3 * layer)
    kern = functools.partial(_attn_kernel, tq=tq, tk=tk, out_scale=1.0 - lam_init)
    nh = ATTN_HEADS_PER_STEP
    groups = DA_HEADS // nh
    return pl.pallas_call(
        kern,
        grid=(b, groups),
        in_specs=[pl.BlockSpec(memory_space=pltpu.SMEM),
                  pl.BlockSpec(memory_space=pltpu.SMEM),
                  pl.BlockSpec((1, s, nh * DA_W), lambda bi, h: (bi, 0, h)),
                  pl.BlockSpec((1, s, nh * DA_W), lambda bi, h: (bi, 0, groups + h)),
                  pl.BlockSpec((1, nh, vrows, s), lambda bi, h: (bi, h, 0, 0)),
                  pl.BlockSpec((bias.shape[0], nh) + bias.shape[2:], lambda bi, h: (0, h, 0, 0)),
                  pl.BlockSpec((1, DA_W), lambda bi, h: (0, 0))],
        out_specs=pl.BlockSpec((1, s, nh * DA_W), lambda bi, h: (bi, 0, h)),
        out_shape=jax.ShapeDtypeStruct((b, s, MIX_W), _BF),
        scratch_shapes=[pltpu.VMEM((2 * nh, tk, tq), _F32), pltpu.VMEM((2 * nh, tk, tq), _F32),
                        pltpu.VMEM((2 * nh, vrows, tq), _F32)],
        compiler_params=_params("parallel", "parallel"),
        name="diff_attn",
    )(lam, far, p3, p3, vt, bias, subln)


def _hgrn_chunk_stages(z, q, v, lb, st, tri, rev):
    c_rows = z.shape[0]
    e = jnp.exp(-jnp.abs(z))
    r = 1.0 / (1.0 + e)
    sig_pos, sig_neg = jnp.where(z >= 0, r, e * r), jnp.where(z >= 0, e * r, r)
    f = lb + (1.0 - lb) * sig_pos
    kk = (1.0 - lb) * sig_neg
    logf = jnp.log(f)
    hi = logf.astype(_BF)
    lo = (logf - hi.astype(_F32)).astype(_BF)
    cum = (jnp.dot(tri, hi, preferred_element_type=_F32)
           + jnp.dot(tri, lo, preferred_element_type=_F32))
    qh = _silu(q) * (HG_DK ** -0.5)
    vb = v.astype(_BF)
    yield None
    total = cum[0:1, :] if rev else cum[c_rows - 1:c_rows, :]
    o_inter = lax.dot_general((qh * jnp.exp(cum)).astype(_BF), st.astype(_BF), _NT,
                              preferred_element_type=_F32)
    ks = (kk * jnp.exp(total - cum)).astype(_BF)
    st_new = st * jnp.exp(total) + jnp.dot(v.T.astype(_BF), ks, preferred_element_type=_F32)
    blocks = []
    for b in range(c_rows // HG_SUB):
        lo_r, hi_r = b * HG_SUB, (b + 1) * HG_SUB
        if rev:
            cols = slice(lo_r, c_rows)
            base = cum[hi_r:hi_r + 1, :] if hi_r < c_rows else jnp.zeros_like(total)
        else:
            cols = slice(0, hi_r)
            base = cum[lo_r - 1:lo_r, :] if b else jnp.zeros_like(total)
        qq = (qh[lo_r:hi_r] * jnp.exp(cum[lo_r:hi_r] - base)).astype(_BF)
        kt = (kk[cols] * jnp.exp(base - cum[cols])).astype(_BF)
        a = lax.dot_general(qq, kt, _NT, preferred_element_type=_F32)
        blocks.append((lo_r, hi_r, cols, a))
    yield None
    parts = []
    for lo_r, hi_r, cols, a in blocks:
        sub_row = lax.broadcasted_iota(jnp.int32, a.shape, 0) + lo_r
        sub_col = lax.broadcasted_iota(jnp.int32, a.shape, 1) + cols.start
        seen = (sub_col >= sub_row) if rev else (sub_col <= sub_row)
        a = jnp.where(seen, a, 0.0)
        parts.append(o_inter[lo_r:hi_r]
                     + jnp.dot(a.astype(_BF), vb[cols], preferred_element_type=_F32))
    yield jnp.concatenate(parts, axis=0), st_new


def _hgrn_kernel(qf_ref, vf_ref, zf_ref, qb_ref, vb_ref, zb_ref, lb_ref, of_ref, ob_ref, st_ref,
                 *, rows):
    @pl.when(pl.program_id(1) == 0)
    def _():
        st_ref[...] = jnp.zeros_like(st_ref)

    c_rows = HG_CHUNK
    n_chunks = rows // c_rows
    row = lax.broadcasted_iota(jnp.int32, (c_rows, c_rows), 0)
    col = lax.broadcasted_iota(jnp.int32, (c_rows, c_rows), 1)
    tris = ((col <= row).astype(_BF), (col >= row).astype(_BF))
    dirs = ((qf_ref, vf_ref, zf_ref, of_ref), (qb_ref, vb_ref, zb_ref, ob_ref))

    def chunk(c, carry):
        starts = (pl.multiple_of(c * c_rows, c_rows),
                  pl.multiple_of((n_chunks - 1 - c) * c_rows, c_rows))
        for h0 in range(0, HG_HEADS, HG_HEAD_GROUP):
            chains = []
            for h in range(h0, h0 + HG_HEAD_GROUP):
                cs = slice(h * HG_DK, (h + 1) * HG_DK)
                for d, (q_ref, v_ref, z_ref, o_ref) in enumerate(dirs):
                    rs = pl.ds(starts[d], c_rows)
                    gen = _hgrn_chunk_stages(z_ref[0, rs, cs], q_ref[0, rs, cs], v_ref[0, rs, cs],
                                             lb_ref[d, :, cs], st_ref[d, h], tris[d], rev=bool(d))
                    chains.append((gen, o_ref, rs, cs, d, h))
            for _ in range(2):
                for chain in chains:
                    next(chain[0])
            for gen, o_ref, rs, cs, d, h in chains:
                o, st = next(gen)
                o_ref[0, rs, cs] = o
                st_ref[d, h] = st
        return carry

    lax.fori_loop(0, n_chunks, chunk, 0)


def _hgrn_scan(p3, lb, rows):
    b, s, _ = p3.shape
    w = MIX_W
    t = s // rows
    fwd = lambda c: pl.BlockSpec((1, rows, w), lambda bi, i: (bi, i, c))
    bwd = lambda c: pl.BlockSpec((1, rows, w), lambda bi, i: (bi, t - 1 - i, c))
    out = jax.ShapeDtypeStruct((b, s, w), _F32)
    return pl.pallas_call(
        functools.partial(_hgrn_kernel, rows=rows),
        grid=(b, t),
        in_specs=[fwd(0), fwd(1), fwd(3), bwd(0), bwd(1), bwd(4),
                  pl.BlockSpec((2, 1, w), lambda bi, i: (0, 0, 0))],
        out_specs=[fwd(0), bwd(0)],
        out_shape=[out, out],
        scratch_shapes=[pltpu.VMEM((2, HG_HEADS, HG_DK, HG_DK), _F32)],
        compiler_params=_params("parallel", "arbitrary"),
        name="hgrn_scan",
    )(p3, p3, p3, p3, p3, p3, lb)


def _hgrn_finish_kernel(of_ref, ob_ref, g_ref, w_ref, o_ref):
    o = of_ref[...] + ob_ref[...]
    gate = _silu(g_ref[...])
    w = w_ref[...]
    for h in range(HG_HEADS):
        cs = slice(h * HG_DK, (h + 1) * HG_DK)
        oh = o[:, cs]
        ms = jnp.mean(oh * oh, axis=-1, keepdims=True)
        o_ref[:, cs] = (oh * lax.rsqrt(ms + LN_EPS) * w[:, cs] * gate[:, cs]).astype(o_ref.dtype)


def _hgrn_finish(o_fw, o_bw, p2, g_col, norm_w, tm):
    n = o_fw.shape[0]
    return pl.pallas_call(
        _hgrn_finish_kernel,
        grid=(n // tm,),
        in_specs=[pl.BlockSpec((tm, MIX_W), lambda i: (i, 0)),
                  pl.BlockSpec((tm, MIX_W), lambda i: (i, 0)),
                  pl.BlockSpec((tm, MIX_W), lambda i: (i, g_col)),
                  pl.BlockSpec((1, MIX_W), lambda i: (0, 0))],
        out_specs=pl.BlockSpec((tm, MIX_W), lambda i: (i, 0)),
        out_shape=jax.ShapeDtypeStruct((n, MIX_W), _BF),
        compiler_params=_params("parallel"),
        name="hgrn_finish",
    )(o_fw, o_bw, p2, norm_w)


def _sconv_kernel(gb_ref, gc_ref, h_ref, gcp_ref, hp_ref, gcn_ref, hn_ref, w_ref, o_ref, *, halo):
    i = pl.program_id(1)
    u = gc_ref[0].astype(_F32) * h_ref[0].astype(_F32)
    n = u.shape[0]
    u_prev = (gcp_ref[0].astype(_F32) * hp_ref[0].astype(_F32))[halo - 1:halo]
    u_next = (gcn_ref[0].astype(_F32) * hn_ref[0].astype(_F32))[0:1]
    u_prev = jnp.where(i == 0, 0.0, u_prev)
    u_next = jnp.where(i == pl.num_programs(1) - 1, 0.0, u_next)
    row = lax.broadcasted_iota(jnp.int32, u.shape, 0)
    down = jnp.where(row == 0, u_prev, pltpu.roll(u, 1, 0))
    up = jnp.where(row == n - 1, u_next, pltpu.roll(u, n - 1, 0))
    w = w_ref[...]
    y = down * w[0:1] + u * w[1:2] + up * w[2:3]
    o_ref[0] = (gb_ref[0].astype(_F32) * y).astype(o_ref.dtype)


def _short_conv(p3, conv_w, ts):
    b, s, _ = p3.shape
    halo = SUBLANES_BF16
    hb = ts // halo
    last = s // halo - 1
    main = lambda c: pl.BlockSpec((1, ts, MIX_W), lambda bi, i: (bi, i, c))
    prev = lambda c: pl.BlockSpec((1, halo, MIX_W), lambda bi, i: (bi, jnp.maximum(i * hb - 1, 0), c))
    nxt = lambda c: pl.BlockSpec((1, halo, MIX_W), lambda bi, i: (bi, jnp.minimum((i + 1) * hb, last), c))
    return pl.pallas_call(
        functools.partial(_sconv_kernel, halo=halo),
        grid=(b, s // ts),
        in_specs=[main(0), main(1), main(2), prev(1), prev(2), nxt(1), nxt(2),
                  pl.BlockSpec((3, MIX_W), lambda bi, i: (0, 0))],
        out_specs=pl.BlockSpec((1, ts, MIX_W), lambda bi, i: (bi, i, 0)),
        out_shape=jax.ShapeDtypeStruct((b, s, MIX_W), _BF),
        compiler_params=_params("parallel", "parallel"),
        name="short_conv",
    )(p3, p3, p3, p3, p3, p3, p3, conv_w)


def _tile(n, pref):
    return pref if n % pref == 0 else n


def kernel(x, mem, rel_bias, attn_w_in, attn_lambda, attn_subln, hgrn_w_in, hgrn_lower_bound,
           hgrn_norm, conv_w_in, conv_w, mem_w_kv, w_o, ln_gain, ln_bias, ffn_w_up, ffn_conv,
           ffn_w_down):
    b, s, d = x.shape
    n = b * s
    m = mem.shape[1]
    tm = _tile(s, ROW_TILE)
    hg_rows = _tile(s, HG_ROWS)

    bias, far = _bias_tiles(rel_bias, ATTN_Q_TILE, ATTN_ROW_CHUNK)
    mem2 = mem.reshape(b * m, d)
    lbw = jax.nn.softmax(hgrn_lower_bound.astype(_F32), axis=0)
    lb_all = jnp.cumsum(lbw, axis=0) - lbw[0]

    for layer in range(DEPTH):
        kind, j = layer % N_MIXERS, layer // N_MIXERS
        x2 = x.reshape(n, d)
        kv = _matmul(mem2, mem_w_kv[layer].astype(_BF), _BF, _tile(b * m, ROW_TILE))
        kv = kv.reshape(b, m, 2 * XATTN_W)
        if kind == 0:
            q_scale = jnp.where(jnp.arange(attn_w_in.shape[2]) < MIX_W, DA_HD ** -0.5 * LOG2E, 1.0)
            w_in = (attn_w_in[j] * q_scale).astype(_BF)
            p = _matmul(x2, w_in, _BF, tm).reshape(b, s, -1)
            lp = attn_lambda[j].astype(_F32)
            lam_init = 0.8 - 0.6 * math.exp(-0.3 * layer)
            lam = jnp.exp(jnp.sum(lp[0] * lp[1])) - jnp.exp(jnp.sum(lp[2] * lp[3])) + lam_init
            vt = jnp.swapaxes(p[..., 2 * MIX_W:3 * MIX_W], 1, 2).reshape(b, DA_HEADS, DA_W, s)
            vt = jnp.concatenate([vt, jnp.ones((b, DA_HEADS, SUBLANES_BF16, s), _BF)], axis=2)
            mixed = _diff_attention(p, vt, lam.reshape(1),
                                    attn_subln[j].astype(_F32).reshape(1, DA_W),
                                    bias, far, layer, ATTN_Q_TILE, ATTN_K_TILE)
            q_col = 3 * MIX_W // XATTN_W
        elif kind == 1:
            w_in = hgrn_w_in[j].astype(_BF)
            p = _matmul(x2, w_in, _F32, tm).reshape(b, s, -1)
            o_fw, o_bw = _hgrn_scan(p, lb_all[layer].reshape(2, 1, MIX_W), hg_rows)
            mixed = _hgrn_finish(o_fw.reshape(n, MIX_W), o_bw.reshape(n, MIX_W),
                                 p.reshape(n, -1), 2,
                                 hgrn_norm[j].astype(_F32).reshape(1, MIX_W), tm)
            q_col = 5 * MIX_W // XATTN_W
        else:
            w_in = conv_w_in[j].astype(_BF)
            p = _matmul(x2, w_in, _BF, tm).reshape(b, s, -1)
            mixed = _short_conv(p, conv_w[j].astype(_F32), tm)
            q_col = 3 * MIX_W // XATTN_W
        recalled = _memory_attention(p, kv, q_col, tm)
        x2 = _out_ln(x2, mixed.reshape(n, MIX_W), recalled.reshape(n, XATTN_W),
                     w_o[layer].astype(_BF), ln_gain[layer, 0].reshape(1, d),
                     ln_bias[layer, 0].reshape(1, d), tm)
        x = _conv_ffn_ln(x2.reshape(b, s, d), ffn_w_up[layer].astype(_BF),
                         ffn_conv[layer].astype(_F32), ffn_w_down[layer].astype(_BF),
                         ln_gain[layer, 1].reshape(1, d), ln_bias[layer, 1].reshape(1, d),
                         tm, FFN_TILE)
    return x
```

```python
import functools
import math

import jax
import jax.numpy as jnp
from jax import lax
from jax.experimental import pallas as pl
from jax.experimental.pallas import tpu as pltpu

D_MODEL = 1024
DEPTH = 4
N_MIXERS = 3
MIX_W = 3 * D_MODEL // 4
XATTN_HEADS = 4
XATTN_W = D_MODEL - MIX_W
XATTN_HD = XATTN_W // XATTN_HEADS
DA_HD = 64
DA_HEADS = MIX_W // (2 * DA_HD)
DA_W = 2 * DA_HD
N_BUCKETS = 32
MAX_DISTANCE = 128
HG_DK = 128
HG_HEADS = MIX_W // HG_DK
HG_CHUNK = 64
HG_SUB = 16
D_FF = ((8 * D_MODEL // 3 + 255) // 256) * 256
LN_EPS = 1e-5
ALPHA = (2 * DEPTH) ** 0.25
LOG2E = math.log2(math.e)

LANES = 128
SUBLANES_F32 = 8
SUBLANES_BF16 = 16
VMEM_LIMIT = 52 * 1024 * 1024

ROW_TILE = 512
ATTN_Q_TILE = 256
ATTN_K_TILE = 4096
ATTN_HEADS_PER_STEP = 1
ATTN_ROW_CHUNK = 256
MXU_WIDTH = 256
FFN_TILE = MXU_WIDTH
FFN_OUT_ROWS = 256
HG_ROWS = 512
HG_HEAD_GROUP = 6

_NEG = -1e30
_BF = jnp.bfloat16
_F32 = jnp.float32
_NT = (((1,), (1,)), ((), ()))


def _params(*sem):
    return pltpu.CompilerParams(dimension_semantics=sem, vmem_limit_bytes=VMEM_LIMIT)


def _sigmoid(z):
    e = jnp.exp(-jnp.abs(z))
    r = 1.0 / (1.0 + e)
    return jnp.where(z >= 0, r, e * r)


def _silu(z):
    return z * _sigmoid(z)


def _layer_norm(r, g, b):
    mu = jnp.mean(r, axis=-1, keepdims=True)
    rc = r - mu
    var = jnp.mean(rc * rc, axis=-1, keepdims=True)
    return rc * lax.rsqrt(var + LN_EPS) * g + b


def _mm_kernel(x_ref, w_ref, o_ref):
    o_ref[...] = jnp.dot(x_ref[...].astype(_BF), w_ref[...],
                         preferred_element_type=_F32).astype(o_ref.dtype)


def _matmul(x, w, out_dtype, tm):
    m, k = x.shape
    n = w.shape[1]
    return pl.pallas_call(
        _mm_kernel,
        grid=(m // tm,),
        in_specs=[pl.BlockSpec((tm, k), lambda i: (i, 0)),
                  pl.BlockSpec((k, n), lambda i: (0, 0), pipeline_mode=pl.Buffered(1))],
        out_specs=pl.BlockSpec((tm, n), lambda i: (i, 0)),
        out_shape=jax.ShapeDtypeStruct((m, n), out_dtype),
        compiler_params=_params("parallel"),
        name="proj",
    )(x, w)


def _memattn_kernel(q_ref, km_ref, vm_ref, o_ref):
    q = q_ref[0].astype(_BF)
    km = km_ref[0]
    vm = vm_ref[0]
    lane = lax.broadcasted_iota(jnp.int32, (1, XATTN_W), 1)
    acc = jnp.zeros(q.shape, _F32)
    for h in range(XATTN_HEADS):
        head = (lane >= h * XATTN_HD) & (lane < (h + 1) * XATTN_HD)
        qh = jnp.where(head, q, jnp.zeros_like(q))
        s = lax.dot_general(qh, km, _NT, preferred_element_type=_F32) * (XATTN_HD ** -0.5)
        p = jnp.exp(s - jnp.max(s, axis=-1, keepdims=True))
        l = jnp.sum(p, axis=-1, keepdims=True)
        vh = jnp.where(head, vm, jnp.zeros_like(vm))
        acc = acc + jnp.dot(p.astype(_BF), vh, preferred_element_type=_F32) / l
    o_ref[0] = acc.astype(o_ref.dtype)


def _memory_attention(p3, kv, q_col, tm):
    b, s, _ = p3.shape
    m = kv.shape[1]
    return pl.pallas_call(
        _memattn_kernel,
        grid=(b, s // tm),
        in_specs=[pl.BlockSpec((1, tm, XATTN_W), lambda bi, i: (bi, i, q_col)),
                  pl.BlockSpec((1, m, XATTN_W), lambda bi, i: (bi, 0, 0)),
                  pl.BlockSpec((1, m, XATTN_W), lambda bi, i: (bi, 0, 1))],
        out_specs=pl.BlockSpec((1, tm, XATTN_W), lambda bi, i: (bi, i, 0)),
        out_shape=jax.ShapeDtypeStruct((b, s, XATTN_W), _BF),
        compiler_params=_params("parallel", "parallel"),
        name="memattn",
    )(p3, kv, kv)


def _out_ln_kernel(x_ref, mix_ref, rec_ref, wo_ref, g_ref, b_ref, o_ref):
    y = jnp.dot(mix_ref[...], wo_ref[:MIX_W, :], preferred_element_type=_F32)
    y = y + jnp.dot(rec_ref[...], wo_ref[MIX_W:, :], preferred_element_type=_F32)
    o_ref[...] = _layer_norm(ALPHA * x_ref[...] + y, g_ref[...], b_ref[...])


def _out_ln(x2, mixed, recalled, wo, g, b, tm):
    n = x2.shape[0]
    return pl.pallas_call(
        _out_ln_kernel,
        grid=(n // tm,),
        in_specs=[pl.BlockSpec((tm, D_MODEL), lambda i: (i, 0)),
                  pl.BlockSpec((tm, MIX_W), lambda i: (i, 0)),
                  pl.BlockSpec((tm, XATTN_W), lambda i: (i, 0)),
                  pl.BlockSpec((D_MODEL, D_MODEL), lambda i: (0, 0)),
                  pl.BlockSpec((1, D_MODEL), lambda i: (0, 0)),
                  pl.BlockSpec((1, D_MODEL), lambda i: (0, 0))],
        out_specs=pl.BlockSpec((tm, D_MODEL), lambda i: (i, 0)),
        out_shape=jax.ShapeDtypeStruct((n, D_MODEL), _F32),
        compiler_params=_params("parallel"),
        name="out_ln",
    )(x2, mixed, recalled, wo, g, b)


def _dwconv_rows(h, w):
    n = h.shape[0]
    return (pltpu.roll(h, 1, 0) * w[0:1] + h * w[1:2] + pltpu.roll(h, n - 1, 0) * w[2:3])


def _ffn_kernel(x_ref, xp_ref, xn_ref, wu_ref, wc_ref, wd_ref, g_ref, b_ref, o_ref, gated_ref,
                *, ts, halo, fb):
    i = pl.program_id(1)
    prev = jnp.where(i == 0, 0.0, xp_ref[0])
    nxt = jnp.where(i == pl.num_programs(1) - 1, 0.0, xn_ref[0])
    xb = jnp.concatenate([prev, x_ref[0], nxt], axis=0).astype(_BF)
    for c in range(D_FF // fb):
        ca = slice(c * fb, (c + 1) * fb)
        cv = slice(D_FF + c * fb, D_FF + (c + 1) * fb)
        ha = jnp.dot(xb, wu_ref[:, ca], preferred_element_type=_F32)
        hv = jnp.dot(xb, wu_ref[:, cv], preferred_element_type=_F32)
        a = _dwconv_rows(ha, wc_ref[:, ca])[halo:halo + ts]
        v = _dwconv_rows(hv, wc_ref[:, cv])[halo:halo + ts]
        gated_ref[:, ca] = (_silu(a) * v).astype(_BF)
    for r0 in range(0, ts, FFN_OUT_ROWS):
        rows = slice(r0, r0 + FFN_OUT_ROWS)
        f = jnp.dot(gated_ref[rows, :], wd_ref[...], preferred_element_type=_F32)
        o_ref[0, rows, :] = _layer_norm(ALPHA * x_ref[0, rows, :] + f, g_ref[...], b_ref[...])


def _conv_ffn_ln(x3, w_up, w_conv, w_down, g, b, ts, fb):
    bsz, s, d = x3.shape
    halo = SUBLANES_BF16
    hb = ts // halo
    last = s // halo - 1
    kern = functools.partial(_ffn_kernel, ts=ts, halo=halo, fb=fb)
    resident = lambda shape: pl.BlockSpec(shape, lambda bi, i: (0, 0),
                                          pipeline_mode=pl.Buffered(1))
    return pl.pallas_call(
        kern,
        grid=(bsz, s // ts),
        in_specs=[pl.BlockSpec((1, ts, d), lambda bi, i: (bi, i, 0)),
                  pl.BlockSpec((1, halo, d), lambda bi, i: (bi, jnp.maximum(i * hb - 1, 0), 0)),
                  pl.BlockSpec((1, halo, d), lambda bi, i: (bi, jnp.minimum((i + 1) * hb, last), 0)),
                  resident((d, 2 * D_FF)),
                  resident((3, 2 * D_FF)),
                  resident((D_FF, d)),
                  pl.BlockSpec((1, d), lambda bi, i: (0, 0)),
                  pl.BlockSpec((1, d), lambda bi, i: (0, 0))],
        out_specs=pl.BlockSpec((1, ts, d), lambda bi, i: (bi, i, 0)),
        out_shape=jax.ShapeDtypeStruct((bsz, s, d), _F32),
        scratch_shapes=[pltpu.VMEM((ts, D_FF), _BF)],
        compiler_params=_params("parallel", "parallel"),
        name="conv_ffn",
    )(x3, x3, x3, w_up, w_conv, w_down, g, b)


def _fold_rows(x, op, group=SUBLANES_F32):
    acc = x[:group]
    for g in range(1, x.shape[0] // group):
        acc = op(acc, x[g * group:(g + 1) * group])
    while acc.shape[0] > SUBLANES_F32:
        half = acc.shape[0] // 2
        acc = op(acc[:half], acc[half:])
    return acc


def _attn_kernel(lam_ref, far_ref, q_ref, k_ref, vt_ref, bias_ref, subln_ref, o_ref,
                 s_a, s_b, acc_ref, *, tq, tk, out_scale):
    heads = q_ref.shape[2] // DA_W
    streams = [(hh, mi) for hh in range(heads) for mi in range(2)]
    n_str = len(streams)
    hp = pl.program_id(1)
    nkv = k_ref.shape[1] // tk
    nq = q_ref.shape[1] // tq
    n_chunks = tk // ATTN_ROW_CHUNK
    lane = lax.broadcasted_iota(jnp.int32, (1, heads * DA_W), 1)

    unit, u_min, u_max = _near_offsets(tq, ATTN_ROW_CHUNK)
    consts = [[far_ref[3 * (heads * hp + hh) + c] for c in range(3)] for hh in range(heads)]

    def stream_queries(i):
        q = q_ref[0, pl.ds(pl.multiple_of(i * tq, tq), tq), :]
        zero = jnp.zeros_like(q)
        return [jnp.where((lane >= hh * DA_W + mi * DA_HD) & (lane < hh * DA_W + (mi + 1) * DA_HD),
                          q, zero).T for hh, mi in streams]

    def side_select(i, j, r, left, mid, right):
        u = (tk * j + ATTN_ROW_CHUNK * r - tq * i) // unit
        return jnp.where(u < u_min, left, jnp.where(u > u_max, right, mid))

    def max_bound(i, j, r, hh):
        return side_select(i, j, r, consts[hh][0], consts[hh][2], consts[hh][1])

    def score_rows(qs, i, j, r, s_out, st):
        k0 = pl.multiple_of(j * tk + r * ATTN_ROW_CHUNK, ATTN_ROW_CHUNK)
        s = jnp.dot(k_ref[0, pl.ds(k0, ATTN_ROW_CHUNK), :], qs[st], preferred_element_type=_F32)
        s_out[st, r * ATTN_ROW_CHUNK:(r + 1) * ATTN_ROW_CHUNK, :] = s
        return _fold_rows(s, jnp.maximum) + max_bound(i, j, r, streams[st][0])

    def add_near_bias(i, j, s_out):
        u_first = (tk * j - tq * i) // unit
        u_last = (tk * j + ATTN_ROW_CHUNK * (n_chunks - 1) - tq * i) // unit

        @pl.when((u_last >= u_min) & (u_first <= u_max))
        def _():
            for r in range(n_chunks):
                u = (tk * j + ATTN_ROW_CHUNK * r - tq * i) // unit
                rows = slice(r * ATTN_ROW_CHUNK, (r + 1) * ATTN_ROW_CHUNK)

                @pl.when((u >= u_min) & (u <= u_max))
                def _():
                    for st, (hh, _) in enumerate(streams):
                        s_out[st, rows, :] = s_out[st, rows, :] + bias_ref[u - u_min, hh]

    def half_step(i, j, carry, s_cur, s_nxt, nxt):
        ni, nj, nqs = nxt
        tmax = [None] * n_str
        pv = [None] * n_str
        for r in range(n_chunks):
            rows = slice(r * ATTN_ROW_CHUNK, (r + 1) * ATTN_ROW_CHUNK)
            for st in range(n_str):
                t = score_rows(nqs, ni, nj, r, s_nxt, st)
                tmax[st] = t if r == 0 else jnp.maximum(tmax[st], t)
            v0 = pl.multiple_of(j * tk + r * ATTN_ROW_CHUNK, ATTN_ROW_CHUNK)
            for st, (hh, _) in enumerate(streams):
                shift = side_select(i, j, r, consts[hh][0], 0.0, consts[hh][1])
                p = jnp.exp2(s_cur[st, rows, :] - (carry[st][0] - shift)).astype(_BF)
                d = jnp.dot(vt_ref[0, hh, :, pl.ds(v0, ATTN_ROW_CHUNK)], p,
                            preferred_element_type=_F32)
                pv[st] = d if r == 0 else pv[st] + d
        new, tile_max = [], []
        for st in range(n_str):
            m, alpha = carry[st]
            acc_ref[st] = alpha * acc_ref[st] + pv[st]
            cand = jnp.max(tmax[st], axis=0, keepdims=True)
            m_nxt = jnp.maximum(m, cand)
            new.append((m_nxt, jnp.exp2(m - m_nxt)))
            tile_max.append(cand)
        add_near_bias(ni, nj, s_nxt)
        return tuple(new), tuple(tile_max)

    qs0 = stream_queries(0)
    first = []
    for st in range(n_str):
        parts = [score_rows(qs0, 0, 0, r, s_a, st) for r in range(n_chunks)]
        first.append(jnp.max(functools.reduce(jnp.maximum, parts), axis=0, keepdims=True))
    add_near_bias(0, 0, s_a)
    lam = lam_ref[0]

    def query_tile(i, m_first):
        qs = stream_queries(i)
        acc_ref[...] = jnp.zeros_like(acc_ref)
        carry = tuple((m_first[st], jnp.zeros((1, tq), _F32)) for st in range(n_str))

        def pair(jj, carry):
            j = 2 * jj
            carry, _ = half_step(i, j, carry, s_a, s_b, (i, j + 1, qs))
            carry, _ = half_step(i, j + 1, carry, s_b, s_a, (i, j + 2, qs))
            return carry

        carry = lax.fori_loop(0, nkv // 2 - 1, pair, carry)
        carry, _ = half_step(i, nkv - 2, carry, s_a, s_b, (i, nkv - 1, qs))
        i_next = jnp.minimum(i + 1, nq - 1)
        _, m_next = half_step(i, nkv - 1, carry, s_b, s_a, (i_next, 0, stream_queries(i_next)))

        q0 = pl.multiple_of(i * tq, tq)
        for hh in range(heads):
            o0 = acc_ref[2 * hh, :DA_W] / acc_ref[2 * hh, DA_W:DA_W + 1]
            o1 = acc_ref[2 * hh + 1, :DA_W] / acc_ref[2 * hh + 1, DA_W:DA_W + 1]
            o = (o0 - lam * o1).T
            ms = jnp.mean(o * o, axis=-1, keepdims=True)
            o_ref[0, pl.ds(q0, tq), hh * DA_W:(hh + 1) * DA_W] = (
                o * lax.rsqrt(ms + LN_EPS) * subln_ref[...] * out_scale).astype(o_ref.dtype)
        return m_next

    lax.fori_loop(0, nq, query_tile, tuple(first))


def _t5_bucket(rel):
    half = N_BUCKETS // 2
    max_exact = half // 2
    ret = jnp.where(rel > 0, half, 0)
    n = jnp.abs(rel)
    nf = jnp.maximum(n, 1).astype(_F32)
    large = max_exact + (jnp.log(nf / max_exact) / math.log(MAX_DISTANCE / max_exact)
                         * (half - max_exact)).astype(jnp.int32)
    large = jnp.minimum(large, half - 1)
    return ret + jnp.where(n < max_exact, n, large)


def _near_offsets(tq, tk):
    unit = math.gcd(tq, tk)
    first = -((tk + MAX_DISTANCE - 2) // unit)
    last = (tq + MAX_DISTANCE - 2) // unit
    return unit, first, last


def _bias_tiles(rel_bias, tq, tk):
    unit, u_min, u_max = _near_offsets(tq, tk)
    n_near = u_max - u_min + 1
    c = jnp.arange(tk, dtype=jnp.int32)[:, None]
    r = jnp.arange(tq, dtype=jnp.int32)[None, :]
    u = jnp.arange(u_min, u_max + 1, dtype=jnp.int32)[:, None, None]
    table = rel_bias.astype(_F32) * LOG2E
    bucket = _t5_bucket(u * unit + c - r)[:, None]
    tiles = jnp.zeros((n_near, DA_HEADS, tk, tq), _F32)
    for b in range(N_BUCKETS):
        tiles = jnp.where(bucket == b, table[b][None, :, None, None], tiles)
    far =table[_t5_bucket(jnp.array([-MAX_DISTANCE, MAX_DISTANCE], jnp.int32))]
    consts = jnp.concatenate([far, jnp.max(table, axis=0, keepdims=True)], axis=0)
    return tiles, consts.T.reshape(-1)


def _diff_attention(p3, vt, lam, subln, bias, far, layer, tq, tk):
    b, s, _ = p3.shape
    vrows = vt.shape[2]
    assert s % (2 * tk) == 0
    lam_init = 0.8 - 0.6 * math.exp(-0.3 * layer)
    kern = functools.partial(_attn_kernel, tq=tq, tk=tk, out_scale=1.0 - lam_init)
    nh = ATTN_HEADS_PER_STEP
    groups = DA_HEADS // nh
    return pl.pallas_call(
        kern,
        grid=(b, groups),
        in_specs=[pl.BlockSpec(memory_space=pltpu.SMEM),
                  pl.BlockSpec(memory_space=pltpu.SMEM),
                  pl.BlockSpec((1, s, nh * DA_W), lambda bi, h: (bi, 0, h)),
                  pl.BlockSpec((1, s, nh * DA_W), lambda bi, h: (bi, 0, groups + h)),
                  pl.BlockSpec((1, nh, vrows, s), lambda bi, h: (bi, h, 0, 0)),
                  pl.BlockSpec((bias.shape[0], nh) + bias.shape[2:], lambda bi, h: (0, h, 0, 0)),
                  pl.BlockSpec((1, DA_W), lambda bi, h: (0, 0))],
        out_specs=pl.BlockSpec((1, s, nh * DA_W), lambda bi, h: (bi, 0, h)),
        out_shape=jax.ShapeDtypeStruct((b, s, MIX_W), _BF),
        scratch_shapes=[pltpu.VMEM((2 * nh, tk, tq), _F32), pltpu.VMEM((2 * nh, tk, tq), _F32),
                        pltpu.VMEM((2 * nh, vrows, tq), _F32)],
        compiler_params=_params("parallel", "parallel"),
        name="diff_attn",
    )(lam, far, p3, p3, vt, bias, subln)


def _hgrn_chunk_stages(z, q, v, lb, st, tri, rev):
    c_rows = z.shape[0]
    e = jnp.exp(-jnp.abs(z))
    r = 1.0 / (1.0 + e)
    sig_pos, sig_neg = jnp.where(z >= 0, r, e * r), jnp.where(z >= 0, e * r, r)
    f = lb + (1.0 - lb) * sig_pos
    kk = (1.0 - lb) * sig_neg
    logf = jnp.log(f)
    hi = logf.astype(_BF)
    lo = (logf - hi.astype(_F32)).astype(_BF)
    cum = (jnp.dot(tri, hi, preferred_element_type=_F32)
           + jnp.dot(tri, lo, preferred_element_type=_F32))
    qh = _silu(q) * (HG_DK ** -0.5)
    vb = v.astype(_BF)
    yield None
    total = cum[0:1, :] if rev else cum[c_rows - 1:c_rows, :]
    o_inter = lax.dot_general((qh * jnp.exp(cum)).astype(_BF), st.astype(_BF), _NT,
                              preferred_element_type=_F32)
    ks = (kk * jnp.exp(total - cum)).astype(_BF)
    st_new = st * jnp.exp(total) + jnp.dot(v.T.astype(_BF), ks, preferred_element_type=_F32)
    blocks = []
    for b in range(c_rows // HG_SUB):
        lo_r, hi_r = b * HG_SUB, (b + 1) * HG_SUB
        if rev:
            cols = slice(lo_r, c_rows)
            base = cum[hi_r:hi_r + 1, :] if hi_r < c_rows else jnp.zeros_like(total)
        else:
            cols = slice(0, hi_r)
            base = cum[lo_r - 1:lo_r, :] if b else jnp.zeros_like(total)
        qq = (qh[lo_r:hi_r] * jnp.exp(cum[lo_r:hi_r] - base)).astype(_BF)
        kt = (kk[cols] * jnp.exp(base - cum[cols])).astype(_BF)
        a = lax.dot_general(qq, kt, _NT, preferred_element_type=_F32)
        blocks.append((lo_r, hi_r, cols, a))
    yield None
    parts = []
    for lo_r, hi_r, cols, a in blocks:
        sub_row = lax.broadcasted_iota(jnp.int32, a.shape, 0) + lo_r
        sub_col = lax.broadcasted_iota(jnp.int32, a.shape, 1) + cols.start
        seen = (sub_col >= sub_row) if rev else (sub_col <= sub_row)
        a = jnp.where(seen, a, 0.0)
        parts.append(o_inter[lo_r:hi_r]
                     + jnp.dot(a.astype(_BF), vb[cols], preferred_element_type=_F32))
    yield jnp.concatenate(parts, axis=0), st_new


def _hgrn_kernel(qf_ref, vf_ref, zf_ref, qb_ref, vb_ref, zb_ref, lb_ref, of_ref, ob_ref, st_ref,
                 *, rows):
    @pl.when(pl.program_id(1) == 0)
    def _():
        st_ref[...] = jnp.zeros_like(st_ref)

    c_rows = HG_CHUNK
    n_chunks = rows // c_rows
    row = lax.broadcasted_iota(jnp.int32, (c_rows, c_rows), 0)
    col = lax.broadcasted_iota(jnp.int32, (c_rows, c_rows), 1)
    tris = ((col <= row).astype(_BF), (col >= row).astype(_BF))
    dirs = ((qf_ref, vf_ref, zf_ref, of_ref), (qb_ref, vb_ref, zb_ref, ob_ref))

    def chunk(c, carry):
        starts = (pl.multiple_of(c * c_rows, c_rows),
                  pl.multiple_of((n_chunks - 1 - c) * c_rows, c_rows))
        for h0 in range(0, HG_HEADS, HG_HEAD_GROUP):
            chains = []
            for h in range(h0, h0 + HG_HEAD_GROUP):
                cs = slice(h * HG_DK, (h + 1) * HG_DK)
                for d, (q_ref, v_ref, z_ref, o_ref) in enumerate(dirs):
                    rs = pl.ds(starts[d], c_rows)
                    gen = _hgrn_chunk_stages(z_ref[0, rs, cs], q_ref[0, rs, cs], v_ref[0, rs, cs],
                                             lb_ref[d, :, cs], st_ref[d, h], tris[d], rev=bool(d))
                    chains.append((gen, o_ref, rs, cs, d, h))
            for _ in range(2):
                for chain in chains:
                    next(chain[0])
            for gen, o_ref, rs, cs, d, h in chains:
                o, st = next(gen)
                o_ref[0, rs, cs] = o
                st_ref[d, h] = st
        return carry

    lax.fori_loop(0, n_chunks, chunk, 0)


def _hgrn_scan(p3, lb, rows):
    b, s, _ = p3.shape
    w = MIX_W
    t = s // rows
    fwd = lambda c: pl.BlockSpec((1, rows, w), lambda bi, i: (bi, i, c))
    bwd = lambda c: pl.BlockSpec((1, rows, w), lambda bi, i: (bi, t - 1 - i, c))
    out = jax.ShapeDtypeStruct((b, s, w), _F32)
    return pl.pallas_call(
        functools.partial(_hgrn_kernel, rows=rows),
        grid=(b, t),
        in_specs=[fwd(0), fwd(1), fwd(3), bwd(0), bwd(1), bwd(4),
                  pl.BlockSpec((2, 1, w), lambda bi, i: (0, 0, 0))],
        out_specs=[fwd(0), bwd(0)],
        out_shape=[out, out],
        scratch_shapes=[pltpu.VMEM((2, HG_HEADS, HG_DK, HG_DK), _F32)],
        compiler_params=_params("parallel", "arbitrary"),
        name="hgrn_scan",
    )(p3, p3, p3, p3, p3, p3, lb)


def _hgrn_finish_kernel(of_ref, ob_ref, g_ref, w_ref, o_ref):
    o = of_ref[...] + ob_ref[...]
    gate = _silu(g_ref[...])
    w = w_ref[...]
    for h in range(HG_HEADS):
        cs = slice(h * HG_DK, (h + 1) * HG_DK)
        oh = o[:, cs]
        ms = jnp.mean(oh * oh, axis=-1, keepdims=True)
        o_ref[:, cs] = (oh * lax.rsqrt(ms + LN_EPS) * w[:, cs] * gate[:, cs]).astype(o_ref.dtype)


def _hgrn_finish(o_fw, o_bw, p2, g_col, norm_w, tm):
    n = o_fw.shape[0]
    return pl.pallas_call(
        _hgrn_finish_kernel,
        grid=(n // tm,),
        in_specs=[pl.BlockSpec((tm, MIX_W), lambda i: (i, 0)),
                  pl.BlockSpec((tm, MIX_W), lambda i: (i, 0)),
                  pl.BlockSpec((tm, MIX_W), lambda i: (i, g_col)),
                  pl.BlockSpec((1, MIX_W), lambda i: (0, 0))],
        out_specs=pl.BlockSpec((tm, MIX_W), lambda i: (i, 0)),
        out_shape=jax.ShapeDtypeStruct((n, MIX_W), _BF),
        compiler_params=_params("parallel"),
        name="hgrn_finish",
    )(o_fw, o_bw, p2, norm_w)


def _sconv_kernel(gb_ref, gc_ref, h_ref, gcp_ref, hp_ref, gcn_ref, hn_ref, w_ref, o_ref, *, halo):
    i = pl.program_id(1)
    u = gc_ref[0].astype(_F32) * h_ref[0].astype(_F32)
    n = u.shape[0]
    u_prev = (gcp_ref[0].astype(_F32) * hp_ref[0].astype(_F32))[halo - 1:halo]
    u_next = (gcn_ref[0].astype(_F32) * hn_ref[0].astype(_F32))[0:1]
    u_prev = jnp.where(i == 0, 0.0, u_prev)
    u_next = jnp.where(i == pl.num_programs(1) - 1, 0.0, u_next)
    row = lax.broadcasted_iota(jnp.int32, u.shape, 0)
    down = jnp.where(row == 0, u_prev, pltpu.roll(u, 1, 0))
    up = jnp.where(row == n - 1, u_next, pltpu.roll(u, n - 1, 0))
    w = w_ref[...]
    y = down * w[0:1] + u * w[1:2] + up * w[2:3]
    o_ref[0] = (gb_ref[0].astype(_F32) * y).astype(o_ref.dtype)


def _short_conv(p3, conv_w, ts):
    b, s, _ = p3.shape
    halo = SUBLANES_BF16
    hb = ts // halo
    last = s // halo - 1
    main = lambda c: pl.BlockSpec((1, ts, MIX_W), lambda bi, i: (bi, i, c))
    prev = lambda c: pl.BlockSpec((1, halo, MIX_W), lambda bi, i: (bi, jnp.maximum(i * hb - 1, 0), c))
    nxt = lambda c: pl.BlockSpec((1, halo, MIX_W), lambda bi, i: (bi, jnp.minimum((i + 1) * hb, last), c))
    return pl.pallas_call(
        functools.partial(_sconv_kernel, halo=halo),
        grid=(b, s // ts),
        in_specs=[main(0), main(1), main(2), prev(1), prev(2), nxt(1), nxt(2),
                  pl.BlockSpec((3, MIX_W), lambda bi, i: (0, 0))],
        out_specs=pl.BlockSpec((1, ts, MIX_W), lambda bi, i: (bi, i, 0)),
        out_shape=jax.ShapeDtypeStruct((b, s, MIX_W), _BF),
        compiler_params=_params("parallel", "parallel"),
        name="short_conv",
    )(p3, p3, p3, p3, p3, p3, p3, conv_w)


def _tile(n, pref):
    return pref if n % pref == 0 else n


def kernel(x, mem, rel_bias, attn_w_in, attn_lambda, attn_subln, hgrn_w_in, hgrn_lower_bound,
           hgrn_norm, conv_w_in, conv_w, mem_w_kv, w_o, ln_gain, ln_bias, ffn_w_up, ffn_conv,
           ffn_w_down):
    b, s, d = x.shape
    n = b * s
    m = mem.shape[1]
    tm = _tile(s, ROW_TILE)
    hg_rows = _tile(s, HG_ROWS)

    bias, far = _bias_tiles(rel_bias, ATTN_Q_TILE, ATTN_ROW_CHUNK)
    mem2 = mem.reshape(b * m, d)
    lbw = jax.nn.softmax(hgrn_lower_bound.astype(_F32), axis=0)
    lb_all = jnp.cumsum(lbw, axis=0) - lbw[0]

    for layer in range(DEPTH):
        kind, j = layer % N_MIXERS, layer // N_MIXERS
        x2 = x.reshape(n, d)
        kv = _matmul(mem2, mem_w_kv[layer].astype(_BF), _BF, _tile(b * m, ROW_TILE))
        kv = kv.reshape(b, m, 2 * XATTN_W)
        if kind == 0:
            q_scale = jnp.where(jnp.arange(attn_w_in.shape[2]) < MIX_W, DA_HD ** -0.5 * LOG2E, 1.0)
            w_in = (attn_w_in[j] * q_scale).astype(_BF)
            p = _matmul(x2, w_in, _BF, tm).reshape(b, s, -1)
            lp = attn_lambda[j].astype(_F32)
            lam_init = 0.8 - 0.6 * math.exp(-0.3 * layer)
            lam = jnp.exp(jnp.sum(lp[0] * lp[1])) - jnp.exp(jnp.sum(lp[2] * lp[3])) + lam_init
            vt = jnp.swapaxes(p[..., 2 * MIX_W:3 * MIX_W], 1, 2).reshape(b, DA_HEADS, DA_W, s)
            vt = jnp.concatenate([vt, jnp.ones((b, DA_HEADS, SUBLANES_BF16, s), _BF)], axis=2)
            mixed = _diff_attention(p, vt, lam.reshape(1),
                                    attn_subln[j].astype(_F32).reshape(1, DA_W),
                                    bias, far, layer, ATTN_Q_TILE, ATTN_K_TILE)
            q_col = 3 * MIX_W // XATTN_W
        elif kind == 1:
            w_in = hgrn_w_in[j].astype(_BF)
            p = _matmul(x2, w_in, _F32, tm).reshape(b, s, -1)
            o_fw, o_bw = _hgrn_scan(p, lb_all[layer].reshape(2, 1, MIX_W), hg_rows)
            mixed = _hgrn_finish(o_fw.reshape(n, MIX_W), o_bw.reshape(n, MIX_W),
                                 p.reshape(n, -1), 2,
                                 hgrn_norm[j].astype(_F32).reshape(1, MIX_W), tm)
            q_col = 5 * MIX_W // XATTN_W
        else:
            w_in = conv_w_in[j].astype(_BF)
            p = _matmul(x2, w_in, _BF, tm).reshape(b, s, -1)
            mixed = _short_conv(p, conv_w[j].astype(_F32), tm)
            q_col = 3 * MIX_W // XATTN_W
        recalled = _memory_attention(p, kv, q_col, tm)
        x2 = _out_ln(x2, mixed.reshape(n, MIX_W), recalled.reshape(n, XATTN_W),
                     w_o[layer].astype(_BF), ln_gain[layer, 0].reshape(1, d),
                     ln_bias[layer, 0].reshape(1, d), tm)
        x = _conv_ffn_ln(x2.reshape(b, s, d), ffn_w_up[layer].astype(_BF),
                         ffn_conv[layer].astype(_F32), ffn_w_down[layer].astype(_BF),
                         ln_gain[layer, 1].reshape(1, d), ln_bias[layer, 1].reshape(1, d),
                         tm, FFN_TILE)
    return x
```

```python
import functools
import math

import jax
import jax.numpy as jnp
from jax import lax
from jax.experimental import pallas as pl
from jax.experimental.pallas import tpu as pltpu

D_MODEL = 1024
DEPTH = 4
N_MIXERS = 3
MIX_W = 3 * D_MODEL // 4
XATTN_HEADS = 4
XATTN_W = D_MODEL - MIX_W
XATTN_HD = XATTN_W // XATTN_HEADS
DA_HD = 64
DA_HEADS = MIX_W // (2 * DA_HD)
DA_W = 2 * DA_HD
N_BUCKETS = 32
MAX_DISTANCE = 128
HG_DK = 128
HG_HEADS = MIX_W // HG_DK
HG_CHUNK = 64
HG_SUB = 16
D_FF = ((8 * D_MODEL // 3 + 255) // 256) * 256
LN_EPS = 1e-5
ALPHA = (2 * DEPTH) ** 0.25
LOG2E = math.log2(math.e)

LANES = 128
SUBLANES_F32 = 8
SUBLANES_BF16 = 16
VMEM_LIMIT = 52 * 1024 * 1024

ROW_TILE = 512
ATTN_Q_TILE = 256
ATTN_K_TILE = 8192
ATTN_HEADS_PER_STEP = 1
ATTN_ROW_CHUNK = 256
MXU_WIDTH = 256
FFN_TILE = MXU_WIDTH
FFN_OUT_ROWS = 256
HG_ROWS = 512
HG_HEAD_GROUP = 6

_NEG = -1e30
_BF = jnp.bfloat16
_F32 = jnp.float32
_NT = (((1,), (1,)), ((), ()))


def _params(*sem):
    return pltpu.CompilerParams(dimension_semantics=sem, vmem_limit_bytes=VMEM_LIMIT)


def _sigmoid(z):
    e = jnp.exp(-jnp.abs(z))
    r = 1.0 / (1.0 + e)
    return jnp.where(z >= 0, r, e * r)


def _silu(z):
    return z * _sigmoid(z)


def _layer_norm(r, g, b):
    mu = jnp.mean(r, axis=-1, keepdims=True)
    rc = r - mu
    var = jnp.mean(rc * rc, axis=-1, keepdims=True)
    return rc * lax.rsqrt(var + LN_EPS) * g + b


def _mm_kernel(x_ref, w_ref, o_ref):
    o_ref[...] = jnp.dot(x_ref[...].astype(_BF), w_ref[...],
                         preferred_element_type=_F32).astype(o_ref.dtype)


def _matmul(x, w, out_dtype, tm):
    m, k = x.shape
    n = w.shape[1]
    return pl.pallas_call(
        _mm_kernel,
        grid=(m // tm,),
        in_specs=[pl.BlockSpec((tm, k), lambda i: (i, 0)),
                  pl.BlockSpec((k, n), lambda i: (0, 0), pipeline_mode=pl.Buffered(1))],
        out_specs=pl.BlockSpec((tm, n), lambda i: (i, 0)),
        out_shape=jax.ShapeDtypeStruct((m, n), out_dtype),
        compiler_params=_params("parallel"),
        name="proj",
    )(x, w)


def _memattn_kernel(q_ref, km_ref, vm_ref, o_ref):
    q = q_ref[0].astype(_BF)
    km = km_ref[0]
    vm = vm_ref[0]
    lane = lax.broadcasted_iota(jnp.int32, (1, XATTN_W), 1)
    acc = jnp.zeros(q.shape, _F32)
    for h in range(XATTN_HEADS):
        head = (lane >= h * XATTN_HD) & (lane < (h + 1) * XATTN_HD)
        qh = jnp.where(head, q, jnp.zeros_like(q))
        s = lax.dot_general(qh, km, _NT, preferred_element_type=_F32) * (XATTN_HD ** -0.5)
        p = jnp.exp(s - jnp.max(s, axis=-1, keepdims=True))
        l = jnp.sum(p, axis=-1, keepdims=True)
        vh = jnp.where(head, vm, jnp.zeros_like(vm))
        acc = acc + jnp.dot(p.astype(_BF), vh, preferred_element_type=_F32) / l
    o_ref[0] = acc.astype(o_ref.dtype)


def _memory_attention(p3, kv, q_col, tm):
    b, s, _ = p3.shape
    m = kv.shape[1]
    return pl.pallas_call(
        _memattn_kernel,
        grid=(b, s // tm),
        in_specs=[pl.BlockSpec((1, tm, XATTN_W), lambda bi, i: (bi, i, q_col)),
                  pl.BlockSpec((1, m, XATTN_W), lambda bi, i: (bi, 0, 0)),
                  pl.BlockSpec((1, m, XATTN_W), lambda bi, i: (bi, 0, 1))],
        out_specs=pl.BlockSpec((1, tm, XATTN_W), lambda bi, i: (bi, i, 0)),
        out_shape=jax.ShapeDtypeStruct((b, s, XATTN_W), _BF),
        compiler_params=_params("parallel", "parallel"),
        name="memattn",
    )(p3, kv, kv)


def _out_ln_kernel(x_ref, mix_ref, rec_ref, wo_ref, g_ref, b_ref, o_ref):
    for r0 in range(0, x_ref.shape[0], FFN_OUT_ROWS):
        rows = slice(r0, r0 + FFN_OUT_ROWS)
        y = jnp.dot(mix_ref[rows, :], wo_ref[:MIX_W, :], preferred_element_type=_F32)
        y = y + jnp.dot(rec_ref[rows, :], wo_ref[MIX_W:, :], preferred_element_type=_F32)
        o_ref[rows, :] = _layer_norm(ALPHA * x_ref[rows, :] + y, g_ref[...], b_ref[...])


def _out_ln(x2, mixed, recalled, wo, g, b, tm):
    n = x2.shape[0]
    return pl.pallas_call(
        _out_ln_kernel,
        grid=(n // tm,),
        in_specs=[pl.BlockSpec((tm, D_MODEL), lambda i: (i, 0)),
                  pl.BlockSpec((tm, MIX_W), lambda i: (i, 0)),
                  pl.BlockSpec((tm, XATTN_W), lambda i: (i, 0)),
                  pl.BlockSpec((D_MODEL, D_MODEL), lambda i: (0, 0)),
                  pl.BlockSpec((1, D_MODEL), lambda i: (0, 0)),
                  pl.BlockSpec((1, D_MODEL), lambda i: (0, 0))],
        out_specs=pl.BlockSpec((tm, D_MODEL), lambda i: (i, 0)),
        out_shape=jax.ShapeDtypeStruct((n, D_MODEL), _F32),
        compiler_params=_params("parallel"),
        name="out_ln",
    )(x2, mixed, recalled, wo, g, b)


def _dwconv_rows(h, w):
    n = h.shape[0]
    return (pltpu.roll(h, 1, 0) * w[0:1] + h * w[1:2] + pltpu.roll(h, n - 1, 0) * w[2:3])


def _ffn_kernel(x_ref, xp_ref, xn_ref, wu_ref, wc_ref, wd_ref, g_ref, b_ref, o_ref, gated_ref,
                *, ts, halo, fb):
    i = pl.program_id(1)
    prev = jnp.where(i == 0, 0.0, xp_ref[0])
    nxt = jnp.where(i == pl.num_programs(1) - 1, 0.0, xn_ref[0])
    xb = jnp.concatenate([prev, x_ref[0], nxt], axis=0).astype(_BF)
    for c in range(D_FF // fb):
        ca = slice(c * fb, (c + 1) * fb)
        cv = slice(D_FF + c * fb, D_FF + (c + 1) * fb)
        ha = jnp.dot(xb, wu_ref[:, ca], preferred_element_type=_F32)
        hv = jnp.dot(xb, wu_ref[:, cv], preferred_element_type=_F32)
        a = _dwconv_rows(ha, wc_ref[:, ca])[halo:halo + ts]
        v = _dwconv_rows(hv, wc_ref[:, cv])[halo:halo + ts]
        gated_ref[:, ca] = (_silu(a) * v).astype(_BF)
    for r0 in range(0, ts, FFN_OUT_ROWS):
        rows = slice(r0, r0 + FFN_OUT_ROWS)
        f = jnp.dot(gated_ref[rows, :], wd_ref[...], preferred_element_type=_F32)
        o_ref[0, rows, :] = _layer_norm(ALPHA * x_ref[0, rows, :] + f, g_ref[...], b_ref[...])


def _conv_ffn_ln(x3, w_up, w_conv, w_down, g, b, ts, fb):
    bsz, s, d = x3.shape
    halo = SUBLANES_BF16
    hb = ts // halo
    last = s // halo - 1
    kern = functools.partial(_ffn_kernel, ts=ts, halo=halo, fb=fb)
    resident = lambda shape: pl.BlockSpec(shape, lambda bi, i: (0, 0),
                                          pipeline_mode=pl.Buffered(1))
    return pl.pallas_call(
        kern,
        grid=(bsz, s // ts),
        in_specs=[pl.BlockSpec((1, ts, d), lambda bi, i: (bi, i, 0)),
                  pl.BlockSpec((1, halo, d), lambda bi, i: (bi, jnp.maximum(i * hb - 1, 0), 0)),
                  pl.BlockSpec((1, halo, d), lambda bi, i: (bi, jnp.minimum((i + 1) * hb, last), 0)),
                  resident((d, 2 * D_FF)),
                  resident((3, 2 * D_FF)),
                  resident((D_FF, d)),
                  pl.BlockSpec((1, d), lambda bi, i: (0, 0)),
                  pl.BlockSpec((1, d), lambda bi, i: (0, 0))],
        out_specs=pl.BlockSpec((1, ts, d), lambda bi, i: (bi, i, 0)),
        out_shape=jax.ShapeDtypeStruct((bsz, s, d), _F32),
        scratch_shapes=[pltpu.VMEM((ts, D_FF), _BF)],
        compiler_params=_params("parallel", "parallel"),
        name="conv_ffn",
    )(x3, x3, x3, w_up, w_conv, w_down, g, b)


def _fold_rows(x, op, group=SUBLANES_F32):
    acc = x[:group]
    for g in range(1, x.shape[0] // group):
        acc = op(acc, x[g * group:(g + 1) * group])
    while acc.shape[0] > SUBLANES_F32:
        half = acc.shape[0] // 2
        acc = op(acc[:half], acc[half:])
    return acc


def _attn_kernel(lam_ref, far_ref, q_ref, k_ref, vt_ref, bias_ref, subln_ref, o_ref,
                 s_a, s_b, acc_ref, *, tq, tk, out_scale):
    heads = q_ref.shape[2] // DA_W
    streams = [(hh, mi) for hh in range(heads) for mi in range(2)]
    n_str = len(streams)
    hp = pl.program_id(1)
    nkv = k_ref.shape[1] // tk
    nq = q_ref.shape[1] // tq
    n_chunks = tk // ATTN_ROW_CHUNK
    lane = lax.broadcasted_iota(jnp.int32, (1, heads * DA_W), 1)

    unit, u_min, u_max = _near_offsets(tq, ATTN_ROW_CHUNK)
    consts = [[far_ref[3 * (heads * hp + hh) + c] for c in range(3)] for hh in range(heads)]

    def stream_queries(i):
        q = q_ref[0, pl.ds(pl.multiple_of(i * tq, tq), tq), :]
        zero = jnp.zeros_like(q)
        return [jnp.where((lane >= hh * DA_W + mi * DA_HD) & (lane < hh * DA_W + (mi + 1) * DA_HD),
                          q, zero).T for hh, mi in streams]

    def side_select(i, j, r, left, mid, right):
        u = (tk * j + ATTN_ROW_CHUNK * r - tq * i) // unit
        return jnp.where(u < u_min, left, jnp.where(u > u_max, right, mid))

    def max_bound(i, j, r, hh):
        return side_select(i, j, r, consts[hh][0], consts[hh][2], consts[hh][1])

    def score_rows(qs, i, j, r, s_out, st):
        k0 = pl.multiple_of(j * tk + r * ATTN_ROW_CHUNK, ATTN_ROW_CHUNK)
        s = jnp.dot(k_ref[0, pl.ds(k0, ATTN_ROW_CHUNK), :], qs[st], preferred_element_type=_F32)
        s_out[st, r * ATTN_ROW_CHUNK:(r + 1) * ATTN_ROW_CHUNK, :] = s
        return _fold_rows(s, jnp.maximum) + max_bound(i, j, r, streams[st][0])

    def add_near_bias(i, j, s_out):
        u_first = (tk * j - tq * i) // unit
        u_last = (tk * j + ATTN_ROW_CHUNK * (n_chunks - 1) - tq * i) // unit

        @pl.when((u_last >= u_min) & (u_first <= u_max))
        def _():
            for r in range(n_chunks):
                u = (tk * j + ATTN_ROW_CHUNK * r - tq * i) // unit
                rows = slice(r * ATTN_ROW_CHUNK, (r + 1) * ATTN_ROW_CHUNK)

                @pl.when((u >= u_min) & (u <= u_max))
                def _():
                    for st, (hh, _) in enumerate(streams):
                        s_out[st, rows, :] = s_out[st, rows, :] + bias_ref[u - u_min, hh]

    def half_step(i, j, carry, s_cur, s_nxt, nxt):
        ni, nj, nqs = nxt
        tmax = [None] * n_str
        pv = [None] * n_str
        for r in range(n_chunks):
            rows = slice(r * ATTN_ROW_CHUNK, (r + 1) * ATTN_ROW_CHUNK)
            for st in range(n_str):
                t = score_rows(nqs, ni, nj, r, s_nxt, st)
                tmax[st] = t if r == 0 else jnp.maximum(tmax[st], t)
            v0 = pl.multiple_of(j * tk + r * ATTN_ROW_CHUNK, ATTN_ROW_CHUNK)
            for st, (hh, _) in enumerate(streams):
                shift = side_select(i, j, r, consts[hh][0], 0.0, consts[hh][1])
                p = jnp.exp2(s_cur[st, rows, :] - (carry[st][0] - shift)).astype(_BF)
                d = jnp.dot(vt_ref[0, hh, :, pl.ds(v0, ATTN_ROW_CHUNK)], p,
                            preferred_element_type=_F32)
                pv[st] = d if r == 0 else pv[st] + d
        new, tile_max = [], []
        for st in range(n_str):
            m, alpha = carry[st]
            acc_ref[st] = alpha * acc_ref[st] + pv[st]
            cand = jnp.max(tmax[st], axis=0, keepdims=True)
            m_nxt = jnp.maximum(m, cand)
            new.append((m_nxt, jnp.exp2(m - m_nxt)))
            tile_max.append(cand)
        add_near_bias(ni, nj, s_nxt)
        return tuple(new), tuple(tile_max)

    qs0 = stream_queries(0)
    first = []
    for st in range(n_str):
        parts = [score_rows(qs0, 0, 0, r, s_a, st) for r in range(n_chunks)]
        first.append(jnp.max(functools.reduce(jnp.maximum, parts), axis=0, keepdims=True))
    add_near_bias(0, 0, s_a)
    lam = lam_ref[0]

    def query_tile(i, m_first, bufs=(s_a, s_b)):
        qs = stream_queries(i)
        acc_ref[...] = jnp.zeros_like(acc_ref)
        carry = tuple((m_first[st], jnp.zeros((1, tq), _F32)) for st in range(n_str))
        i_next = jnp.minimum(i + 1, nq - 1)
        first_of_next = (i_next, 0, stream_queries(i_next))

        if nkv == 1:
            _, m_next = half_step(i, 0, carry, bufs[0], bufs[1], first_of_next)
        else:
            def pair(jj, carry):
                j = 2 * jj
                carry, _ = half_step(i, j, carry, s_a, s_b, (i, j + 1, qs))
                carry, _ = half_step(i, j + 1, carry, s_b, s_a, (i, j + 2, qs))
                return carry

            carry = lax.fori_loop(0, nkv // 2 - 1, pair, carry)
            carry, _ = half_step(i, nkv - 2, carry, s_a, s_b, (i, nkv - 1, qs))
            _, m_next = half_step(i, nkv - 1, carry, s_b, s_a, first_of_next)

        q0 = pl.multiple_of(i * tq, tq)
        for hh in range(heads):
            o0 = acc_ref[2 * hh, :DA_W] / acc_ref[2 * hh, DA_W:DA_W + 1]
            o1 = acc_ref[2 * hh + 1, :DA_W] / acc_ref[2 * hh + 1, DA_W:DA_W + 1]
            o = (o0 - lam * o1).T
            ms = jnp.mean(o * o, axis=-1, keepdims=True)
            o_ref[0, pl.ds(q0, tq), hh * DA_W:(hh + 1) * DA_W] = (
                o * lax.rsqrt(ms + LN_EPS) * subln_ref[...] * out_scale).astype(o_ref.dtype)
        return m_next

    if nkv == 1:
        def query_tile_pair(ii, m_first):
            m_first = query_tile(2 * ii, m_first, (s_a, s_b))
            return query_tile(2 * ii + 1, m_first, (s_b, s_a))

        lax.fori_loop(0, nq // 2, query_tile_pair, tuple(first))
    else:
        lax.fori_loop(0, nq, query_tile, tuple(first))


def _t5_bucket(rel):
    half = N_BUCKETS // 2
    max_exact = half // 2
    ret = jnp.where(rel > 0, half, 0)
    n = jnp.abs(rel)
    nf = jnp.maximum(n, 1).astype(_F32)
    large = max_exact + (jnp.log(nf / max_exact) / math.log(MAX_DISTANCE / max_exact)
                         * (half - max_exact)).astype(jnp.int32)
    large = jnp.minimum(large, half - 1)
    return ret + jnp.where(n < max_exact, n, large)


def _near_offsets(tq, tk):
    unit = math.gcd(tq, tk)
    first = -((tk + MAX_DISTANCE - 2) // unit)
    last = (tq + MAX_DISTANCE - 2) // unit
    return unit, first, last


def _bias_tiles(rel_bias, tq, tk):
    unit, u_min, u_max = _near_offsets(tq, tk)
    n_near = u_max - u_min + 1
    c = jnp.arange(tk, dtype=jnp.int32)[:, None]
    r = jnp.arange(tq, dtype=jnp.int32)[None, :]
    u = jnp.arange(u_min, u_max + 1, dtype=jnp.int32)[:, None, None]
    table = rel_bias.astype(_F32) * LOG2E
    bucket = _t5_bucket(u * unit + c - r)[:, None]
    tiles = jnp.zeros((n_near, DA_HEADS, tk, tq), _F32)
    for b in range(N_BUCKETS):
        tiles = jnp.where(bucket == b, table[b][None, :, None, None], tiles)
    far =table[_t5_bucket(jnp.array([-MAX_DISTANCE, MAX_DISTANCE], jnp.int32))]
    consts = jnp.concatenate([far, jnp.max(table, axis=0, keepdims=True)], axis=0)
    return tiles, consts.T.reshape(-1)


def _diff_attention(p3, vt, lam, subln, bias, far, layer, tq, tk):
    b, s, _ = p3.shape
    vrows = vt.shape[2]
    assert s % (2 * tk) == 0 or (tk == s and s % (2 * tq) == 0)
    lam_init = 0.8 - 0.6 * math.exp(-0.3 * layer)
    kern = functools.partial(_attn_kernel, tq=tq, tk=tk, out_scale=1.0 - lam_init)
    nh = ATTN_HEADS_PER_STEP
    groups = DA_HEADS // nh
    return pl.pallas_call(
        kern,
        grid=(b, groups),
        in_specs=[pl.BlockSpec(memory_space=pltpu.SMEM),
                  pl.BlockSpec(memory_space=pltpu.SMEM),
                  pl.BlockSpec((1, s, nh * DA_W), lambda bi, h: (bi, 0, h),
                               pipeline_mode=pl.Buffered(1)),
                  pl.BlockSpec((1, s, nh * DA_W), lambda bi, h: (bi, 0, groups + h),
                               pipeline_mode=pl.Buffered(1)),
                  pl.BlockSpec((1, nh, vrows, s), lambda bi, h: (bi, h, 0, 0),
                               pipeline_mode=pl.Buffered(1)),
                  pl.BlockSpec((bias.shape[0], nh) + bias.shape[2:], lambda bi, h: (0, h, 0, 0)),
                  pl.BlockSpec((1, DA_W), lambda bi, h: (0, 0))],
        out_specs=pl.BlockSpec((1, s, nh * DA_W), lambda bi, h: (bi, 0, h)),
        out_shape=jax.ShapeDtypeStruct((b, s, MIX_W), _BF),
        scratch_shapes=[pltpu.VMEM((2 * nh, tk, tq), _F32), pltpu.VMEM((2 * nh, tk, tq), _F32),
                        pltpu.VMEM((2 * nh, vrows, tq), _F32)],
        compiler_params=_params("parallel", "parallel"),
        name="diff_attn",
    )(lam, far, p3, p3, vt, bias, subln)


def _hgrn_chunk_stages(z, q, v, lb, st, tri, rev):
    c_rows = z.shape[0]
    e = jnp.exp(-jnp.abs(z))
    r = 1.0 / (1.0 + e)
    sig_pos, sig_neg = jnp.where(z >= 0, r, e * r), jnp.where(z >= 0, e * r, r)
    f = lb + (1.0 - lb) * sig_pos
    kk = (1.0 - lb) * sig_neg
    logf = jnp.log(f)
    hi = logf.astype(_BF)
    lo = (logf - hi.astype(_F32)).astype(_BF)
    cum = (jnp.dot(tri, hi, preferred_element_type=_F32)
           + jnp.dot(tri, lo, preferred_element_type=_F32))
    qh = _silu(q) * (HG_DK ** -0.5)
    vb = v.astype(_BF)
    yield None
    total = cum[0:1, :] if rev else cum[c_rows - 1:c_rows, :]
    o_inter = lax.dot_general((qh * jnp.exp(cum)).astype(_BF), st.astype(_BF), _NT,
                              preferred_element_type=_F32)
    ks = (kk * jnp.exp(total - cum)).astype(_BF)
    st_new = st * jnp.exp(total) + jnp.dot(v.T.astype(_BF), ks, preferred_element_type=_F32)
    blocks = []
    for b in range(c_rows // HG_SUB):
        lo_r, hi_r = b * HG_SUB, (b + 1) * HG_SUB
        if rev:
            cols = slice(lo_r, c_rows)
            base = cum[hi_r:hi_r + 1, :] if hi_r < c_rows else jnp.zeros_like(total)
        else:
            cols = slice(0, hi_r)
            base = cum[lo_r - 1:lo_r, :] if b else jnp.zeros_like(total)
        qq = (qh[lo_r:hi_r] * jnp.exp(cum[lo_r:hi_r] - base)).astype(_BF)
        kt = (kk[cols] * jnp.exp(base - cum[cols])).astype(_BF)
        a = lax.dot_general(qq, kt, _NT, preferred_element_type=_F32)
        blocks.append((lo_r, hi_r, cols, a))
    yield None
    parts = []
    for lo_r, hi_r, cols, a in blocks:
        sub_row = lax.broadcasted_iota(jnp.int32, a.shape, 0) + lo_r
        sub_col = lax.broadcasted_iota(jnp.int32, a.shape, 1) + cols.start
        seen = (sub_col >= sub_row) if rev else (sub_col <= sub_row)
        a = jnp.where(seen, a, 0.0)
        parts.append(o_inter[lo_r:hi_r]
                     + jnp.dot(a.astype(_BF), vb[cols], preferred_element_type=_F32))
    yield jnp.concatenate(parts, axis=0), st_new


def _hgrn_kernel(qf_ref, vf_ref, zf_ref, qb_ref, vb_ref, zb_ref, lb_ref, of_ref, ob_ref, st_ref,
                 *, rows):
    @pl.when(pl.program_id(1) == 0)
    def _():
        st_ref[...] = jnp.zeros_like(st_ref)

    c_rows = HG_CHUNK
    n_chunks = rows // c_rows
    row = lax.broadcasted_iota(jnp.int32, (c_rows, c_rows), 0)
    col = lax.broadcasted_iota(jnp.int32, (c_rows, c_rows), 1)
    tris = ((col <= row).astype(_BF), (col >= row).astype(_BF))
    dirs = ((qf_ref, vf_ref, zf_ref, of_ref), (qb_ref, vb_ref, zb_ref, ob_ref))

    def chunk(c, carry):
        starts = (pl.multiple_of(c * c_rows, c_rows),
                  pl.multiple_of((n_chunks - 1 - c) * c_rows, c_rows))
        for h0 in range(0, HG_HEADS, HG_HEAD_GROUP):
            chains = []
            for h in range(h0, h0 + HG_HEAD_GROUP):
                cs = slice(h * HG_DK, (h + 1) * HG_DK)
                for d, (q_ref, v_ref, z_ref, o_ref) in enumerate(dirs):
                    rs = pl.ds(starts[d], c_rows)
                    gen = _hgrn_chunk_stages(z_ref[0, rs, cs], q_ref[0, rs, cs], v_ref[0, rs, cs],
                                             lb_ref[d, :, cs], st_ref[d, h], tris[d], rev=bool(d))
                    chains.append((gen, o_ref, rs, cs, d, h))
            for _ in range(2):
                for chain in chains:
                    next(chain[0])
            for gen, o_ref, rs, cs, d, h in chains:
                o, st = next(gen)
                o_ref[0, rs, cs] = o
                st_ref[d, h] = st
        return carry

    lax.fori_loop(0, n_chunks, chunk, 0)


def _hgrn_scan(p3, lb, rows):
    b, s, _ = p3.shape
    w = MIX_W
    t = s // rows
    fwd = lambda c: pl.BlockSpec((1, rows, w), lambda bi, i: (bi, i, c))
    bwd = lambda c: pl.BlockSpec((1, rows, w), lambda bi, i: (bi, t - 1 - i, c))
    out = jax.ShapeDtypeStruct((b, s, w), _F32)
    return pl.pallas_call(
        functools.partial(_hgrn_kernel, rows=rows),
        grid=(b, t),
        in_specs=[fwd(0), fwd(1), fwd(3), bwd(0), bwd(1), bwd(4),
                  pl.BlockSpec((2, 1, w), lambda bi, i: (0, 0, 0))],
        out_specs=[fwd(0), bwd(0)],
        out_shape=[out, out],
        scratch_shapes=[pltpu.VMEM((2, HG_HEADS, HG_DK, HG_DK), _F32)],
        compiler_params=_params("parallel", "arbitrary"),
        name="hgrn_scan",
    )(p3, p3, p3, p3, p3, p3, lb)


def _hgrn_finish_kernel(of_ref, ob_ref, g_ref, w_ref, o_ref):
    o = of_ref[...] + ob_ref[...]
    gate = _silu(g_ref[...])
    w = w_ref[...]
    for h in range(HG_HEADS):
        cs = slice(h * HG_DK, (h + 1) * HG_DK)
        oh = o[:, cs]
        ms = jnp.mean(oh * oh, axis=-1, keepdims=True)
        o_ref[:, cs] = (oh * lax.rsqrt(ms + LN_EPS) * w[:, cs] * gate[:, cs]).astype(o_ref.dtype)


def _hgrn_finish(o_fw, o_bw, p2, g_col, norm_w, tm):
    n = o_fw.shape[0]
    return pl.pallas_call(
        _hgrn_finish_kernel,
        grid=(n // tm,),
        in_specs=[pl.BlockSpec((tm, MIX_W), lambda i: (i, 0)),
                  pl.BlockSpec((tm, MIX_W), lambda i: (i, 0)),
                  pl.BlockSpec((tm, MIX_W), lambda i: (i, g_col)),
                  pl.BlockSpec((1, MIX_W), lambda i: (0, 0))],
        out_specs=pl.BlockSpec((tm, MIX_W), lambda i: (i, 0)),
        out_shape=jax.ShapeDtypeStruct((n, MIX_W), _BF),
        compiler_params=_params("parallel"),
        name="hgrn_finish",
    )(o_fw, o_bw, p2, norm_w)


def _sconv_kernel(gb_ref, gc_ref, h_ref, gcp_ref, hp_ref, gcn_ref, hn_ref, w_ref, o_ref, *, halo):
    i = pl.program_id(1)
    u = gc_ref[0].astype(_F32) * h_ref[0].astype(_F32)
    n = u.shape[0]
    u_prev = (gcp_ref[0].astype(_F32) * hp_ref[0].astype(_F32))[halo - 1:halo]
    u_next = (gcn_ref[0].astype(_F32) * hn_ref[0].astype(_F32))[0:1]
    u_prev = jnp.where(i == 0, 0.0, u_prev)
    u_next = jnp.where(i == pl.num_programs(1) - 1, 0.0, u_next)
    row = lax.broadcasted_iota(jnp.int32, u.shape, 0)
    down = jnp.where(row == 0, u_prev, pltpu.roll(u, 1, 0))
    up = jnp.where(row == n - 1, u_next, pltpu.roll(u, n - 1, 0))
    w = w_ref[...]
    y = down * w[0:1] + u * w[1:2] + up * w[2:3]
    o_ref[0] = (gb_ref[0].astype(_F32) * y).astype(o_ref.dtype)


def _short_conv(p3, conv_w, ts):
    b, s, _ = p3.shape
    halo = SUBLANES_BF16
    hb = ts // halo
    last = s // halo - 1
    main = lambda c: pl.BlockSpec((1, ts, MIX_W), lambda bi, i: (bi, i, c))
    prev = lambda c: pl.BlockSpec((1, halo, MIX_W), lambda bi, i: (bi, jnp.maximum(i * hb - 1, 0), c))
    nxt = lambda c: pl.BlockSpec((1, halo, MIX_W), lambda bi, i: (bi, jnp.minimum((i + 1) * hb, last), c))
    return pl.pallas_call(
        functools.partial(_sconv_kernel, halo=halo),
        grid=(b, s // ts),
        in_specs=[main(0), main(1), main(2), prev(1), prev(2), nxt(1), nxt(2),
                  pl.BlockSpec((3, MIX_W), lambda bi, i: (0, 0))],
        out_specs=pl.BlockSpec((1, ts, MIX_W), lambda bi, i: (bi, i, 0)),
        out_shape=jax.ShapeDtypeStruct((b, s, MIX_W), _BF),
        compiler_params=_params("parallel", "parallel"),
        name="short_conv",
    )(p3, p3, p3, p3, p3, p3, p3, conv_w)


def _tile(n, pref):
    return pref if n % pref == 0 else n


def kernel(x, mem, rel_bias, attn_w_in, attn_lambda, attn_subln, hgrn_w_in, hgrn_lower_bound,
           hgrn_norm, conv_w_in, conv_w, mem_w_kv, w_o, ln_gain, ln_bias, ffn_w_up, ffn_conv,
           ffn_w_down):
    b, s, d = x.shape
    n = b * s
    m = mem.shape[1]
    tm = _tile(s, ROW_TILE)
    hg_rows = _tile(s, HG_ROWS)

    bias, far = _bias_tiles(rel_bias, ATTN_Q_TILE, ATTN_ROW_CHUNK)
    mem2 = mem.reshape(b * m, d)
    lbw = jax.nn.softmax(hgrn_lower_bound.astype(_F32), axis=0)
    lb_all = jnp.cumsum(lbw, axis=0) - lbw[0]

    for layer in range(DEPTH):
        kind, j = layer % N_MIXERS, layer // N_MIXERS
        x2 = x.reshape(n, d)
        kv = _matmul(mem2, mem_w_kv[layer].astype(_BF), _BF, _tile(b * m, ROW_TILE))
        kv = kv.reshape(b, m, 2 * XATTN_W)
        if kind == 0:
            q_scale = jnp.where(jnp.arange(attn_w_in.shape[2]) < MIX_W, DA_HD ** -0.5 * LOG2E, 1.0)
            w_in = (attn_w_in[j] * q_scale).astype(_BF)
            p = _matmul(x2, w_in, _BF, tm).reshape(b, s, -1)
            lp = attn_lambda[j].astype(_F32)
            lam_init = 0.8 - 0.6 * math.exp(-0.3 * layer)
            lam = jnp.exp(jnp.sum(lp[0] * lp[1])) - jnp.exp(jnp.sum(lp[2] * lp[3])) + lam_init
            vt = jnp.swapaxes(p[..., 2 * MIX_W:3 * MIX_W], 1, 2).reshape(b, DA_HEADS, DA_W, s)
            vt = jnp.concatenate([vt, jnp.ones((b, DA_HEADS, SUBLANES_BF16, s), _BF)], axis=2)
            mixed = _diff_attention(p, vt, lam.reshape(1),
                                    attn_subln[j].astype(_F32).reshape(1, DA_W),
                                    bias, far, layer, ATTN_Q_TILE, ATTN_K_TILE)
            q_col = 3 * MIX_W // XATTN_W
        elif kind == 1:
            w_in = hgrn_w_in[j].astype(_BF)
            p = _matmul(x2, w_in, _F32, tm).reshape(b, s, -1)
            o_fw, o_bw = _hgrn_scan(p, lb_all[layer].reshape(2, 1, MIX_W), hg_rows)
            mixed = _hgrn_finish(o_fw.reshape(n, MIX_W), o_bw.reshape(n, MIX_W),
                                 p.reshape(n, -1), 2,
                                 hgrn_norm[j].astype(_F32).reshape(1, MIX_W), tm)
            q_col = 5 * MIX_W // XATTN_W
        else:
            w_in = conv_w_in[j].astype(_BF)
            p = _matmul(x2, w_in, _BF, tm).reshape(b, s, -1)
            mixed = _short_conv(p, conv_w[j].astype(_F32), tm)
            q_col = 3 * MIX_W // XATTN_W
        recalled = _memory_attention(p, kv, q_col, tm)
        x2 = _out_ln(x2, mixed.reshape(n, MIX_W), recalled.reshape(n, XATTN_W),
                     w_o[layer].astype(_BF), ln_gain[layer, 0].reshape(1, d),
                     ln_bias[layer, 0].reshape(1, d), tm)
        x = _conv_ffn_ln(x2.reshape(b, s, d), ffn_w_up[layer].astype(_BF),
                         ffn_conv[layer].astype(_F32), ffn_w_down[layer].astype(_BF),
                         ln_gain[layer, 1].reshape(1, d), ln_bias[layer, 1].reshape(1, d),
                         tm, FFN_TILE)
    return x
```

```python
import functools
import math

import jax
import jax.numpy as jnp
from jax import lax
from jax.experimental import pallas as pl
from jax.experimental.pallas import tpu as pltpu

D_MODEL = 1024
DEPTH = 4
N_MIXERS = 3
MIX_W = 3 * D_MODEL // 4
XATTN_HEADS = 4
XATTN_W = D_MODEL - MIX_W
XATTN_HD = XATTN_W // XATTN_HEADS
DA_HD = 64
DA_HEADS = MIX_W // (2 * DA_HD)
DA_W = 2 * DA_HD
N_BUCKETS = 32
MAX_DISTANCE = 128
HG_DK = 128
HG_HEADS = MIX_W // HG_DK
HG_CHUNK = 64
HG_SUB = 16
D_FF = ((8 * D_MODEL // 3 + 255) // 256) * 256
LN_EPS = 1e-5
ALPHA = (2 * DEPTH) ** 0.25
LOG2E = math.log2(math.e)

LANES = 128
SUBLANES_F32 = 8
SUBLANES_BF16 = 16
VMEM_LIMIT = 52 * 1024 * 1024

ROW_TILE = 512
ATTN_Q_TILE = 256
ATTN_K_TILE = 8192
ATTN_HEADS_PER_STEP = 1
ATTN_ROW_CHUNK = 256
MXU_WIDTH = 256
FFN_TILE = MXU_WIDTH
FFN_OUT_ROWS = 256
HG_ROWS = 512
HG_HEAD_GROUP = 6

_NEG = -1e30
_BF = jnp.bfloat16
_F32 = jnp.float32
_NT = (((1,), (1,)), ((), ()))


def _params(*sem):
    return pltpu.CompilerParams(dimension_semantics=sem, vmem_limit_bytes=VMEM_LIMIT)


def _sigmoid(z):
    e = jnp.exp(-jnp.abs(z))
    r = 1.0 / (1.0 + e)
    return jnp.where(z >= 0, r, e * r)


def _silu(z):
    return z * _sigmoid(z)


def _layer_norm(r, g, b):
    mu = jnp.mean(r, axis=-1, keepdims=True)
    rc = r - mu
    var = jnp.mean(rc * rc, axis=-1, keepdims=True)
    return rc * lax.rsqrt(var + LN_EPS) * g + b


def _mm_kernel(x_ref, w_ref, o_ref):
    o_ref[...] = jnp.dot(x_ref[...].astype(_BF), w_ref[...],
                         preferred_element_type=_F32).astype(o_ref.dtype)


def _matmul(x, w, out_dtype, tm):
    m, k = x.shape
    n = w.shape[1]
    return pl.pallas_call(
        _mm_kernel,
        grid=(m // tm,),
        in_specs=[pl.BlockSpec((tm, k), lambda i: (i, 0)),
                  pl.BlockSpec((k, n), lambda i: (0, 0), pipeline_mode=pl.Buffered(1))],
        out_specs=pl.BlockSpec((tm, n), lambda i: (i, 0)),
        out_shape=jax.ShapeDtypeStruct((m, n), out_dtype),
        compiler_params=_params("parallel"),
        name="proj",
    )(x, w)


def _memattn_kernel(q_ref, km_ref, vm_ref, o_ref):
    q = q_ref[0].astype(_BF)
    km = km_ref[0]
    vm = vm_ref[0]
    lane = lax.broadcasted_iota(jnp.int32, (1, XATTN_W), 1)
    acc = jnp.zeros(q.shape, _F32)
    for h in range(XATTN_HEADS):
        head = (lane >= h * XATTN_HD) & (lane < (h + 1) * XATTN_HD)
        qh = jnp.where(head, q, jnp.zeros_like(q))
        s = lax.dot_general(qh, km, _NT, preferred_element_type=_F32) * (XATTN_HD ** -0.5)
        p = jnp.exp(s - jnp.max(s, axis=-1, keepdims=True))
        l = jnp.sum(p, axis=-1, keepdims=True)
        vh = jnp.where(head, vm, jnp.zeros_like(vm))
        acc = acc + jnp.dot(p.astype(_BF), vh, preferred_element_type=_F32) / l
    o_ref[0] = acc.astype(o_ref.dtype)


def _memory_attention(p3, kv, q_col, tm):
    b, s, _ = p3.shape
    m = kv.shape[1]
    return pl.pallas_call(
        _memattn_kernel,
        grid=(b, s // tm),
        in_specs=[pl.BlockSpec((1, tm, XATTN_W), lambda bi, i: (bi, i, q_col)),
                  pl.BlockSpec((1, m, XATTN_W), lambda bi, i: (bi, 0, 0)),
                  pl.BlockSpec((1, m, XATTN_W), lambda bi, i: (bi, 0, 1))],
        out_specs=pl.BlockSpec((1, tm, XATTN_W), lambda bi, i: (bi, i, 0)),
        out_shape=jax.ShapeDtypeStruct((b, s, XATTN_W), _BF),
        compiler_params=_params("parallel", "parallel"),
        name="memattn",
    )(p3, kv, kv)


def _out_ln_kernel(x_ref, mix_ref, rec_ref, wo_ref, g_ref, b_ref, o_ref):
    for r0 in range(0, x_ref.shape[0], FFN_OUT_ROWS):
        rows = slice(r0, r0 + FFN_OUT_ROWS)
        y = jnp.dot(mix_ref[rows, :], wo_ref[:MIX_W, :], preferred_element_type=_F32)
        y = y + jnp.dot(rec_ref[rows, :], wo_ref[MIX_W:, :], preferred_element_type=_F32)
        o_ref[rows, :] = _layer_norm(ALPHA * x_ref[rows, :] + y, g_ref[...], b_ref[...])


def _out_ln(x2, mixed, recalled, wo, g, b, tm):
    n = x2.shape[0]
    return pl.pallas_call(
        _out_ln_kernel,
        grid=(n // tm,),
        in_specs=[pl.BlockSpec((tm, D_MODEL), lambda i: (i, 0)),
                  pl.BlockSpec((tm, MIX_W), lambda i: (i, 0)),
                  pl.BlockSpec((tm, XATTN_W), lambda i: (i, 0)),
                  pl.BlockSpec((D_MODEL, D_MODEL), lambda i: (0, 0)),
                  pl.BlockSpec((1, D_MODEL), lambda i: (0, 0)),
                  pl.BlockSpec((1, D_MODEL), lambda i: (0, 0))],
        out_specs=pl.BlockSpec((tm, D_MODEL), lambda i: (i, 0)),
        out_shape=jax.ShapeDtypeStruct((n, D_MODEL), _F32),
        compiler_params=_params("parallel"),
        name="out_ln",
    )(x2, mixed, recalled, wo, g, b)


def _dwconv_rows(h, w):
    n = h.shape[0]
    return (pltpu.roll(h, 1, 0) * w[0:1] + h * w[1:2] + pltpu.roll(h, n - 1, 0) * w[2:3])


def _ffn_kernel(x_ref, xp_ref, xn_ref, wu_ref, wc_ref, wd_ref, g_ref, b_ref, o_ref, gated_ref,
                *, ts, halo, fb):
    i = pl.program_id(1)
    prev = jnp.where(i == 0, 0.0, xp_ref[0])
    nxt = jnp.where(i == pl.num_programs(1) - 1, 0.0, xn_ref[0])
    xb = jnp.concatenate([prev, x_ref[0], nxt], axis=0).astype(_BF)
    for c in range(D_FF // fb):
        ca = slice(c * fb, (c + 1) * fb)
        cv = slice(D_FF + c * fb, D_FF + (c + 1) * fb)
        ha = jnp.dot(xb, wu_ref[:, ca], preferred_element_type=_F32)
        hv = jnp.dot(xb, wu_ref[:, cv], preferred_element_type=_F32)
        a = _dwconv_rows(ha, wc_ref[:, ca])[halo:halo + ts]
        v = _dwconv_rows(hv, wc_ref[:, cv])[halo:halo + ts]
        gated_ref[:, ca] = (_silu(a) * v).astype(_BF)
    for r0 in range(0, ts, FFN_OUT_ROWS):
        rows = slice(r0, r0 + FFN_OUT_ROWS)
        f = jnp.dot(gated_ref[rows, :], wd_ref[...], preferred_element_type=_F32)
        o_ref[0, rows, :] = _layer_norm(ALPHA * x_ref[0, rows, :] + f, g_ref[...], b_ref[...])


def _conv_ffn_ln(x3, w_up, w_conv, w_down, g, b, ts, fb):
    bsz, s, d = x3.shape
    halo = SUBLANES_BF16
    hb = ts // halo
    last = s // halo - 1
    kern = functools.partial(_ffn_kernel, ts=ts, halo=halo, fb=fb)
    resident = lambda shape: pl.BlockSpec(shape, lambda bi, i: (0, 0),
                                          pipeline_mode=pl.Buffered(1))
    return pl.pallas_call(
        kern,
        grid=(bsz, s // ts),
        in_specs=[pl.BlockSpec((1, ts, d), lambda bi, i: (bi, i, 0)),
                  pl.BlockSpec((1, halo, d), lambda bi, i: (bi, jnp.maximum(i * hb - 1, 0), 0)),
                  pl.BlockSpec((1, halo, d), lambda bi, i: (bi, jnp.minimum((i + 1) * hb, last), 0)),
                  resident((d, 2 * D_FF)),
                  resident((3, 2 * D_FF)),
                  resident((D_FF, d)),
                  pl.BlockSpec((1, d), lambda bi, i: (0, 0)),
                  pl.BlockSpec((1, d), lambda bi, i: (0, 0))],
        out_specs=pl.BlockSpec((1, ts, d), lambda bi, i: (bi, i, 0)),
        out_shape=jax.ShapeDtypeStruct((bsz, s, d), _F32),
        scratch_shapes=[pltpu.VMEM((ts, D_FF), _BF)],
        compiler_params=_params("parallel", "parallel"),
        name="conv_ffn",
    )(x3, x3, x3, w_up, w_conv, w_down, g, b)


def _fold_rows(x, op, group=SUBLANES_F32):
    acc = x[:group]
    for g in range(1, x.shape[0] // group):
        acc = op(acc, x[g * group:(g + 1) * group])
    while acc.shape[0] > SUBLANES_F32:
        half = acc.shape[0] // 2
        acc = op(acc[:half], acc[half:])
    return acc


def _attn_kernel(lam_ref, far_ref, q_ref, k_ref, vt_ref, bias_ref, subln_ref, o_ref,
                 s_a, s_b, acc_ref, *, tq, tk, out_scale):
    heads = q_ref.shape[2] // DA_W
    streams = [(hh, mi) for hh in range(heads) for mi in range(2)]
    n_str = len(streams)
    hp = pl.program_id(1)
    nkv = k_ref.shape[1] // tk
    nq = q_ref.shape[1] // tq
    n_chunks = tk // ATTN_ROW_CHUNK
    lane = lax.broadcasted_iota(jnp.int32, (1, heads * DA_W), 1)

    unit, u_min, u_max = _near_offsets(tq, ATTN_ROW_CHUNK)
    consts = [[far_ref[3 * (heads * hp + hh) + c] for c in range(3)] for hh in range(heads)]

    def stream_queries(i):
        q = q_ref[0, pl.ds(pl.multiple_of(i * tq, tq), tq), :]
        zero = jnp.zeros_like(q)
        return [jnp.where((lane >= hh * DA_W + mi * DA_HD) & (lane < hh * DA_W + (mi + 1) * DA_HD),
                          q, zero).T for hh, mi in streams]

    def side_select(i, j, r, left, mid, right):
        u = (tk * j + ATTN_ROW_CHUNK * r - tq * i) // unit
        return jnp.where(u < u_min, left, jnp.where(u > u_max, right, mid))

    def max_bound(i, j, r, hh):
        return side_select(i, j, r, consts[hh][0], consts[hh][2], consts[hh][1])

    def score_rows(qs, i, j, r, s_out, st):
        k0 = pl.multiple_of(j * tk + r * ATTN_ROW_CHUNK, ATTN_ROW_CHUNK)
        s = jnp.dot(k_ref[0, pl.ds(k0, ATTN_ROW_CHUNK), :], qs[st], preferred_element_type=_F32)
        s_out[st, r * ATTN_ROW_CHUNK:(r + 1) * ATTN_ROW_CHUNK, :] = s
        return _fold_rows(s, jnp.maximum) + max_bound(i, j, r, streams[st][0])

    def add_near_bias(i, j, s_out):
        assert unit == ATTN_ROW_CHUNK
        for u in range(u_min, u_max + 1):
            r = (tq // unit) * i + u - n_chunks * j

            @pl.when((r >= 0) & (r < n_chunks))
            def _():
                rows = pl.ds(pl.multiple_of(r * ATTN_ROW_CHUNK, ATTN_ROW_CHUNK), ATTN_ROW_CHUNK)
                for st, (hh, _) in enumerate(streams):
                    s_out[st, rows, :] = s_out[st, rows, :] + bias_ref[u - u_min, hh]

    def half_step(i, j, carry, s_cur, s_nxt, nxt):
        ni, nj, nqs = nxt
        tmax = [None] * n_str
        pv = [None] * n_str
        for r in range(n_chunks):
            rows = slice(r * ATTN_ROW_CHUNK, (r + 1) * ATTN_ROW_CHUNK)
            for st in range(n_str):
                t = score_rows(nqs, ni, nj, r, s_nxt, st)
                tmax[st] = t if r == 0 else jnp.maximum(tmax[st], t)
            v0 = pl.multiple_of(j * tk + r * ATTN_ROW_CHUNK, ATTN_ROW_CHUNK)
            for st, (hh, _) in enumerate(streams):
                shift = side_select(i, j, r, consts[hh][0], 0.0, consts[hh][1])
                p = jnp.exp2(s_cur[st, rows, :] - (carry[st][0] - shift)).astype(_BF)
                d = jnp.dot(vt_ref[0, hh, :, pl.ds(v0, ATTN_ROW_CHUNK)], p,
                            preferred_element_type=_F32)
                pv[st] = d if r == 0 else pv[st] + d
        new, tile_max = [], []
        for st in range(n_str):
            m, alpha = carry[st]
            acc_ref[st] = alpha * acc_ref[st] + pv[st]
            cand = jnp.max(tmax[st], axis=0, keepdims=True)
            m_nxt = jnp.maximum(m, cand)
            new.append((m_nxt, jnp.exp2(m - m_nxt)))
            tile_max.append(cand)
        add_near_bias(ni, nj, s_nxt)
        return tuple(new), tuple(tile_max)

    qs0 = stream_queries(0)
    first = []
    for st in range(n_str):
        parts = [score_rows(qs0, 0, 0, r, s_a, st) for r in range(n_chunks)]
        first.append(jnp.max(functools.reduce(jnp.maximum, parts), axis=0, keepdims=True))
    add_near_bias(0, 0, s_a)
    lam = lam_ref[0]

    def query_tile(i, m_first, bufs=(s_a, s_b)):
        qs = stream_queries(i)
        acc_ref[...] = jnp.zeros_like(acc_ref)
        carry = tuple((m_first[st], jnp.zeros((1, tq), _F32)) for st in range(n_str))
        i_next = jnp.minimum(i + 1, nq - 1)
        first_of_next = (i_next, 0, stream_queries(i_next))

        if nkv == 1:
            _, m_next = half_step(i, 0, carry, bufs[0], bufs[1], first_of_next)
        else:
            def pair(jj, carry):
                j = 2 * jj
                carry, _ = half_step(i, j, carry, s_a, s_b, (i, j + 1, qs))
                carry, _ = half_step(i, j + 1, carry, s_b, s_a, (i, j + 2, qs))
                return carry

            carry = lax.fori_loop(0, nkv // 2 - 1, pair, carry)
            carry, _ = half_step(i, nkv - 2, carry, s_a, s_b, (i, nkv - 1, qs))
            _, m_next = half_step(i, nkv - 1, carry, s_b, s_a, first_of_next)

        q0 = pl.multiple_of(i * tq, tq)
        for hh in range(heads):
            o0 = acc_ref[2 * hh, :DA_W] / acc_ref[2 * hh, DA_W:DA_W + 1]
            o1 = acc_ref[2 * hh + 1, :DA_W] / acc_ref[2 * hh + 1, DA_W:DA_W + 1]
            o = (o0 - lam * o1).T
            ms = jnp.mean(o * o, axis=-1, keepdims=True)
            o_ref[0, pl.ds(q0, tq), hh * DA_W:(hh + 1) * DA_W] = (
                o * lax.rsqrt(ms + LN_EPS) * subln_ref[...] * out_scale).astype(o_ref.dtype)
        return m_next

    if nkv == 1:
        def query_tile_pair(ii, m_first):
            m_first = query_tile(2 * ii, m_first, (s_a, s_b))
            return query_tile(2 * ii + 1, m_first, (s_b, s_a))

        lax.fori_loop(0, nq // 2, query_tile_pair, tuple(first))
    else:
        lax.fori_loop(0, nq, query_tile, tuple(first))


def _t5_bucket(rel):
    half = N_BUCKETS // 2
    max_exact = half // 2
    ret = jnp.where(rel > 0, half, 0)
    n = jnp.abs(rel)
    nf = jnp.maximum(n, 1).astype(_F32)
    large = max_exact + (jnp.log(nf / max_exact) / math.log(MAX_DISTANCE / max_exact)
                         * (half - max_exact)).astype(jnp.int32)
    large = jnp.minimum(large, half - 1)
    return ret + jnp.where(n < max_exact, n, large)


def _near_offsets(tq, tk):
    unit = math.gcd(tq, tk)
    first = -((tk + MAX_DISTANCE - 2) // unit)
    last = (tq + MAX_DISTANCE - 2) // unit
    return unit, first, last


def _bias_tiles(rel_bias, tq, tk):
    unit, u_min, u_max = _near_offsets(tq, tk)
    n_near = u_max - u_min + 1
    c = jnp.arange(tk, dtype=jnp.int32)[:, None]
    r = jnp.arange(tq, dtype=jnp.int32)[None, :]
    u = jnp.arange(u_min, u_max + 1, dtype=jnp.int32)[:, None, None]
    table = rel_bias.astype(_F32) * LOG2E
    bucket = _t5_bucket(u * unit + c - r)[:, None]
    tiles = jnp.zeros((n_near, DA_HEADS, tk, tq), _F32)
    for b in range(N_BUCKETS):
        tiles = jnp.where(bucket == b, table[b][None, :, None, None], tiles)
    far =table[_t5_bucket(jnp.array([-MAX_DISTANCE, MAX_DISTANCE], jnp.int32))]
    consts = jnp.concatenate([far, jnp.max(table, axis=0, keepdims=True)], axis=0)
    return tiles, consts.T.reshape(-1)


def _diff_attention(p3, vt, lam, subln, bias, far, layer, tq, tk):
    b, s, _ = p3.shape
    vrows = vt.shape[2]
    assert s % (2 * tk) == 0 or (tk == s and s % (2 * tq) == 0)
    lam_init = 0.8 - 0.6 * math.exp(-0.3 * layer)
    kern = functools.partial(_attn_kernel, tq=tq, tk=tk, out_scale=1.0 - lam_init)
    nh = ATTN_HEADS_PER_STEP
    groups = DA_HEADS // nh
    return pl.pallas_call(
        kern,
        grid=(b, groups),
        in_specs=[pl.BlockSpec(memory_space=pltpu.SMEM),
                  pl.BlockSpec(memory_space=pltpu.SMEM),
                  pl.BlockSpec((1, s, nh * DA_W), lambda bi, h: (bi, 0, h),
                               pipeline_mode=pl.Buffered(1)),
                  pl.BlockSpec((1, s, nh * DA_W), lambda bi, h: (bi, 0, groups + h),
                               pipeline_mode=pl.Buffered(1)),
                  pl.BlockSpec((1, nh, vrows, s), lambda bi, h: (bi, h, 0, 0),
                               pipeline_mode=pl.Buffered(1)),
                  pl.BlockSpec((bias.shape[0], nh) + bias.shape[2:], lambda bi, h: (0, h, 0, 0)),
                  pl.BlockSpec((1, DA_W), lambda bi, h: (0, 0))],
        out_specs=pl.BlockSpec((1, s, nh * DA_W), lambda bi, h: (bi, 0, h)),
        out_shape=jax.ShapeDtypeStruct((b, s, MIX_W), _BF),
        scratch_shapes=[pltpu.VMEM((2 * nh, tk, tq), _F32), pltpu.VMEM((2 * nh, tk, tq), _F32),
                        pltpu.VMEM((2 * nh, vrows, tq), _F32)],
        compiler_params=_params("parallel", "parallel"),
        name="diff_attn",
    )(lam, far, p3, p3, vt, bias, subln)


def _hgrn_chunk_stages(z, q, v, lb, st, tri, rev):
    c_rows = z.shape[0]
    e = jnp.exp(-jnp.abs(z))
    r = 1.0 / (1.0 + e)
    sig_pos, sig_neg = jnp.where(z >= 0, r, e * r), jnp.where(z >= 0, e * r, r)
    f = lb + (1.0 - lb) * sig_pos
    kk = (1.0 - lb) * sig_neg
    logf = jnp.log(f)
    hi = logf.astype(_BF)
    lo = (logf - hi.astype(_F32)).astype(_BF)
    cum = (jnp.dot(tri, hi, preferred_element_type=_F32)
           + jnp.dot(tri, lo, preferred_element_type=_F32))
    qh = _silu(q) * (HG_DK ** -0.5)
    vb = v.astype(_BF)
    yield None
    total = cum[0:1, :] if rev else cum[c_rows - 1:c_rows, :]
    o_inter = lax.dot_general((qh * jnp.exp(cum)).astype(_BF), st.astype(_BF), _NT,
                              preferred_element_type=_F32)
    ks = (kk * jnp.exp(total - cum)).astype(_BF)
    st_new = st * jnp.exp(total) + jnp.dot(v.T.astype(_BF), ks, preferred_element_type=_F32)
    blocks = []
    for b in range(c_rows // HG_SUB):
        lo_r, hi_r = b * HG_SUB, (b + 1) * HG_SUB
        if rev:
            cols = slice(lo_r, c_rows)
            base = cum[hi_r:hi_r + 1, :] if hi_r < c_rows else jnp.zeros_like(total)
        else:
            cols = slice(0, hi_r)
            base = cum[lo_r - 1:lo_r, :] if b else jnp.zeros_like(total)
        qq = (qh[lo_r:hi_r] * jnp.exp(cum[lo_r:hi_r] - base)).astype(_BF)
        kt = (kk[cols] * jnp.exp(base - cum[cols])).astype(_BF)
        a = lax.dot_general(qq, kt, _NT, preferred_element_type=_F32)
        blocks.append((lo_r, hi_r, cols, a))
    yield None
    parts = []
    for lo_r, hi_r, cols, a in blocks:
        sub_row = lax.broadcasted_iota(jnp.int32, a.shape, 0) + lo_r
        sub_col = lax.broadcasted_iota(jnp.int32, a.shape, 1) + cols.start
        seen = (sub_col >= sub_row) if rev else (sub_col <= sub_row)
        a = jnp.where(seen, a, 0.0)
        parts.append(o_inter[lo_r:hi_r]
                     + jnp.dot(a.astype(_BF), vb[cols], preferred_element_type=_F32))
    yield jnp.concatenate(parts, axis=0), st_new


def _hgrn_kernel(qf_ref, vf_ref, zf_ref, qb_ref, vb_ref, zb_ref, lb_ref, of_ref, ob_ref, st_ref,
                 *, rows):
    @pl.when(pl.program_id(1) == 0)
    def _():
        st_ref[...] = jnp.zeros_like(st_ref)

    c_rows = HG_CHUNK
    n_chunks = rows // c_rows
    row = lax.broadcasted_iota(jnp.int32, (c_rows, c_rows), 0)
    col = lax.broadcasted_iota(jnp.int32, (c_rows, c_rows), 1)
    tris = ((col <= row).astype(_BF), (col >= row).astype(_BF))
    dirs = ((qf_ref, vf_ref, zf_ref, of_ref), (qb_ref, vb_ref, zb_ref, ob_ref))

    def chunk(c, carry):
        starts = (pl.multiple_of(c * c_rows, c_rows),
                  pl.multiple_of((n_chunks - 1 - c) * c_rows, c_rows))
        for h0 in range(0, HG_HEADS, HG_HEAD_GROUP):
            chains = []
            for h in range(h0, h0 + HG_HEAD_GROUP):
                cs = slice(h * HG_DK, (h + 1) * HG_DK)
                for d, (q_ref, v_ref, z_ref, o_ref) in enumerate(dirs):
                    rs = pl.ds(starts[d], c_rows)
                    gen = _hgrn_chunk_stages(z_ref[0, rs, cs], q_ref[0, rs, cs], v_ref[0, rs, cs],
                                             lb_ref[d, :, cs], st_ref[d, h], tris[d], rev=bool(d))
                    chains.append((gen, o_ref, rs, cs, d, h))
            for _ in range(2):
                for chain in chains:
                    next(chain[0])
            for gen, o_ref, rs, cs, d, h in chains:
                o, st = next(gen)
                o_ref[0, rs, cs] = o
                st_ref[d, h] = st
        return carry

    lax.fori_loop(0, n_chunks, chunk, 0)


def _hgrn_scan(p3, lb, rows):
    b, s, _ = p3.shape
    w = MIX_W
    t = s // rows
    fwd = lambda c: pl.BlockSpec((1, rows, w), lambda bi, i: (bi, i, c))
    bwd = lambda c: pl.BlockSpec((1, rows, w), lambda bi, i: (bi, t - 1 - i, c))
    out = jax.ShapeDtypeStruct((b, s, w), _F32)
    return pl.pallas_call(
        functools.partial(_hgrn_kernel, rows=rows),
        grid=(b, t),
        in_specs=[fwd(0), fwd(1), fwd(3), bwd(0), bwd(1), bwd(4),
                  pl.BlockSpec((2, 1, w), lambda bi, i: (0, 0, 0))],
        out_specs=[fwd(0), bwd(0)],
        out_shape=[out, out],
        scratch_shapes=[pltpu.VMEM((2, HG_HEADS, HG_DK, HG_DK), _F32)],
        compiler_params=_params("parallel", "arbitrary"),
        name="hgrn_scan",
    )(p3, p3, p3, p3, p3, p3, lb)


def _hgrn_finish_kernel(of_ref, ob_ref, g_ref, w_ref, o_ref):
    o = of_ref[...] + ob_ref[...]
    gate = _silu(g_ref[...])
    w = w_ref[...]
    for h in range(HG_HEADS):
        cs = slice(h * HG_DK, (h + 1) * HG_DK)
        oh = o[:, cs]
        ms = jnp.mean(oh * oh, axis=-1, keepdims=True)
        o_ref[:, cs] = (oh * lax.rsqrt(ms + LN_EPS) * w[:, cs] * gate[:, cs]).astype(o_ref.dtype)


def _hgrn_finish(o_fw, o_bw, p2, g_col, norm_w, tm):
    n = o_fw.shape[0]
    return pl.pallas_call(
        _hgrn_finish_kernel,
        grid=(n // tm,),
        in_specs=[pl.BlockSpec((tm, MIX_W), lambda i: (i, 0)),
                  pl.BlockSpec((tm, MIX_W), lambda i: (i, 0)),
                  pl.BlockSpec((tm, MIX_W), lambda i: (i, g_col)),
                  pl.BlockSpec((1, MIX_W), lambda i: (0, 0))],
        out_specs=pl.BlockSpec((tm, MIX_W), lambda i: (i, 0)),
        out_shape=jax.ShapeDtypeStruct((n, MIX_W), _BF),
        compiler_params=_params("parallel"),
        name="hgrn_finish",
    )(o_fw, o_bw, p2, norm_w)


def _sconv_kernel(gb_ref, gc_ref, h_ref, gcp_ref, hp_ref, gcn_ref, hn_ref, w_ref, o_ref, *, halo):
    i = pl.program_id(1)
    u = gc_ref[0].astype(_F32) * h_ref[0].astype(_F32)
    n = u.shape[0]
    u_prev = (gcp_ref[0].astype(_F32) * hp_ref[0].astype(_F32))[halo - 1:halo]
    u_next = (gcn_ref[0].astype(_F32) * hn_ref[0].astype(_F32))[0:1]
    u_prev = jnp.where(i == 0, 0.0, u_prev)
    u_next = jnp.where(i == pl.num_programs(1) - 1, 0.0, u_next)
    row = lax.broadcasted_iota(jnp.int32, u.shape, 0)
    down = jnp.where(row == 0, u_prev, pltpu.roll(u, 1, 0))
    up = jnp.where(row == n - 1, u_next, pltpu.roll(u, n - 1, 0))
    w = w_ref[...]
    y = down * w[0:1] + u * w[1:2] + up * w[2:3]
    o_ref[0] = (gb_ref[0].astype(_F32) * y).astype(o_ref.dtype)


def _short_conv(p3, conv_w, ts):
    b, s, _ = p3.shape
    halo = SUBLANES_BF16
    hb = ts // halo
    last = s // halo - 1
    main = lambda c: pl.BlockSpec((1, ts, MIX_W), lambda bi, i: (bi, i, c))
    prev = lambda c: pl.BlockSpec((1, halo, MIX_W), lambda bi, i: (bi, jnp.maximum(i * hb - 1, 0), c))
    nxt = lambda c: pl.BlockSpec((1, halo, MIX_W), lambda bi, i: (bi, jnp.minimum((i + 1) * hb, last), c))
    return pl.pallas_call(
        functools.partial(_sconv_kernel, halo=halo),
        grid=(b, s // ts),
        in_specs=[main(0), main(1), main(2), prev(1), prev(2), nxt(1), nxt(2),
                  pl.BlockSpec((3, MIX_W), lambda bi, i: (0, 0))],
        out_specs=pl.BlockSpec((1, ts, MIX_W), lambda bi, i: (bi, i, 0)),
        out_shape=jax.ShapeDtypeStruct((b, s, MIX_W), _BF),
        compiler_params=_params("parallel", "parallel"),
        name="short_conv",
    )(p3, p3, p3, p3, p3, p3, p3, conv_w)


def _tile(n, pref):
    return pref if n % pref == 0 else n


def kernel(x, mem, rel_bias, attn_w_in, attn_lambda, attn_subln, hgrn_w_in, hgrn_lower_bound,
           hgrn_norm, conv_w_in, conv_w, mem_w_kv, w_o, ln_gain, ln_bias, ffn_w_up, ffn_conv,
           ffn_w_down):
    b, s, d = x.shape
    n = b * s
    m = mem.shape[1]
    tm = _tile(s, ROW_TILE)
    hg_rows = _tile(s, HG_ROWS)

    bias, far = _bias_tiles(rel_bias, ATTN_Q_TILE, ATTN_ROW_CHUNK)
    mem2 = mem.reshape(b * m, d)
    lbw = jax.nn.softmax(hgrn_lower_bound.astype(_F32), axis=0)
    lb_all = jnp.cumsum(lbw, axis=0) - lbw[0]

    for layer in range(DEPTH):
        kind, j = layer % N_MIXERS, layer // N_MIXERS
        x2 = x.reshape(n, d)
        kv = _matmul(mem2, mem_w_kv[layer].astype(_BF), _BF, _tile(b * m, ROW_TILE))
        kv = kv.reshape(b, m, 2 * XATTN_W)
        if kind == 0:
            q_scale = jnp.where(jnp.arange(attn_w_in.shape[2]) < MIX_W, DA_HD ** -0.5 * LOG2E, 1.0)
            w_in = (attn_w_in[j] * q_scale).astype(_BF)
            p = _matmul(x2, w_in, _BF, tm).reshape(b, s, -1)
            lp = attn_lambda[j].astype(_F32)
            lam_init = 0.8 - 0.6 * math.exp(-0.3 * layer)
            lam = jnp.exp(jnp.sum(lp[0] * lp[1])) - jnp.exp(jnp.sum(lp[2] * lp[3])) + lam_init
            vt = jnp.swapaxes(p[..., 2 * MIX_W:3 * MIX_W], 1, 2).reshape(b, DA_HEADS, DA_W, s)
            vt = jnp.concatenate([vt, jnp.ones((b, DA_HEADS, SUBLANES_BF16, s), _BF)], axis=2)
            mixed = _diff_attention(p, vt, lam.reshape(1),
                                    attn_subln[j].astype(_F32).reshape(1, DA_W),
                                    bias, far, layer, ATTN_Q_TILE, ATTN_K_TILE)
            q_col = 3 * MIX_W // XATTN_W
        elif kind == 1:
            w_in = hgrn_w_in[j].astype(_BF)
            p = _matmul(x2, w_in, _F32, tm).reshape(b, s, -1)
            o_fw, o_bw = _hgrn_scan(p, lb_all[layer].reshape(2, 1, MIX_W), hg_rows)
            mixed = _hgrn_finish(o_fw.reshape(n, MIX_W), o_bw.reshape(n, MIX_W),
                                 p.reshape(n, -1), 2,
                                 hgrn_norm[j].astype(_F32).reshape(1, MIX_W), tm)
            q_col = 5 * MIX_W // XATTN_W
        else:
            w_in = conv_w_in[j].astype(_BF)
            p = _matmul(x2, w_in, _BF, tm).reshape(b, s, -1)
            mixed = _short_conv(p, conv_w[j].astype(_F32), tm)
            q_col = 3 * MIX_W // XATTN_W
        recalled = _memory_attention(p, kv, q_col, tm)
        x2 = _out_ln(x2, mixed.reshape(n, MIX_W), recalled.reshape(n, XATTN_W),
                     w_o[layer].astype(_BF), ln_gain[layer, 0].reshape(1, d),
                     ln_bias[layer, 0].reshape(1, d), tm)
        x = _conv_ffn_ln(x2.reshape(b, s, d), ffn_w_up[layer].astype(_BF),
                         ffn_conv[layer].astype(_F32), ffn_w_down[layer].astype(_BF),
                         ln_gain[layer, 1].reshape(1, d), ln_bias[layer, 1].reshape(1, d),
                         tm, FFN_TILE)
    return x
```

```python
import functools
import math

import jax
import jax.numpy as jnp
from jax import lax
from jax.experimental import pallas as pl
from jax.experimental.pallas import tpu as pltpu

D_MODEL = 1024
DEPTH = 4
N_MIXERS = 3
MIX_W = 3 * D_MODEL // 4
XATTN_HEADS = 4
XATTN_W = D_MODEL - MIX_W
XATTN_HD = XATTN_W // XATTN_HEADS
DA_HD = 64
DA_HEADS = MIX_W // (2 * DA_HD)
DA_W = 2 * DA_HD
N_BUCKETS = 32
MAX_DISTANCE = 128
HG_DK = 128
HG_HEADS = MIX_W // HG_DK
HG_CHUNK = 64
HG_SUB = 16
D_FF = ((8 * D_MODEL // 3 + 255) // 256) * 256
LN_EPS = 1e-5
ALPHA = (2 * DEPTH) ** 0.25
LOG2E = math.log2(math.e)

LANES = 128
SUBLANES_F32 = 8
SUBLANES_BF16 = 16
VMEM_LIMIT = 52 * 1024 * 1024

ROW_TILE = 512
ATTN_Q_TILE = 256
ATTN_K_TILE = 8192
ATTN_HEADS_PER_STEP = 1
ATTN_ROW_CHUNK = 256
MXU_WIDTH = 256
FFN_TILE = MXU_WIDTH
FFN_OUT_ROWS = 256
HG_ROWS = 512
HG_HEAD_GROUP = 6

_NEG = -1e30
_BF = jnp.bfloat16
_F32 = jnp.float32
_NT = (((1,), (1,)), ((), ()))


def _params(*sem):
    return pltpu.CompilerParams(dimension_semantics=sem, vmem_limit_bytes=VMEM_LIMIT)


def _sigmoid(z):
    e = jnp.exp(-jnp.abs(z))
    r = 1.0 / (1.0 + e)
    return jnp.where(z >= 0, r, e * r)


def _silu(z):
    return z * _sigmoid(z)


def _layer_norm(r, g, b):
    mu = jnp.mean(r, axis=-1, keepdims=True)
    rc = r - mu
    var = jnp.mean(rc * rc, axis=-1, keepdims=True)
    return rc * lax.rsqrt(var + LN_EPS) * g + b


def _mm_kernel(x_ref, w_ref, o_ref):
    o_ref[...] = jnp.dot(x_ref[...].astype(_BF), w_ref[...],
                         preferred_element_type=_F32).astype(o_ref.dtype)


def _matmul(x, w, out_dtype, tm):
    m, k = x.shape
    n = w.shape[1]
    return pl.pallas_call(
        _mm_kernel,
        grid=(m // tm,),
        in_specs=[pl.BlockSpec((tm, k), lambda i: (i, 0)),
                  pl.BlockSpec((k, n), lambda i: (0, 0), pipeline_mode=pl.Buffered(1))],
        out_specs=pl.BlockSpec((tm, n), lambda i: (i, 0)),
        out_shape=jax.ShapeDtypeStruct((m, n), out_dtype),
        compiler_params=_params("parallel"),
        name="proj",
    )(x, w)


def _memattn_kernel(q_ref, km_ref, vm_ref, o_ref):
    q = q_ref[0].astype(_BF)
    km = km_ref[0]
    vm = vm_ref[0]
    lane = lax.broadcasted_iota(jnp.int32, (1, XATTN_W), 1)
    acc = jnp.zeros(q.shape, _F32)
    for h in range(XATTN_HEADS):
        head = (lane >= h * XATTN_HD) & (lane < (h + 1) * XATTN_HD)
        qh = jnp.where(head, q, jnp.zeros_like(q))
        s = lax.dot_general(qh, km, _NT, preferred_element_type=_F32) * (XATTN_HD ** -0.5)
        p = jnp.exp(s - jnp.max(s, axis=-1, keepdims=True))
        l = jnp.sum(p, axis=-1, keepdims=True)
        vh = jnp.where(head, vm, jnp.zeros_like(vm))
        acc = acc + jnp.dot(p.astype(_BF), vh, preferred_element_type=_F32) / l
    o_ref[0] = acc.astype(o_ref.dtype)


def _memory_attention(p3, kv, q_col, tm):
    b, s, _ = p3.shape
    m = kv.shape[1]
    return pl.pallas_call(
        _memattn_kernel,
        grid=(b, s // tm),
        in_specs=[pl.BlockSpec((1, tm, XATTN_W), lambda bi, i: (bi, i, q_col)),
                  pl.BlockSpec((1, m, XATTN_W), lambda bi, i: (bi, 0, 0)),
                  pl.BlockSpec((1, m, XATTN_W), lambda bi, i: (bi, 0, 1))],
        out_specs=pl.BlockSpec((1, tm, XATTN_W), lambda bi, i: (bi, i, 0)),
        out_shape=jax.ShapeDtypeStruct((b, s, XATTN_W), _BF),
        compiler_params=_params("parallel", "parallel"),
        name="memattn",
    )(p3, kv, kv)


def _out_ln_kernel(x_ref, mix_ref, rec_ref, wo_ref, g_ref, b_ref, o_ref):
    for r0 in range(0, x_ref.shape[0], FFN_OUT_ROWS):
        rows = slice(r0, r0 + FFN_OUT_ROWS)
        y = jnp.dot(mix_ref[rows, :], wo_ref[:MIX_W, :], preferred_element_type=_F32)
        y = y + jnp.dot(rec_ref[rows, :], wo_ref[MIX_W:, :], preferred_element_type=_F32)
        o_ref[rows, :] = _layer_norm(ALPHA * x_ref[rows, :] + y, g_ref[...], b_ref[...])


def _out_ln(x2, mixed, recalled, wo, g, b, tm):
    n = x2.shape[0]
    return pl.pallas_call(
        _out_ln_kernel,
        grid=(n // tm,),
        in_specs=[pl.BlockSpec((tm, D_MODEL), lambda i: (i, 0)),
                  pl.BlockSpec((tm, MIX_W), lambda i: (i, 0)),
                  pl.BlockSpec((tm, XATTN_W), lambda i: (i, 0)),
                  pl.BlockSpec((D_MODEL, D_MODEL), lambda i: (0, 0)),
                  pl.BlockSpec((1, D_MODEL), lambda i: (0, 0)),
                  pl.BlockSpec((1, D_MODEL), lambda i: (0, 0))],
        out_specs=pl.BlockSpec((tm, D_MODEL), lambda i: (i, 0)),
        out_shape=jax.ShapeDtypeStruct((n, D_MODEL), _F32),
        compiler_params=_params("parallel"),
        name="out_ln",
    )(x2, mixed, recalled, wo, g, b)


def _dwconv_rows(h, w):
    n = h.shape[0]
    return (pltpu.roll(h, 1, 0) * w[0:1] + h * w[1:2] + pltpu.roll(h, n - 1, 0) * w[2:3])


def _ffn_kernel(x_ref, xp_ref, xn_ref, wu_ref, wc_ref, wd_ref, g_ref, b_ref, o_ref, gated_ref,
                *, ts, halo, fb):
    i = pl.program_id(1)
    prev = jnp.where(i == 0, 0.0, xp_ref[0])
    nxt = jnp.where(i == pl.num_programs(1) - 1, 0.0, xn_ref[0])
    xb = jnp.concatenate([prev, x_ref[0], nxt], axis=0).astype(_BF)
    for c in range(D_FF // fb):
        ca = slice(c * fb, (c + 1) * fb)
        cv = slice(D_FF + c * fb, D_FF + (c + 1) * fb)
        ha = jnp.dot(xb, wu_ref[:, ca], preferred_element_type=_F32)
        hv = jnp.dot(xb, wu_ref[:, cv], preferred_element_type=_F32)
        a = _dwconv_rows(ha, wc_ref[:, ca])[halo:halo + ts]
        v = _dwconv_rows(hv, wc_ref[:, cv])[halo:halo + ts]
        gated_ref[:, ca] = (_silu(a) * v).astype(_BF)
    for r0 in range(0, ts, FFN_OUT_ROWS):
        rows = slice(r0, r0 + FFN_OUT_ROWS)
        f = jnp.dot(gated_ref[rows, :], wd_ref[...], preferred_element_type=_F32)
        o_ref[0, rows, :] = _layer_norm(ALPHA * x_ref[0, rows, :] + f, g_ref[...], b_ref[...])


def _conv_ffn_ln(x3, w_up, w_conv, w_down, g, b, ts, fb):
    bsz, s, d = x3.shape
    halo = SUBLANES_BF16
    hb = ts // halo
    last = s // halo - 1
    kern = functools.partial(_ffn_kernel, ts=ts, halo=halo, fb=fb)
    resident = lambda shape: pl.BlockSpec(shape, lambda bi, i: (0, 0),
                                          pipeline_mode=pl.Buffered(1))
    return pl.pallas_call(
        kern,
        grid=(bsz, s // ts),
        in_specs=[pl.BlockSpec((1, ts, d), lambda bi, i: (bi, i, 0)),
                  pl.BlockSpec((1, halo, d), lambda bi, i: (bi, jnp.maximum(i * hb - 1, 0), 0)),
                  pl.BlockSpec((1, halo, d), lambda bi, i: (bi, jnp.minimum((i + 1) * hb, last), 0)),
                  resident((d, 2 * D_FF)),
                  resident((3, 2 * D_FF)),
                  resident((D_FF, d)),
                  pl.BlockSpec((1, d), lambda bi, i: (0, 0)),
                  pl.BlockSpec((1, d), lambda bi, i: (0, 0))],
        out_specs=pl.BlockSpec((1, ts, d), lambda bi, i: (bi, i, 0)),
        out_shape=jax.ShapeDtypeStruct((bsz, s, d), _F32),
        scratch_shapes=[pltpu.VMEM((ts, D_FF), _BF)],
        compiler_params=_params("parallel", "parallel"),
        name="conv_ffn",
    )(x3, x3, x3, w_up, w_conv, w_down, g, b)


def _fold_rows(x, op, group=SUBLANES_F32):
    acc = x[:group]
    for g in range(1, x.shape[0] // group):
        acc = op(acc, x[g * group:(g + 1) * group])
    while acc.shape[0] > SUBLANES_F32:
        half = acc.shape[0] // 2
        acc = op(acc[:half], acc[half:])
    return acc


def _attn_kernel(lam_ref, far_ref, q_ref, k_ref, vt_ref, bias_ref, subln_ref, o_ref,
                 s_a, s_b, acc_ref, *, tq, tk, out_scale):
    heads = q_ref.shape[2] // DA_W
    streams = [(hh, mi) for hh in range(heads) for mi in range(2)]
    n_str = len(streams)
    hp = pl.program_id(1)
    nkv = k_ref.shape[1] // tk
    nq = q_ref.shape[1] // tq
    n_chunks = tk // ATTN_ROW_CHUNK
    lane = lax.broadcasted_iota(jnp.int32, (1, heads * DA_W), 1)

    unit, u_min, u_max = _near_offsets(tq, ATTN_ROW_CHUNK)
    consts = [[far_ref[3 * (heads * hp + hh) + c] for c in range(3)] for hh in range(heads)]

    def stream_queries(i):
        q = q_ref[0, pl.ds(pl.multiple_of(i * tq, tq), tq), :]
        zero = jnp.zeros_like(q)
        return [jnp.where((lane >= hh * DA_W + mi * DA_HD) & (lane < hh * DA_W + (mi + 1) * DA_HD),
                          q, zero).T for hh, mi in streams]

    def side_select(i, j, r, left, mid, right):
        u = (tk * j + ATTN_ROW_CHUNK * r - tq * i) // unit
        return jnp.where(u < u_min, left, jnp.where(u > u_max, right, mid))

    def max_bound(i, j, r, hh):
        return side_select(i, j, r, consts[hh][0], consts[hh][2], consts[hh][1])

    def score_rows(qs, i, j, r, s_out, st):
        k0 = pl.multiple_of(j * tk + r * ATTN_ROW_CHUNK, ATTN_ROW_CHUNK)
        s = jnp.dot(k_ref[0, pl.ds(k0, ATTN_ROW_CHUNK), :], qs[st], preferred_element_type=_F32)
        s_out[st, r * ATTN_ROW_CHUNK:(r + 1) * ATTN_ROW_CHUNK, :] = s
        return _fold_rows(s, jnp.maximum) + max_bound(i, j, r, streams[st][0])

    def add_near_bias(i, j, s_out):
        assert unit == ATTN_ROW_CHUNK
        for u in range(u_min, u_max + 1):
            r = (tq // unit) * i + u - n_chunks * j

            @pl.when((r >= 0) & (r < n_chunks))
            def _():
                rows = pl.ds(pl.multiple_of(r * ATTN_ROW_CHUNK, ATTN_ROW_CHUNK), ATTN_ROW_CHUNK)
                for st, (hh, _) in enumerate(streams):
                    s_out[st, rows, :] = s_out[st, rows, :] + bias_ref[u - u_min, hh]

    def half_step(i, j, carry, s_cur, s_nxt, nxt):
        ni, nj, nqs = nxt
        tmax = [None] * n_str
        pv = [None] * n_str
        for r in range(n_chunks):
            rows = slice(r * ATTN_ROW_CHUNK, (r + 1) * ATTN_ROW_CHUNK)
            for st in range(n_str):
                t = score_rows(nqs, ni, nj, r, s_nxt, st)
                tmax[st] = t if r == 0 else jnp.maximum(tmax[st], t)
            v0 = pl.multiple_of(j * tk + r * ATTN_ROW_CHUNK, ATTN_ROW_CHUNK)
            for st, (hh, _) in enumerate(streams):
                shift = side_select(i, j, r, consts[hh][0], 0.0, consts[hh][1])
                p = jnp.exp2(s_cur[st, rows, :] - (carry[st][0] - shift)).astype(_BF)
                d = jnp.dot(vt_ref[0, hh, :, pl.ds(v0, ATTN_ROW_CHUNK)], p,
                            preferred_element_type=_F32)
                pv[st] = d if r == 0 else pv[st] + d
        new, tile_max = [], []
        for st in range(n_str):
            m, alpha = carry[st]
            acc_ref[st] = pv[st] if nkv == 1 else alpha * acc_ref[st] + pv[st]
            cand = jnp.max(tmax[st], axis=0, keepdims=True)
            m_nxt = jnp.maximum(m, cand)
            new.append((m_nxt, jnp.exp2(m - m_nxt)))
            tile_max.append(cand)
        add_near_bias(ni, nj, s_nxt)
        return tuple(new), tuple(tile_max)

    qs0 = stream_queries(0)
    first = []
    for st in range(n_str):
        parts = [score_rows(qs0, 0, 0, r, s_a, st) for r in range(n_chunks)]
        first.append(jnp.max(functools.reduce(jnp.maximum, parts), axis=0, keepdims=True))
    add_near_bias(0, 0, s_a)
    lam = lam_ref[0]

    def query_tile(i, m_first, bufs=(s_a, s_b)):
        qs = stream_queries(i)
        if nkv > 1:
            acc_ref[...] = jnp.zeros_like(acc_ref)
        carry = tuple((m_first[st], jnp.zeros((1, tq), _F32)) for st in range(n_str))
        i_next = jnp.minimum(i + 1, nq - 1)
        first_of_next = (i_next, 0, stream_queries(i_next))

        if nkv == 1:
            _, m_next = half_step(i, 0, carry, bufs[0], bufs[1], first_of_next)
        else:
            def pair(jj, carry):
                j = 2 * jj
                carry, _ = half_step(i, j, carry, s_a, s_b, (i, j + 1, qs))
                carry, _ = half_step(i, j + 1, carry, s_b, s_a, (i, j + 2, qs))
                return carry

            carry = lax.fori_loop(0, nkv // 2 - 1, pair, carry)
            carry, _ = half_step(i, nkv - 2, carry, s_a, s_b, (i, nkv - 1, qs))
            _, m_next = half_step(i, nkv - 1, carry, s_b, s_a, first_of_next)

        q0 = pl.multiple_of(i * tq, tq)
        for hh in range(heads):
            o0 = acc_ref[2 * hh, :DA_W] / acc_ref[2 * hh, DA_W:DA_W + 1]
            o1 = acc_ref[2 * hh + 1, :DA_W] / acc_ref[2 * hh + 1, DA_W:DA_W + 1]
            o = (o0 - lam * o1).T
            ms = jnp.mean(o * o, axis=-1, keepdims=True)
            o_ref[0, pl.ds(q0, tq), hh * DA_W:(hh + 1) * DA_W] = (
                o * lax.rsqrt(ms + LN_EPS) * subln_ref[...] * out_scale).astype(o_ref.dtype)
        return m_next

    if nkv == 1:
        def query_tile_pair(ii, m_first):
            m_first = query_tile(2 * ii, m_first, (s_a, s_b))
            return query_tile(2 * ii + 1, m_first, (s_b, s_a))

        lax.fori_loop(0, nq // 2, query_tile_pair, tuple(first))
    else:
        lax.fori_loop(0, nq, query_tile, tuple(first))


def _t5_bucket(rel):
    half = N_BUCKETS // 2
    max_exact = half // 2
    ret = jnp.where(rel > 0, half, 0)
    n = jnp.abs(rel)
    nf = jnp.maximum(n, 1).astype(_F32)
    large = max_exact + (jnp.log(nf / max_exact) / math.log(MAX_DISTANCE / max_exact)
                         * (half - max_exact)).astype(jnp.int32)
    large = jnp.minimum(large, half - 1)
    return ret + jnp.where(n < max_exact, n, large)


def _near_offsets(tq, tk):
    unit = math.gcd(tq, tk)
    first = -((tk + MAX_DISTANCE - 2) // unit)
    last = (tq + MAX_DISTANCE - 2) // unit
    return unit, first, last


def _bias_tiles(rel_bias, tq, tk):
    unit, u_min, u_max = _near_offsets(tq, tk)
    n_near = u_max - u_min + 1
    c = jnp.arange(tk, dtype=jnp.int32)[:, None]
    r = jnp.arange(tq, dtype=jnp.int32)[None, :]
    u = jnp.arange(u_min, u_max + 1, dtype=jnp.int32)[:, None, None]
    table = rel_bias.astype(_F32) * LOG2E
    bucket = _t5_bucket(u * unit + c - r)[:, None]
    tiles = jnp.zeros((n_near, DA_HEADS, tk, tq), _F32)
    for b in range(N_BUCKETS):
        tiles = jnp.where(bucket == b, table[b][None, :, None, None], tiles)
    far =table[_t5_bucket(jnp.array([-MAX_DISTANCE, MAX_DISTANCE], jnp.int32))]
    consts = jnp.concatenate([far, jnp.max(table, axis=0, keepdims=True)], axis=0)
    return tiles, consts.T.reshape(-1)


def _diff_attention(p3, vt, lam, subln, bias, far, layer, tq, tk):
    b, s, _ = p3.shape
    vrows = vt.shape[2]
    assert s % (2 * tk) == 0 or (tk == s and s % (2 * tq) == 0)
    lam_init = 0.8 - 0.6 * math.exp(-0.3 * layer)
    kern = functools.partial(_attn_kernel, tq=tq, tk=tk, out_scale=1.0 - lam_init)
    nh = ATTN_HEADS_PER_STEP
    groups = DA_HEADS // nh
    return pl.pallas_call(
        kern,
        grid=(b, groups),
        in_specs=[pl.BlockSpec(memory_space=pltpu.SMEM),
                  pl.BlockSpec(memory_space=pltpu.SMEM),
                  pl.BlockSpec((1, s, nh * DA_W), lambda bi, h: (bi, 0, h),
                               pipeline_mode=pl.Buffered(1)),
                  pl.BlockSpec((1, s, nh * DA_W), lambda bi, h: (bi, 0, groups + h),
                               pipeline_mode=pl.Buffered(1)),
                  pl.BlockSpec((1, nh, vrows, s), lambda bi, h: (bi, h, 0, 0),
                               pipeline_mode=pl.Buffered(1)),
                  pl.BlockSpec((bias.shape[0], nh) + bias.shape[2:], lambda bi, h: (0, h, 0, 0)),
                  pl.BlockSpec((1, DA_W), lambda bi, h: (0, 0))],
        out_specs=pl.BlockSpec((1, s, nh * DA_W), lambda bi, h: (bi, 0, h)),
        out_shape=jax.ShapeDtypeStruct((b, s, MIX_W), _BF),
        scratch_shapes=[pltpu.VMEM((2 * nh, tk, tq), _F32), pltpu.VMEM((2 * nh, tk, tq), _F32),
                        pltpu.VMEM((2 * nh, vrows, tq), _F32)],
        compiler_params=_params("parallel", "parallel"),
        name="diff_attn",
    )(lam, far, p3, p3, vt, bias, subln)


def _hgrn_chunk_stages(z, q, v, lb, st, tri, rev):
    c_rows = z.shape[0]
    e = jnp.exp(-jnp.abs(z))
    r = 1.0 / (1.0 + e)
    sig_pos, sig_neg = jnp.where(z >= 0, r, e * r), jnp.where(z >= 0, e * r, r)
    f = lb + (1.0 - lb) * sig_pos
    kk = (1.0 - lb) * sig_neg
    logf = jnp.log(f)
    hi = logf.astype(_BF)
    lo = (logf - hi.astype(_F32)).astype(_BF)
    cum = (jnp.dot(tri, hi, preferred_element_type=_F32)
           + jnp.dot(tri, lo, preferred_element_type=_F32))
    qh = _silu(q) * (HG_DK ** -0.5)
    vb = v.astype(_BF)
    yield None
    total = cum[0:1, :] if rev else cum[c_rows - 1:c_rows, :]
    o_inter = lax.dot_general((qh * jnp.exp(cum)).astype(_BF), st.astype(_BF), _NT,
                              preferred_element_type=_F32)
    ks = (kk * jnp.exp(total - cum)).astype(_BF)
    st_new = st * jnp.exp(total) + jnp.dot(v.T.astype(_BF), ks, preferred_element_type=_F32)
    blocks = []
    for b in range(c_rows // HG_SUB):
        lo_r, hi_r = b * HG_SUB, (b + 1) * HG_SUB
        if rev:
            cols = slice(lo_r, c_rows)
            base = cum[hi_r:hi_r + 1, :] if hi_r < c_rows else jnp.zeros_like(total)
        else:
            cols = slice(0, hi_r)
            base = cum[lo_r - 1:lo_r, :] if b else jnp.zeros_like(total)
        qq = (qh[lo_r:hi_r] * jnp.exp(cum[lo_r:hi_r] - base)).astype(_BF)
        kt = (kk[cols] * jnp.exp(base - cum[cols])).astype(_BF)
        a = lax.dot_general(qq, kt, _NT, preferred_element_type=_F32)
        blocks.append((lo_r, hi_r, cols, a))
    yield None
    parts = []
    for lo_r, hi_r, cols, a in blocks:
        sub_row = lax.broadcasted_iota(jnp.int32, a.shape, 0) + lo_r
        sub_col = lax.broadcasted_iota(jnp.int32, a.shape, 1) + cols.start
        seen = (sub_col >= sub_row) if rev else (sub_col <= sub_row)
        a = jnp.where(seen, a, 0.0)
        parts.append(o_inter[lo_r:hi_r]
                     + jnp.dot(a.astype(_BF), vb[cols], preferred_element_type=_F32))
    yield jnp.concatenate(parts, axis=0), st_new


def _hgrn_kernel(qf_ref, vf_ref, zf_ref, qb_ref, vb_ref, zb_ref, lb_ref, of_ref, ob_ref, st_ref,
                 *, rows):
    @pl.when(pl.program_id(1) == 0)
    def _():
        st_ref[...] = jnp.zeros_like(st_ref)

    c_rows = HG_CHUNK
    n_chunks = rows // c_rows
    row = lax.broadcasted_iota(jnp.int32, (c_rows, c_rows), 0)
    col = lax.broadcasted_iota(jnp.int32, (c_rows, c_rows), 1)
    tris = ((col <= row).astype(_BF), (col >= row).astype(_BF))
    dirs = ((qf_ref, vf_ref, zf_ref, of_ref), (qb_ref, vb_ref, zb_ref, ob_ref))

    def chunk(c, carry):
        starts = (pl.multiple_of(c * c_rows, c_rows),
                  pl.multiple_of((n_chunks - 1 - c) * c_rows, c_rows))
        for h0 in range(0, HG_HEADS, HG_HEAD_GROUP):
            chains = []
            for h in range(h0, h0 + HG_HEAD_GROUP):
                cs = slice(h * HG_DK, (h + 1) * HG_DK)
                for d, (q_ref, v_ref, z_ref, o_ref) in enumerate(dirs):
                    rs = pl.ds(starts[d], c_rows)
                    gen = _hgrn_chunk_stages(z_ref[0, rs, cs], q_ref[0, rs, cs], v_ref[0, rs, cs],
                                             lb_ref[d, :, cs], st_ref[d, h], tris[d], rev=bool(d))
                    chains.append((gen, o_ref, rs, cs, d, h))
            for _ in range(2):
                for chain in chains:
                    next(chain[0])
            for gen, o_ref, rs, cs, d, h in chains:
                o, st = next(gen)
                o_ref[0, rs, cs] = o
                st_ref[d, h] = st
        return carry

    lax.fori_loop(0, n_chunks, chunk, 0)


def _hgrn_scan(p3, lb, rows):
    b, s, _ = p3.shape
    w = MIX_W
    t = s // rows
    fwd = lambda c: pl.BlockSpec((1, rows, w), lambda bi, i: (bi, i, c))
    bwd = lambda c: pl.BlockSpec((1, rows, w), lambda bi, i: (bi, t - 1 - i, c))
    out = jax.ShapeDtypeStruct((b, s, w), _F32)
    return pl.pallas_call(
        functools.partial(_hgrn_kernel, rows=rows),
        grid=(b, t),
        in_specs=[fwd(0), fwd(1), fwd(3), bwd(0), bwd(1), bwd(4),
                  pl.BlockSpec((2, 1, w), lambda bi, i: (0, 0, 0))],
        out_specs=[fwd(0), bwd(0)],
        out_shape=[out, out],
        scratch_shapes=[pltpu.VMEM((2, HG_HEADS, HG_DK, HG_DK), _F32)],
        compiler_params=_params("parallel", "arbitrary"),
        name="hgrn_scan",
    )(p3, p3, p3, p3, p3, p3, lb)


def _hgrn_finish_kernel(of_ref, ob_ref, g_ref, w_ref, o_ref):
    o = of_ref[...] + ob_ref[...]
    gate = _silu(g_ref[...])
    w = w_ref[...]
    for h in range(HG_HEADS):
        cs = slice(h * HG_DK, (h + 1) * HG_DK)
        oh = o[:, cs]
        ms = jnp.mean(oh * oh, axis=-1, keepdims=True)
        o_ref[:, cs] = (oh * lax.rsqrt(ms + LN_EPS) * w[:, cs] * gate[:, cs]).astype(o_ref.dtype)


def _hgrn_finish(o_fw, o_bw, p2, g_col, norm_w, tm):
    n = o_fw.shape[0]
    return pl.pallas_call(
        _hgrn_finish_kernel,
        grid=(n // tm,),
        in_specs=[pl.BlockSpec((tm, MIX_W), lambda i: (i, 0)),
                  pl.BlockSpec((tm, MIX_W), lambda i: (i, 0)),
                  pl.BlockSpec((tm, MIX_W), lambda i: (i, g_col)),
                  pl.BlockSpec((1, MIX_W), lambda i: (0, 0))],
        out_specs=pl.BlockSpec((tm, MIX_W), lambda i: (i, 0)),
        out_shape=jax.ShapeDtypeStruct((n, MIX_W), _BF),
        compiler_params=_params("parallel"),
        name="hgrn_finish",
    )(o_fw, o_bw, p2, norm_w)


def _sconv_kernel(gb_ref, gc_ref, h_ref, gcp_ref, hp_ref, gcn_ref, hn_ref, w_ref, o_ref, *, halo):
    i = pl.program_id(1)
    u = gc_ref[0].astype(_F32) * h_ref[0].astype(_F32)
    n = u.shape[0]
    u_prev = (gcp_ref[0].astype(_F32) * hp_ref[0].astype(_F32))[halo - 1:halo]
    u_next = (gcn_ref[0].astype(_F32) * hn_ref[0].astype(_F32))[0:1]
    u_prev = jnp.where(i == 0, 0.0, u_prev)
    u_next = jnp.where(i == pl.num_programs(1) - 1, 0.0, u_next)
    row = lax.broadcasted_iota(jnp.int32, u.shape, 0)
    down = jnp.where(row == 0, u_prev, pltpu.roll(u, 1, 0))
    up = jnp.where(row == n - 1, u_next, pltpu.roll(u, n - 1, 0))
    w = w_ref[...]
    y = down * w[0:1] + u * w[1:2] + up * w[2:3]
    o_ref[0] = (gb_ref[0].astype(_F32) * y).astype(o_ref.dtype)


def _short_conv(p3, conv_w, ts):
    b, s, _ = p3.shape
    halo = SUBLANES_BF16
    hb = ts // halo
    last = s // halo - 1
    main = lambda c: pl.BlockSpec((1, ts, MIX_W), lambda bi, i: (bi, i, c))
    prev = lambda c: pl.BlockSpec((1, halo, MIX_W), lambda bi, i: (bi, jnp.maximum(i * hb - 1, 0), c))
    nxt = lambda c: pl.BlockSpec((1, halo, MIX_W), lambda bi, i: (bi, jnp.minimum((i + 1) * hb, last), c))
    return pl.pallas_call(
        functools.partial(_sconv_kernel, halo=halo),
        grid=(b, s // ts),
        in_specs=[main(0), main(1), main(2), prev(1), prev(2), nxt(1), nxt(2),
                  pl.BlockSpec((3, MIX_W), lambda bi, i: (0, 0))],
        out_specs=pl.BlockSpec((1, ts, MIX_W), lambda bi, i: (bi, i, 0)),
        out_shape=jax.ShapeDtypeStruct((b, s, MIX_W), _BF),
        compiler_params=_params("parallel", "parallel"),
        name="short_conv",
    )(p3, p3, p3, p3, p3, p3, p3, conv_w)


def _tile(n, pref):
    return pref if n % pref == 0 else n


def kernel(x, mem, rel_bias, attn_w_in, attn_lambda, attn_subln, hgrn_w_in, hgrn_lower_bound,
           hgrn_norm, conv_w_in, conv_w, mem_w_kv, w_o, ln_gain, ln_bias, ffn_w_up, ffn_conv,
           ffn_w_down):
    b, s, d = x.shape
    n = b * s
    m = mem.shape[1]
    tm = _tile(s, ROW_TILE)
    hg_rows = _tile(s, HG_ROWS)

    bias, far = _bias_tiles(rel_bias, ATTN_Q_TILE, ATTN_ROW_CHUNK)
    mem2 = mem.reshape(b * m, d)
    lbw = jax.nn.softmax(hgrn_lower_bound.astype(_F32), axis=0)
    lb_all = jnp.cumsum(lbw, axis=0) - lbw[0]

    for layer in range(DEPTH):
        kind, j = layer % N_MIXERS, layer // N_MIXERS
        x2 = x.reshape(n, d)
        kv = _matmul(mem2, mem_w_kv[layer].astype(_BF), _BF, _tile(b * m, ROW_TILE))
        kv = kv.reshape(b, m, 2 * XATTN_W)
        if kind == 0:
            q_scale = jnp.where(jnp.arange(attn_w_in.shape[2]) < MIX_W, DA_HD ** -0.5 * LOG2E, 1.0)
            w_in = (attn_w_in[j] * q_scale).astype(_BF)
            p = _matmul(x2, w_in, _BF, tm).reshape(b, s, -1)
            lp = attn_lambda[j].astype(_F32)
            lam_init = 0.8 - 0.6 * math.exp(-0.3 * layer)
            lam = jnp.exp(jnp.sum(lp[0] * lp[1])) - jnp.exp(jnp.sum(lp[2] * lp[3])) + lam_init
            vt = jnp.swapaxes(p[..., 2 * MIX_W:3 * MIX_W], 1, 2).reshape(b, DA_HEADS, DA_W, s)
            vt = jnp.concatenate([vt, jnp.ones((b, DA_HEADS, SUBLANES_BF16, s), _BF)], axis=2)
            mixed = _diff_attention(p, vt, lam.reshape(1),
                                    attn_subln[j].astype(_F32).reshape(1, DA_W),
                                    bias, far, layer, ATTN_Q_TILE, ATTN_K_TILE)
            q_col = 3 * MIX_W // XATTN_W
        elif kind == 1:
            w_in = hgrn_w_in[j].astype(_BF)
            p = _matmul(x2, w_in, _F32, tm).reshape(b, s, -1)
            o_fw, o_bw = _hgrn_scan(p, lb_all[layer].reshape(2, 1, MIX_W), hg_rows)
            mixed = _hgrn_finish(o_fw.reshape(n, MIX_W), o_bw.reshape(n, MIX_W),
                                 p.reshape(n, -1), 2,
                                 hgrn_norm[j].astype(_F32).reshape(1, MIX_W), tm)
            q_col = 5 * MIX_W // XATTN_W
        else:
            w_in = conv_w_in[j].astype(_BF)
            p = _matmul(x2, w_in, _BF, tm).reshape(b, s, -1)
            mixed = _short_conv(p, conv_w[j].astype(_F32), tm)
            q_col = 3 * MIX_W // XATTN_W
        recalled = _memory_attention(p, kv, q_col, tm)
        x2 = _out_ln(x2, mixed.reshape(n, MIX_W), recalled.reshape(n, XATTN_W),
                     w_o[layer].astype(_BF), ln_gain[layer, 0].reshape(1, d),
                     ln_bias[layer, 0].reshape(1, d), tm)
        x = _conv_ffn_ln(x2.reshape(b, s, d), ffn_w_up[layer].astype(_BF),
                         ffn_conv[layer].astype(_F32), ffn_w_down[layer].astype(_BF),
                         ln_gain[layer, 1].reshape(1, d), ln_bias[layer, 1].reshape(1, d),
                         tm, FFN_TILE)
    return x
```

```python
import functools
import math

import jax
import jax.numpy as jnp
from jax import lax
from jax.experimental import pallas as pl
from jax.experimental.pallas import tpu as pltpu

D_MODEL = 1024
DEPTH = 4
N_MIXERS = 3
MIX_W = 3 * D_MODEL // 4
XATTN_HEADS = 4
XATTN_W = D_MODEL - MIX_W
XATTN_HD = XATTN_W // XATTN_HEADS
DA_HD = 64
DA_HEADS = MIX_W // (2 * DA_HD)
DA_W = 2 * DA_HD
N_BUCKETS = 32
MAX_DISTANCE = 128
HG_DK = 128
HG_HEADS = MIX_W // HG_DK
HG_CHUNK = 64
HG_SUB = 16
D_FF = ((8 * D_MODEL // 3 + 255) // 256) * 256
LN_EPS = 1e-5
ALPHA = (2 * DEPTH) ** 0.25
LOG2E = math.log2(math.e)

SUBLANES_F32 = 8
SUBLANES_BF16 = 16
VMEM_LIMIT = 52 * 1024 * 1024

ROW_TILE = 512
ATTN_Q_TILE = 256
ATTN_K_TILE = 8192
ATTN_HEADS_PER_STEP = 1
ATTN_ROW_CHUNK = 256
MXU_WIDTH = 256
FFN_TILE = MXU_WIDTH
FFN_OUT_ROWS = 256
HG_ROWS = 512
HG_HEAD_GROUP = 6

_BF = jnp.bfloat16
_F32 = jnp.float32
_NT = (((1,), (1,)), ((), ()))


def _params(*sem):
    return pltpu.CompilerParams(dimension_semantics=sem, vmem_limit_bytes=VMEM_LIMIT)


def _sigmoid(z):
    e = jnp.exp(-jnp.abs(z))
    r = 1.0 / (1.0 + e)
    return jnp.where(z >= 0, r, e * r)


def _silu(z):
    return z * _sigmoid(z)


def _layer_norm(r, g, b):
    mu = jnp.mean(r, axis=-1, keepdims=True)
    rc = r - mu
    var = jnp.mean(rc * rc, axis=-1, keepdims=True)
    return rc * lax.rsqrt(var + LN_EPS) * g + b


def _mm_kernel(x_ref, w_ref, o_ref):
    o_ref[...] = jnp.dot(x_ref[...].astype(_BF), w_ref[...],
                         preferred_element_type=_F32).astype(o_ref.dtype)


def _matmul(x, w, out_dtype, tm):
    m, k = x.shape
    n = w.shape[1]
    return pl.pallas_call(
        _mm_kernel,
        grid=(m // tm,),
        in_specs=[pl.BlockSpec((tm, k), lambda i: (i, 0)),
                  pl.BlockSpec((k, n), lambda i: (0, 0), pipeline_mode=pl.Buffered(1))],
        out_specs=pl.BlockSpec((tm, n), lambda i: (i, 0)),
        out_shape=jax.ShapeDtypeStruct((m, n), out_dtype),
        compiler_params=_params("parallel"),
        name="proj",
    )(x, w)


def _memattn_kernel(q_ref, km_ref, vm_ref, o_ref):
    q = q_ref[0].astype(_BF)
    km = km_ref[0]
    vm = vm_ref[0]
    lane = lax.broadcasted_iota(jnp.int32, (1, XATTN_W), 1)
    acc = jnp.zeros(q.shape, _F32)
    for h in range(XATTN_HEADS):
        head = (lane >= h * XATTN_HD) & (lane < (h + 1) * XATTN_HD)
        qh = jnp.where(head, q, jnp.zeros_like(q))
        s = lax.dot_general(qh, km, _NT, preferred_element_type=_F32) * (XATTN_HD ** -0.5)
        p = jnp.exp(s - jnp.max(s, axis=-1, keepdims=True))
        l = jnp.sum(p, axis=-1, keepdims=True)
        vh = jnp.where(head, vm, jnp.zeros_like(vm))
        acc = acc + jnp.dot(p.astype(_BF), vh, preferred_element_type=_F32) / l
    o_ref[0] = acc.astype(o_ref.dtype)


def _memory_attention(p3, kv, q_col, tm):
    b, s, _ = p3.shape
    m = kv.shape[1]
    return pl.pallas_call(
        _memattn_kernel,
        grid=(b, s // tm),
        in_specs=[pl.BlockSpec((1, tm, XATTN_W), lambda bi, i: (bi, i, q_col)),
                  pl.BlockSpec((1, m, XATTN_W), lambda bi, i: (bi, 0, 0)),
                  pl.BlockSpec((1, m, XATTN_W), lambda bi, i: (bi, 0, 1))],
        out_specs=pl.BlockSpec((1, tm, XATTN_W), lambda bi, i: (bi, i, 0)),
        out_shape=jax.ShapeDtypeStruct((b, s, XATTN_W), _BF),
        compiler_params=_params("parallel", "parallel"),
        name="memattn",
    )(p3, kv, kv)


def _out_ln_kernel(x_ref, mix_ref, rec_ref, wo_ref, g_ref, b_ref, o_ref):
    for r0 in range(0, x_ref.shape[0], FFN_OUT_ROWS):
        rows = slice(r0, r0 + FFN_OUT_ROWS)
        y = jnp.dot(mix_ref[rows, :], wo_ref[:MIX_W, :], preferred_element_type=_F32)
        y = y + jnp.dot(rec_ref[rows, :], wo_ref[MIX_W:, :], preferred_element_type=_F32)
        o_ref[rows, :] = _layer_norm(ALPHA * x_ref[rows, :] + y, g_ref[...], b_ref[...])


def _out_ln(x2, mixed, recalled, wo, g, b, tm):
    n = x2.shape[0]
    return pl.pallas_call(
        _out_ln_kernel,
        grid=(n // tm,),
        in_specs=[pl.BlockSpec((tm, D_MODEL), lambda i: (i, 0)),
                  pl.BlockSpec((tm, MIX_W), lambda i: (i, 0)),
                  pl.BlockSpec((tm, XATTN_W), lambda i: (i, 0)),
                  pl.BlockSpec((D_MODEL, D_MODEL), lambda i: (0, 0)),
                  pl.BlockSpec((1, D_MODEL), lambda i: (0, 0)),
                  pl.BlockSpec((1, D_MODEL), lambda i: (0, 0))],
        out_specs=pl.BlockSpec((tm, D_MODEL), lambda i: (i, 0)),
        out_shape=jax.ShapeDtypeStruct((n, D_MODEL), _F32),
        compiler_params=_params("parallel"),
        name="out_ln",
    )(x2, mixed, recalled, wo, g, b)


def _dwconv_rows(h, w):
    n = h.shape[0]
    return (pltpu.roll(h, 1, 0) * w[0:1] + h * w[1:2] + pltpu.roll(h, n - 1, 0) * w[2:3])


def _ffn_kernel(x_ref, xp_ref, xn_ref, wu_ref, wc_ref, wd_ref, g_ref, b_ref, o_ref, gated_ref,
                *, ts, halo, fb):
    i = pl.program_id(1)
    prev = jnp.where(i == 0, 0.0, xp_ref[0])
    nxt = jnp.where(i == pl.num_programs(1) - 1, 0.0, xn_ref[0])
    xb = jnp.concatenate([prev, x_ref[0], nxt], axis=0).astype(_BF)
    for c in range(D_FF // fb):
        ca = slice(c * fb, (c + 1) * fb)
        cv = slice(D_FF + c * fb, D_FF + (c + 1) * fb)
        ha = jnp.dot(xb, wu_ref[:, ca], preferred_element_type=_F32)
        hv = jnp.dot(xb, wu_ref[:, cv], preferred_element_type=_F32)
        a = _dwconv_rows(ha, wc_ref[:, ca])[halo:halo + ts]
        v = _dwconv_rows(hv, wc_ref[:, cv])[halo:halo + ts]
        gated_ref[:, ca] = (_silu(a) * v).astype(_BF)
    for r0 in range(0, ts, FFN_OUT_ROWS):
        rows = slice(r0, r0 + FFN_OUT_ROWS)
        f = jnp.dot(gated_ref[rows, :], wd_ref[...], preferred_element_type=_F32)
        o_ref[0, rows, :] = _layer_norm(ALPHA * x_ref[0, rows, :] + f, g_ref[...], b_ref[...])


def _conv_ffn_ln(x3, w_up, w_conv, w_down, g, b, ts, fb):
    bsz, s, d = x3.shape
    halo = SUBLANES_BF16
    hb = ts // halo
    last = s // halo - 1
    kern = functools.partial(_ffn_kernel, ts=ts, halo=halo, fb=fb)
    resident = lambda shape: pl.BlockSpec(shape, lambda bi, i: (0, 0),
                                          pipeline_mode=pl.Buffered(1))
    return pl.pallas_call(
        kern,
        grid=(bsz, s // ts),
        in_specs=[pl.BlockSpec((1, ts, d), lambda bi, i: (bi, i, 0)),
                  pl.BlockSpec((1, halo, d), lambda bi, i: (bi, jnp.maximum(i * hb - 1, 0), 0)),
                  pl.BlockSpec((1, halo, d), lambda bi, i: (bi, jnp.minimum((i + 1) * hb, last), 0)),
                  resident((d, 2 * D_FF)),
                  resident((3, 2 * D_FF)),
                  resident((D_FF, d)),
                  pl.BlockSpec((1, d), lambda bi, i: (0, 0)),
                  pl.BlockSpec((1, d), lambda bi, i: (0, 0))],
        out_specs=pl.BlockSpec((1, ts, d), lambda bi, i: (bi, i, 0)),
        out_shape=jax.ShapeDtypeStruct((bsz, s, d), _F32),
        scratch_shapes=[pltpu.VMEM((ts, D_FF), _BF)],
        compiler_params=_params("parallel", "parallel"),
        name="conv_ffn",
    )(x3, x3, x3, w_up, w_conv, w_down, g, b)


def _fold_rows(x, op, group=SUBLANES_F32):
    acc = x[:group]
    for g in range(1, x.shape[0] // group):
        acc = op(acc, x[g * group:(g + 1) * group])
    while acc.shape[0] > SUBLANES_F32:
        half = acc.shape[0] // 2
        acc = op(acc[:half], acc[half:])
    return acc


def _attn_kernel(lam_ref, far_ref, q_ref, k_ref, vt_ref, bias_ref, subln_ref, o_ref,
                 s_a, s_b, acc_ref, *, tq, tk, out_scale):
    heads = q_ref.shape[2] // DA_W
    streams = [(hh, mi) for hh in range(heads) for mi in range(2)]
    n_str = len(streams)
    hp = pl.program_id(1)
    nkv = k_ref.shape[1] // tk
    nq = q_ref.shape[1] // tq
    n_chunks = tk // ATTN_ROW_CHUNK
    lane = lax.broadcasted_iota(jnp.int32, (1, heads * DA_W), 1)

    unit, u_min, u_max = _near_offsets(tq, ATTN_ROW_CHUNK)
    consts = [[far_ref[3 * (heads * hp + hh) + c] for c in range(3)] for hh in range(heads)]

    def stream_queries(i):
        q = q_ref[0, pl.ds(pl.multiple_of(i * tq, tq), tq), :]
        zero = jnp.zeros_like(q)
        return [jnp.where((lane >= hh * DA_W + mi * DA_HD) & (lane < hh * DA_W + (mi + 1) * DA_HD),
                          q, zero).T for hh, mi in streams]

    def side_select(i, j, r, left, mid, right):
        u = (tk * j + ATTN_ROW_CHUNK * r - tq * i) // unit
        return jnp.where(u < u_min, left, jnp.where(u > u_max, right, mid))

    def max_bound(i, j, r, hh):
        return side_select(i, j, r, consts[hh][0], consts[hh][2], consts[hh][1])

    def score_rows(qs, i, j, r, s_out, st):
        k0 = pl.multiple_of(j * tk + r * ATTN_ROW_CHUNK, ATTN_ROW_CHUNK)
        s = jnp.dot(k_ref[0, pl.ds(k0, ATTN_ROW_CHUNK), :], qs[st], preferred_element_type=_F32)
        s_out[st, r * ATTN_ROW_CHUNK:(r + 1) * ATTN_ROW_CHUNK, :] = s
        return _fold_rows(s, jnp.maximum) + max_bound(i, j, r, streams[st][0])

    def add_near_bias(i, j, s_out):
        assert unit == ATTN_ROW_CHUNK
        for u in range(u_min, u_max + 1):
            r = (tq // unit) * i + u - n_chunks * j

            @pl.when((r >= 0) & (r < n_chunks))
            def _():
                rows = pl.ds(pl.multiple_of(r * ATTN_ROW_CHUNK, ATTN_ROW_CHUNK), ATTN_ROW_CHUNK)
                for st, (hh, _) in enumerate(streams):
                    s_out[st, rows, :] = s_out[st, rows, :] + bias_ref[u - u_min, hh]

    def half_step(i, j, carry, s_cur, s_nxt, nxt):
        ni, nj, nqs = nxt
        tmax = [None] * n_str
        pv = [None] * n_str
        for r in range(n_chunks):
            rows = slice(r * ATTN_ROW_CHUNK, (r + 1) * ATTN_ROW_CHUNK)
            for st in range(n_str):
                t = score_rows(nqs, ni, nj, r, s_nxt, st)
                tmax[st] = t if r == 0 else jnp.maximum(tmax[st], t)
            v0 = pl.multiple_of(j * tk + r * ATTN_ROW_CHUNK, ATTN_ROW_CHUNK)
            for st, (hh, _) in enumerate(streams):
                shift = side_select(i, j, r, consts[hh][0], 0.0, consts[hh][1])
                p = jnp.exp2(s_cur[st, rows, :] - (carry[st][0] - shift)).astype(_BF)
                d = jnp.dot(vt_ref[0, hh, :, pl.ds(v0, ATTN_ROW_CHUNK)], p,
                            preferred_element_type=_F32)
                pv[st] = d if r == 0 else pv[st] + d
        new, tile_max = [], []
        for st in range(n_str):
            m, alpha = carry[st]
            acc_ref[st] = alpha * acc_ref[st] + pv[st]
            cand = jnp.max(tmax[st], axis=0, keepdims=True)
            m_nxt = jnp.maximum(m, cand)
            new.append((m_nxt, jnp.exp2(m - m_nxt)))
            tile_max.append(cand)
        add_near_bias(ni, nj, s_nxt)
        return tuple(new), tuple(tile_max)

    qs0 = stream_queries(0)
    first = []
    for st in range(n_str):
        parts = [score_rows(qs0, 0, 0, r, s_a, st) for r in range(n_chunks)]
        first.append(jnp.max(functools.reduce(jnp.maximum, parts), axis=0, keepdims=True))
    add_near_bias(0, 0, s_a)
    lam = lam_ref[0]

    def query_tile(i, m_first, bufs=(s_a, s_b)):
        qs = stream_queries(i)
        acc_ref[...] = jnp.zeros_like(acc_ref)
        carry = tuple((m_first[st], jnp.zeros((1, tq), _F32)) for st in range(n_str))
        i_next = jnp.minimum(i + 1, nq - 1)
        first_of_next = (i_next, 0, stream_queries(i_next))

        if nkv == 1:
            _, m_next = half_step(i, 0, carry, bufs[0], bufs[1], first_of_next)
        else:
            def pair(jj, carry):
                j = 2 * jj
                carry, _ = half_step(i, j, carry, s_a, s_b, (i, j + 1, qs))
                carry, _ = half_step(i, j + 1, carry, s_b, s_a, (i, j + 2, qs))
                return carry

            carry = lax.fori_loop(0, nkv // 2 - 1, pair, carry)
            carry, _ = half_step(i, nkv - 2, carry, s_a, s_b, (i, nkv - 1, qs))
            _, m_next = half_step(i, nkv - 1, carry, s_b, s_a, first_of_next)

        q0 = pl.multiple_of(i * tq, tq)
        for hh in range(heads):
            o0 = acc_ref[2 * hh, :DA_W] / acc_ref[2 * hh, DA_W:DA_W + 1]
            o1 = acc_ref[2 * hh + 1, :DA_W] / acc_ref[2 * hh + 1, DA_W:DA_W + 1]
            o = (o0 - lam * o1).T
            ms = jnp.mean(o * o, axis=-1, keepdims=True)
            o_ref[0, pl.ds(q0, tq), hh * DA_W:(hh + 1) * DA_W] = (
                o * lax.rsqrt(ms + LN_EPS) * subln_ref[...] * out_scale).astype(o_ref.dtype)
        return m_next

    if nkv == 1:
        def query_tile_pair(ii, m_first):
            m_first = query_tile(2 * ii, m_first, (s_a, s_b))
            return query_tile(2 * ii + 1, m_first, (s_b, s_a))

        lax.fori_loop(0, nq // 2, query_tile_pair, tuple(first))
    else:
        lax.fori_loop(0, nq, query_tile, tuple(first))


def _t5_bucket(rel):
    half = N_BUCKETS // 2
    max_exact = half // 2
    ret = jnp.where(rel > 0, half, 0)
    n = jnp.abs(rel)
    nf = jnp.maximum(n, 1).astype(_F32)
    large = max_exact + (jnp.log(nf / max_exact) / math.log(MAX_DISTANCE / max_exact)
                         * (half - max_exact)).astype(jnp.int32)
    large = jnp.minimum(large, half - 1)
    return ret + jnp.where(n < max_exact, n, large)


def _near_offsets(tq, tk):
    unit = math.gcd(tq, tk)
    first = -((tk + MAX_DISTANCE - 2) // unit)
    last = (tq + MAX_DISTANCE - 2) // unit
    return unit, first, last


def _bias_tiles(rel_bias, tq, tk):
    unit, u_min, u_max = _near_offsets(tq, tk)
    n_near = u_max - u_min + 1
    c = jnp.arange(tk, dtype=jnp.int32)[:, None]
    r = jnp.arange(tq, dtype=jnp.int32)[None, :]
    u = jnp.arange(u_min, u_max + 1, dtype=jnp.int32)[:, None, None]
    table = rel_bias.astype(_F32) * LOG2E
    bucket = _t5_bucket(u * unit + c - r)[:, None]
    tiles = jnp.zeros((n_near, DA_HEADS, tk, tq), _F32)
    for b in range(N_BUCKETS):
        tiles = jnp.where(bucket == b, table[b][None, :, None, None], tiles)
    far = table[_t5_bucket(jnp.array([-MAX_DISTANCE, MAX_DISTANCE], jnp.int32))]
    consts = jnp.concatenate([far, jnp.max(table, axis=0, keepdims=True)], axis=0)
    return tiles, consts.T.reshape(-1)


def _diff_attention(p3, vt, lam, subln, bias, far, layer, tq, tk):
    b, s, _ = p3.shape
    vrows = vt.shape[2]
    assert s % (2 * tk) == 0 or (tk == s and s % (2 * tq) == 0)
    lam_init = 0.8 - 0.6 * math.exp(-0.3 * layer)
    kern = functools.partial(_attn_kernel, tq=tq, tk=tk, out_scale=1.0 - lam_init)
    nh = ATTN_HEADS_PER_STEP
    groups = DA_HEADS // nh
    return pl.pallas_call(
        kern,
        grid=(b, groups),
        in_specs=[pl.BlockSpec(memory_space=pltpu.SMEM),
                  pl.BlockSpec(memory_space=pltpu.SMEM),
                  pl.BlockSpec((1, s, nh * DA_W), lambda bi, h: (bi, 0, h),
                               pipeline_mode=pl.Buffered(1)),
                  pl.BlockSpec((1, s, nh * DA_W), lambda bi, h: (bi, 0, groups + h),
                               pipeline_mode=pl.Buffered(1)),
                  pl.BlockSpec((1, nh, vrows, s), lambda bi, h: (bi, h, 0, 0),
                               pipeline_mode=pl.Buffered(1)),
                  pl.BlockSpec((bias.shape[0], nh) + bias.shape[2:], lambda bi, h: (0, h, 0, 0)),
                  pl.BlockSpec((1, DA_W), lambda bi, h: (0, 0))],
        out_specs=pl.BlockSpec((1, s, nh * DA_W), lambda bi, h: (bi, 0, h)),
        out_shape=jax.ShapeDtypeStruct((b, s, MIX_W), _BF),
        scratch_shapes=[pltpu.VMEM((2 * nh, tk, tq), _F32), pltpu.VMEM((2 * nh, tk, tq), _F32),
                        pltpu.VMEM((2 * nh, vrows, tq), _F32)],
        compiler_params=_params("parallel", "parallel"),
        name="diff_attn",
    )(lam, far, p3, p3, vt, bias, subln)


def _hgrn_chunk_stages(z, q, v, lb, st, tri, rev):
    c_rows = z.shape[0]
    e = jnp.exp(-jnp.abs(z))
    r = 1.0 / (1.0 + e)
    sig_pos, sig_neg = jnp.where(z >= 0, r, e * r), jnp.where(z >= 0, e * r, r)
    f = lb + (1.0 - lb) * sig_pos
    kk = (1.0 - lb) * sig_neg
    logf = jnp.log(f)
    hi = logf.astype(_BF)
    lo = (logf - hi.astype(_F32)).astype(_BF)
    cum = (jnp.dot(tri, hi, preferred_element_type=_F32)
           + jnp.dot(tri, lo, preferred_element_type=_F32))
    qh = _silu(q) * (HG_DK ** -0.5)
    vb = v.astype(_BF)
    yield None
    total = cum[0:1, :] if rev else cum[c_rows - 1:c_rows, :]
    o_inter = lax.dot_general((qh * jnp.exp(cum)).astype(_BF), st.astype(_BF), _NT,
                              preferred_element_type=_F32)
    ks = (kk * jnp.exp(total - cum)).astype(_BF)
    st_new = st * jnp.exp(total) + jnp.dot(v.T.astype(_BF), ks, preferred_element_type=_F32)
    blocks = []
    for b in range(c_rows // HG_SUB):
        lo_r, hi_r = b * HG_SUB, (b + 1) * HG_SUB
        if rev:
            cols = slice(lo_r, c_rows)
            base = cum[hi_r:hi_r + 1, :] if hi_r < c_rows else jnp.zeros_like(total)
        else:
            cols = slice(0, hi_r)
            base = cum[lo_r - 1:lo_r, :] if b else jnp.zeros_like(total)
        qq = (qh[lo_r:hi_r] * jnp.exp(cum[lo_r:hi_r] - base)).astype(_BF)
        kt = (kk[cols] * jnp.exp(base - cum[cols])).astype(_BF)
        a = lax.dot_general(qq, kt, _NT, preferred_element_type=_F32)
        blocks.append((lo_r, hi_r, cols, a))
    yield None
    parts = []
    for lo_r, hi_r, cols, a in blocks:
        sub_row = lax.broadcasted_iota(jnp.int32, a.shape, 0) + lo_r
        sub_col = lax.broadcasted_iota(jnp.int32, a.shape, 1) + cols.start
        seen = (sub_col >= sub_row) if rev else (sub_col <= sub_row)
        a = jnp.where(seen, a, 0.0)
        parts.append(o_inter[lo_r:hi_r]
                     + jnp.dot(a.astype(_BF), vb[cols], preferred_element_type=_F32))
    yield jnp.concatenate(parts, axis=0), st_new


def _hgrn_kernel(qf_ref, vf_ref, zf_ref, qb_ref, vb_ref, zb_ref, lb_ref, of_ref, ob_ref, st_ref,
                 *, rows):
    @pl.when(pl.program_id(1) == 0)
    def _():
        st_ref[...] = jnp.zeros_like(st_ref)

    c_rows = HG_CHUNK
    n_chunks = rows // c_rows
    row = lax.broadcasted_iota(jnp.int32, (c_rows, c_rows), 0)
    col = lax.broadcasted_iota(jnp.int32, (c_rows, c_rows), 1)
    tris = ((col <= row).astype(_BF), (col >= row).astype(_BF))
    dirs = ((qf_ref, vf_ref, zf_ref, of_ref), (qb_ref, vb_ref, zb_ref, ob_ref))

    def chunk(c, carry):
        starts = (pl.multiple_of(c * c_rows, c_rows),
                  pl.multiple_of((n_chunks - 1 - c) * c_rows, c_rows))
        for h0 in range(0, HG_HEADS, HG_HEAD_GROUP):
            chains = []
            for h in range(h0, h0 + HG_HEAD_GROUP):
                cs = slice(h * HG_DK, (h + 1) * HG_DK)
                for d, (q_ref, v_ref, z_ref, o_ref) in enumerate(dirs):
                    rs = pl.ds(starts[d], c_rows)
                    gen = _hgrn_chunk_stages(z_ref[0, rs, cs], q_ref[0, rs, cs], v_ref[0, rs, cs],
                                             lb_ref[d, :, cs], st_ref[d, h], tris[d], rev=bool(d))
                    chains.append((gen, o_ref, rs, cs, d, h))
            for _ in range(2):
                for chain in chains:
                    next(chain[0])
            for gen, o_ref, rs, cs, d, h in chains:
                o, st = next(gen)
                o_ref[0, rs, cs] = o
                st_ref[d, h] = st
        return carry

    lax.fori_loop(0, n_chunks, chunk, 0)


def _hgrn_scan(p3, lb, rows):
    b, s, _ = p3.shape
    w = MIX_W
    t = s // rows
    fwd = lambda c: pl.BlockSpec((1, rows, w), lambda bi, i: (bi, i, c))
    bwd = lambda c: pl.BlockSpec((1, rows, w), lambda bi, i: (bi, t - 1 - i, c))
    out = jax.ShapeDtypeStruct((b, s, w), _F32)
    return pl.pallas_call(
        functools.partial(_hgrn_kernel, rows=rows),
        grid=(b, t),
        in_specs=[fwd(0), fwd(1), fwd(3), bwd(0), bwd(1), bwd(4),
                  pl.BlockSpec((2, 1, w), lambda bi, i: (0, 0, 0))],
        out_specs=[fwd(0), bwd(0)],
        out_shape=[out, out],
        scratch_shapes=[pltpu.VMEM((2, HG_HEADS, HG_DK, HG_DK), _F32)],
        compiler_params=_params("parallel", "arbitrary"),
        name="hgrn_scan",
    )(p3, p3, p3, p3, p3, p3, lb)


def _hgrn_finish_kernel(of_ref, ob_ref, g_ref, w_ref, o_ref):
    o = of_ref[...] + ob_ref[...]
    gate = _silu(g_ref[...])
    w = w_ref[...]
    for h in range(HG_HEADS):
        cs = slice(h * HG_DK, (h + 1) * HG_DK)
        oh = o[:, cs]
        ms = jnp.mean(oh * oh, axis=-1, keepdims=True)
        o_ref[:, cs] = (oh * lax.rsqrt(ms + LN_EPS) * w[:, cs] * gate[:, cs]).astype(o_ref.dtype)


def _hgrn_finish(o_fw, o_bw, p2, g_col, norm_w, tm):
    n = o_fw.shape[0]
    return pl.pallas_call(
        _hgrn_finish_kernel,
        grid=(n // tm,),
        in_specs=[pl.BlockSpec((tm, MIX_W), lambda i: (i, 0)),
                  pl.BlockSpec((tm, MIX_W), lambda i: (i, 0)),
                  pl.BlockSpec((tm, MIX_W), lambda i: (i, g_col)),
                  pl.BlockSpec((1, MIX_W), lambda i: (0, 0))],
        out_specs=pl.BlockSpec((tm, MIX_W), lambda i: (i, 0)),
        out_shape=jax.ShapeDtypeStruct((n, MIX_W), _BF),
        compiler_params=_params("parallel"),
        name="hgrn_finish",
    )(o_fw, o_bw, p2, norm_w)


def _sconv_kernel(gb_ref, gc_ref, h_ref, gcp_ref, hp_ref, gcn_ref, hn_ref, w_ref, o_ref, *, halo):
    i = pl.program_id(1)
    u = gc_ref[0].astype(_F32) * h_ref[0].astype(_F32)
    n = u.shape[0]
    u_prev = (gcp_ref[0].astype(_F32) * hp_ref[0].astype(_F32))[halo - 1:halo]
    u_next = (gcn_ref[0].astype(_F32) * hn_ref[0].astype(_F32))[0:1]
    u_prev = jnp.where(i == 0, 0.0, u_prev)
    u_next = jnp.where(i == pl.num_programs(1) - 1, 0.0, u_next)
    row = lax.broadcasted_iota(jnp.int32, u.shape, 0)
    down = jnp.where(row == 0, u_prev, pltpu.roll(u, 1, 0))
    up = jnp.where(row == n - 1, u_next, pltpu.roll(u, n - 1, 0))
    w = w_ref[...]
    y = down * w[0:1] + u * w[1:2] + up * w[2:3]
    o_ref[0] = (gb_ref[0].astype(_F32) * y).astype(o_ref.dtype)


def _short_conv(p3, conv_w, ts):
    b, s, _ = p3.shape
    halo = SUBLANES_BF16
    hb = ts // halo
    last = s // halo - 1
    main = lambda c: pl.BlockSpec((1, ts, MIX_W), lambda bi, i: (bi, i, c))
    prev = lambda c: pl.BlockSpec((1, halo, MIX_W), lambda bi, i: (bi, jnp.maximum(i * hb - 1, 0), c))
    nxt = lambda c: pl.BlockSpec((1, halo, MIX_W), lambda bi, i: (bi, jnp.minimum((i + 1) * hb, last), c))
    return pl.pallas_call(
        functools.partial(_sconv_kernel, halo=halo),
        grid=(b, s // ts),
        in_specs=[main(0), main(1), main(2), prev(1), prev(2), nxt(1), nxt(2),
                  pl.BlockSpec((3, MIX_W), lambda bi, i: (0, 0))],
        out_specs=pl.BlockSpec((1, ts, MIX_W), lambda bi, i: (bi, i, 0)),
        out_shape=jax.ShapeDtypeStruct((b, s, MIX_W), _BF),
        compiler_params=_params("parallel", "parallel"),
        name="short_conv",
    )(p3, p3, p3, p3, p3, p3, p3, conv_w)


def _tile(n, pref):
    return pref if n % pref == 0 else n


def kernel(x, mem, rel_bias, attn_w_in, attn_lambda, attn_subln, hgrn_w_in, hgrn_lower_bound,
           hgrn_norm, conv_w_in, conv_w, mem_w_kv, w_o, ln_gain, ln_bias, ffn_w_up, ffn_conv,
           ffn_w_down):
    b, s, d = x.shape
    n = b * s
    m = mem.shape[1]
    tm = _tile(s, ROW_TILE)
    hg_rows = _tile(s, HG_ROWS)

    bias, far = _bias_tiles(rel_bias, ATTN_Q_TILE, ATTN_ROW_CHUNK)
    mem2 = mem.reshape(b * m, d)
    lbw = jax.nn.softmax(hgrn_lower_bound.astype(_F32), axis=0)
    lb_all = jnp.cumsum(lbw, axis=0) - lbw[0]

    for layer in range(DEPTH):
        kind, j = layer % N_MIXERS, layer // N_MIXERS
        x2 = x.reshape(n, d)
        kv = _matmul(mem2, mem_w_kv[layer].astype(_BF), _BF, _tile(b * m, ROW_TILE))
        kv = kv.reshape(b, m, 2 * XATTN_W)
        if kind == 0:
            q_scale = jnp.where(jnp.arange(attn_w_in.shape[2]) < MIX_W, DA_HD ** -0.5 * LOG2E, 1.0)
            w_in = (attn_w_in[j] * q_scale).astype(_BF)
            p = _matmul(x2, w_in, _BF, tm).reshape(b, s, -1)
            lp = attn_lambda[j].astype(_F32)
            lam_init = 0.8 - 0.6 * math.exp(-0.3 * layer)
            lam = jnp.exp(jnp.sum(lp[0] * lp[1])) - jnp.exp(jnp.sum(lp[2] * lp[3])) + lam_init
            vt = jnp.swapaxes(p[..., 2 * MIX_W:3 * MIX_W], 1, 2).reshape(b, DA_HEADS, DA_W, s)
            vt = jnp.concatenate([vt, jnp.ones((b, DA_HEADS, SUBLANES_BF16, s), _BF)], axis=2)
            mixed = _diff_attention(p, vt, lam.reshape(1),
                                    attn_subln[j].astype(_F32).reshape(1, DA_W),
                                    bias, far, layer, ATTN_Q_TILE, ATTN_K_TILE)
            q_col = 3 * MIX_W // XATTN_W
        elif kind == 1:
            w_in = hgrn_w_in[j].astype(_BF)
            p = _matmul(x2, w_in, _F32, tm).reshape(b, s, -1)
            o_fw, o_bw = _hgrn_scan(p, lb_all[layer].reshape(2, 1, MIX_W), hg_rows)
            mixed = _hgrn_finish(o_fw.reshape(n, MIX_W), o_bw.reshape(n, MIX_W),
                                 p.reshape(n, -1), 2,
                                 hgrn_norm[j].astype(_F32).reshape(1, MIX_W), tm)
            q_col = 5 * MIX_W // XATTN_W
        else:
            w_in = conv_w_in[j].astype(_BF)
            p = _matmul(x2, w_in, _BF, tm).reshape(b, s, -1)
            mixed = _short_conv(p, conv_w[j].astype(_F32), tm)
            q_col = 3 * MIX_W // XATTN_W
        recalled = _memory_attention(p, kv, q_col, tm)
        x2 = _out_ln(x2, mixed.reshape(n, MIX_W), recalled.reshape(n, XATTN_W),
                     w_o[layer].astype(_BF), ln_gain[layer, 0].reshape(1, d),
                     ln_bias[layer, 0].reshape(1, d), tm)
        x = _conv_ffn_ln(x2.reshape(b, s, d), ffn_w_up[layer].astype(_BF),
                         ffn_conv[layer].astype(_F32), ffn_w_down[layer].astype(_BF),
                         ln_gain[layer, 1].reshape(1, d), ln_bias[layer, 1].reshape(1, d),
                         tm, FFN_TILE)
    return x
```

```python
import functools
import math

import jax
import jax.numpy as jnp
from jax import lax
from jax.experimental import pallas as pl
from jax.experimental.pallas import tpu as pltpu

D_MODEL = 1024
DEPTH = 4
N_MIXERS = 3
MIX_W = 3 * D_MODEL // 4
XATTN_HEADS = 4
XATTN_W = D_MODEL - MIX_W
XATTN_HD = XATTN_W // XATTN_HEADS
DA_HD = 64
DA_HEADS = MIX_W // (2 * DA_HD)
DA_W = 2 * DA_HD
N_BUCKETS = 32
MAX_DISTANCE = 128
HG_DK = 128
HG_HEADS = MIX_W // HG_DK
HG_CHUNK = 64
HG_SUB = 16
D_FF = ((8 * D_MODEL // 3 + 255) // 256) * 256
LN_EPS = 1e-5
ALPHA = (2 * DEPTH) ** 0.25
LOG2E = math.log2(math.e)

SUBLANES_F32 = 8
SUBLANES_BF16 = 16
VMEM_LIMIT = 52 * 1024 * 1024

ROW_TILE = 512
ATTN_Q_TILE = 256
ATTN_K_TILE = 8192
ATTN_HEADS_PER_STEP = 1
ATTN_ROW_CHUNK = 256
MXU_WIDTH = 256
FFN_TILE = MXU_WIDTH
FFN_OUT_ROWS = 256
HG_ROWS = 512
HG_HEAD_GROUP = 6

_BF = jnp.bfloat16
_F32 = jnp.float32
_NT = (((1,), (1,)), ((), ()))


def _params(*sem):
    return pltpu.CompilerParams(dimension_semantics=sem, vmem_limit_bytes=VMEM_LIMIT)


def _sigmoid(z):
    e = jnp.exp(-jnp.abs(z))
    r = 1.0 / (1.0 + e)
    return jnp.where(z >= 0, r, e * r)


def _silu(z):
    return z * _sigmoid(z)


def _layer_norm(r, g, b):
    mu = jnp.mean(r, axis=-1, keepdims=True)
    rc = r - mu
    var = jnp.mean(rc * rc, axis=-1, keepdims=True)
    return rc * lax.rsqrt(var + LN_EPS) * g + b


def _mm_kernel(x_ref, w_ref, o_ref):
    o_ref[...] = jnp.dot(x_ref[...].astype(_BF), w_ref[...],
                         preferred_element_type=_F32).astype(o_ref.dtype)


def _matmul(x, w, out_dtype, tm):
    m, k = x.shape
    n = w.shape[1]
    return pl.pallas_call(
        _mm_kernel,
        grid=(m // tm,),
        in_specs=[pl.BlockSpec((tm, k), lambda i: (i, 0)),
                  pl.BlockSpec((k, n), lambda i: (0, 0), pipeline_mode=pl.Buffered(1))],
        out_specs=pl.BlockSpec((tm, n), lambda i: (i, 0)),
        out_shape=jax.ShapeDtypeStruct((m, n), out_dtype),
        compiler_params=_params("parallel"),
        name="proj",
    )(x, w)


def _memattn_rows(q, km, vm):
    lane = lax.broadcasted_iota(jnp.int32, (1, XATTN_W), 1)
    acc = jnp.zeros(q.shape, _F32)
    for h in range(XATTN_HEADS):
        head = (lane >= h * XATTN_HD) & (lane < (h + 1) * XATTN_HD)
        qh = jnp.where(head, q, jnp.zeros_like(q))
        s = lax.dot_general(qh, km, _NT, preferred_element_type=_F32) * (XATTN_HD ** -0.5)
        p = jnp.exp(s - jnp.max(s, axis=-1, keepdims=True))
        l = jnp.sum(p, axis=-1, keepdims=True)
        vh = jnp.where(head, vm, jnp.zeros_like(vm))
        acc = acc + jnp.dot(p.astype(_BF), vh, preferred_element_type=_F32) / l
    return acc


def _out_ln_kernel(x_ref, mix_ref, q_ref, km_ref, vm_ref, wo_ref, g_ref, b_ref, o_ref):
    km = km_ref[0]
    vm = vm_ref[0]
    for r0 in range(0, x_ref.shape[0], FFN_OUT_ROWS):
        rows = slice(r0, r0 + FFN_OUT_ROWS)
        recalled = _memattn_rows(q_ref[0, rows, :].astype(_BF), km, vm).astype(_BF)
        y = jnp.dot(mix_ref[rows, :], wo_ref[:MIX_W, :], preferred_element_type=_F32)
        y = y + jnp.dot(recalled, wo_ref[MIX_W:, :], preferred_element_type=_F32)
        o_ref[rows, :] = _layer_norm(ALPHA * x_ref[rows, :] + y, g_ref[...], b_ref[...])


def _out_ln(x2, mixed, p3, kv, q_col, wo, g, b, tm):
    bsz, s, _ = p3.shape
    m = kv.shape[1]
    nt = s // tm
    row = lambda bi, i: (bi * nt + i, 0)
    fixed = lambda bi, i: (0, 0)
    return pl.pallas_call(
        _out_ln_kernel,
        grid=(bsz, nt),
        in_specs=[pl.BlockSpec((tm, D_MODEL), row),
                  pl.BlockSpec((tm, MIX_W), row),
                  pl.BlockSpec((1, tm, XATTN_W), lambda bi, i: (bi, i, q_col)),
                  pl.BlockSpec((1, m, XATTN_W), lambda bi, i: (bi, 0, 0)),
                  pl.BlockSpec((1, m, XATTN_W), lambda bi, i: (bi, 0, 1)),
                  pl.BlockSpec((D_MODEL, D_MODEL), fixed),
                  pl.BlockSpec((1, D_MODEL), fixed),
                  pl.BlockSpec((1, D_MODEL), fixed)],
        out_specs=pl.BlockSpec((tm, D_MODEL), row),
        out_shape=jax.ShapeDtypeStruct((bsz * s, D_MODEL), _F32),
        compiler_params=_params("parallel", "parallel"),
        name="out_ln",
    )(x2, mixed, p3, kv, kv, wo, g, b)


def _dwconv_rows(h, w):
    n = h.shape[0]
    return (pltpu.roll(h, 1, 0) * w[0:1] + h * w[1:2] + pltpu.roll(h, n - 1, 0) * w[2:3])


def _ffn_kernel(x_ref, xp_ref, xn_ref, wu_ref, wc_ref, wd_ref, g_ref, b_ref, o_ref, gated_ref,
                *, ts, halo, fb):
    i = pl.program_id(1)
    prev = jnp.where(i == 0, 0.0, xp_ref[0])
    nxt = jnp.where(i == pl.num_programs(1) - 1, 0.0, xn_ref[0])
    xb = jnp.concatenate([prev, x_ref[0], nxt], axis=0).astype(_BF)
    for c in range(D_FF // fb):
        ca = slice(c * fb, (c + 1) * fb)
        cv = slice(D_FF + c * fb, D_FF + (c + 1) * fb)
        ha = jnp.dot(xb, wu_ref[:, ca], preferred_element_type=_F32)
        hv = jnp.dot(xb, wu_ref[:, cv], preferred_element_type=_F32)
        a = _dwconv_rows(ha, wc_ref[:, ca])[halo:halo + ts]
        v = _dwconv_rows(hv, wc_ref[:, cv])[halo:halo + ts]
        gated_ref[:, ca] = (_silu(a) * v).astype(_BF)
    for r0 in range(0, ts, FFN_OUT_ROWS):
        rows = slice(r0, r0 + FFN_OUT_ROWS)
        f = jnp.dot(gated_ref[rows, :], wd_ref[...], preferred_element_type=_F32)
        o_ref[0, rows, :] = _layer_norm(ALPHA * x_ref[0, rows, :] + f, g_ref[...], b_ref[...])


def _conv_ffn_ln(x3, w_up, w_conv, w_down, g, b, ts, fb):
    bsz, s, d = x3.shape
    halo = SUBLANES_BF16
    hb = ts // halo
    last = s // halo - 1
    kern = functools.partial(_ffn_kernel, ts=ts, halo=halo, fb=fb)
    resident = lambda shape: pl.BlockSpec(shape, lambda bi, i: (0, 0),
                                          pipeline_mode=pl.Buffered(1))
    return pl.pallas_call(
        kern,
        grid=(bsz, s // ts),
        in_specs=[pl.BlockSpec((1, ts, d), lambda bi, i: (bi, i, 0)),
                  pl.BlockSpec((1, halo, d), lambda bi, i: (bi, jnp.maximum(i * hb - 1, 0), 0)),
                  pl.BlockSpec((1, halo, d), lambda bi, i: (bi, jnp.minimum((i + 1) * hb, last), 0)),
                  resident((d, 2 * D_FF)),
                  resident((3, 2 * D_FF)),
                  resident((D_FF, d)),
                  pl.BlockSpec((1, d), lambda bi, i: (0, 0)),
                  pl.BlockSpec((1, d), lambda bi, i: (0, 0))],
        out_specs=pl.BlockSpec((1, ts, d), lambda bi, i: (bi, i, 0)),
        out_shape=jax.ShapeDtypeStruct((bsz, s, d), _F32),
        scratch_shapes=[pltpu.VMEM((ts, D_FF), _BF)],
        compiler_params=_params("parallel", "parallel"),
        name="conv_ffn",
    )(x3, x3, x3, w_up, w_conv, w_down, g, b)


def _fold_rows(x, op, group=SUBLANES_F32):
    acc = x[:group]
    for g in range(1, x.shape[0] // group):
        acc = op(acc, x[g * group:(g + 1) * group])
    while acc.shape[0] > SUBLANES_F32:
        half = acc.shape[0] // 2
        acc = op(acc[:half], acc[half:])
    return acc


def _attn_kernel(lam_ref, far_ref, q_ref, k_ref, vt_ref, bias_ref, subln_ref, o_ref,
                 s_a, s_b, acc_ref, *, tq, tk, out_scale):
    heads = q_ref.shape[2] // DA_W
    streams = [(hh, mi) for hh in range(heads) for mi in range(2)]
    n_str = len(streams)
    hp = pl.program_id(1)
    nkv = k_ref.shape[1] // tk
    nq = q_ref.shape[1] // tq
    n_chunks = tk // ATTN_ROW_CHUNK
    lane = lax.broadcasted_iota(jnp.int32, (1, heads * DA_W), 1)

    unit, u_min, u_max = _near_offsets(tq, ATTN_ROW_CHUNK)
    consts = [[far_ref[3 * (heads * hp + hh) + c] for c in range(3)] for hh in range(heads)]

    def stream_queries(i):
        q = q_ref[0, pl.ds(pl.multiple_of(i * tq, tq), tq), :]
        zero = jnp.zeros_like(q)
        return [jnp.where((lane >= hh * DA_W + mi * DA_HD) & (lane < hh * DA_W + (mi + 1) * DA_HD),
                          q, zero).T for hh, mi in streams]

    def side_select(i, j, r, left, mid, right):
        u = (tk * j + ATTN_ROW_CHUNK * r - tq * i) // unit
        return jnp.where(u < u_min, left, jnp.where(u > u_max, right, mid))

    def max_bound(i, j, r, hh):
        return side_select(i, j, r, consts[hh][0], consts[hh][2], consts[hh][1])

    def score_rows(qs, i, j, r, s_out, st):
        k0 = pl.multiple_of(j * tk + r * ATTN_ROW_CHUNK, ATTN_ROW_CHUNK)
        s = jnp.dot(k_ref[0, pl.ds(k0, ATTN_ROW_CHUNK), :], qs[st], preferred_element_type=_F32)
        s_out[st, r * ATTN_ROW_CHUNK:(r + 1) * ATTN_ROW_CHUNK, :] = s
        return _fold_rows(s, jnp.maximum) + max_bound(i, j, r, streams[st][0])

    def add_near_bias(i, j, s_out):
        assert unit == ATTN_ROW_CHUNK
        for u in range(u_min, u_max + 1):
            r = (tq // unit) * i + u - n_chunks * j

            @pl.when((r >= 0) & (r < n_chunks))
            def _():
                rows = pl.ds(pl.multiple_of(r * ATTN_ROW_CHUNK, ATTN_ROW_CHUNK), ATTN_ROW_CHUNK)
                for st, (hh, _) in enumerate(streams):
                    s_out[st, rows, :] = s_out[st, rows, :] + bias_ref[u - u_min, hh]

    def half_step(i, j, carry, s_cur, s_nxt, nxt):
        ni, nj, nqs = nxt
        tmax = [None] * n_str
        pv = [None] * n_str
        for r in range(n_chunks):
            rows = slice(r * ATTN_ROW_CHUNK, (r + 1) * ATTN_ROW_CHUNK)
            for st in range(n_str):
                t = score_rows(nqs, ni, nj, r, s_nxt, st)
                tmax[st] = t if r == 0 else jnp.maximum(tmax[st], t)
            v0 = pl.multiple_of(j * tk + r * ATTN_ROW_CHUNK, ATTN_ROW_CHUNK)
            for st, (hh, _) in enumerate(streams):
                shift = side_select(i, j, r, consts[hh][0], 0.0, consts[hh][1])
                p = jnp.exp2(s_cur[st, rows, :] - (carry[st][0] - shift)).astype(_BF)
                d = jnp.dot(vt_ref[0, hh, :, pl.ds(v0, ATTN_ROW_CHUNK)], p,
                            preferred_element_type=_F32)
                pv[st] = d if r == 0 else pv[st] + d
        new, tile_max = [], []
        for st in range(n_str):
            m, alpha = carry[st]
            acc_ref[st] = alpha * acc_ref[st] + pv[st]
            cand = jnp.max(tmax[st], axis=0, keepdims=True)
            m_nxt = jnp.maximum(m, cand)
            new.append((m_nxt, jnp.exp2(m - m_nxt)))
            tile_max.append(cand)
        add_near_bias(ni, nj, s_nxt)
        return tuple(new), tuple(tile_max)

    qs0 = stream_queries(0)
    first = []
    for st in range(n_str):
        parts = [score_rows(qs0, 0, 0, r, s_a, st) for r in range(n_chunks)]
        first.append(jnp.max(functools.reduce(jnp.maximum, parts), axis=0, keepdims=True))
    add_near_bias(0, 0, s_a)
    lam = lam_ref[0]

    def query_tile(i, m_first, bufs=(s_a, s_b)):
        qs = stream_queries(i)
        acc_ref[...] = jnp.zeros_like(acc_ref)
        carry = tuple((m_first[st], jnp.zeros((1, tq), _F32)) for st in range(n_str))
        i_next = jnp.minimum(i + 1, nq - 1)
        first_of_next = (i_next, 0, stream_queries(i_next))

        if nkv == 1:
            _, m_next = half_step(i, 0, carry, bufs[0], bufs[1], first_of_next)
        else:
            def pair(jj, carry):
                j = 2 * jj
                carry, _ = half_step(i, j, carry, s_a, s_b, (i, j + 1, qs))
                carry, _ = half_step(i, j + 1, carry, s_b, s_a, (i, j + 2, qs))
                return carry

            carry = lax.fori_loop(0, nkv // 2 - 1, pair, carry)
            carry, _ = half_step(i, nkv - 2, carry, s_a, s_b, (i, nkv - 1, qs))
            _, m_next = half_step(i, nkv - 1, carry, s_b, s_a, first_of_next)

        q0 = pl.multiple_of(i * tq, tq)
        for hh in range(heads):
            o0 = acc_ref[2 * hh, :DA_W] / acc_ref[2 * hh, DA_W:DA_W + 1]
            o1 = acc_ref[2 * hh + 1, :DA_W] / acc_ref[2 * hh + 1, DA_W:DA_W + 1]
            o = (o0 - lam * o1).T
            ms = jnp.mean(o * o, axis=-1, keepdims=True)
            o_ref[0, pl.ds(q0, tq), hh * DA_W:(hh + 1) * DA_W] = (
                o * lax.rsqrt(ms + LN_EPS) * subln_ref[...] * out_scale).astype(o_ref.dtype)
        return m_next

    if nkv == 1:
        def query_tile_pair(ii, m_first):
            m_first = query_tile(2 * ii, m_first, (s_a, s_b))
            return query_tile(2 * ii + 1, m_first, (s_b, s_a))

        lax.fori_loop(0, nq // 2, query_tile_pair, tuple(first))
    else:
        lax.fori_loop(0, nq, query_tile, tuple(first))


def _t5_bucket(rel):
    half = N_BUCKETS // 2
    max_exact = half // 2
    ret = jnp.where(rel > 0, half, 0)
    n = jnp.abs(rel)
    nf = jnp.maximum(n, 1).astype(_F32)
    large = max_exact + (jnp.log(nf / max_exact) / math.log(MAX_DISTANCE / max_exact)
                         * (half - max_exact)).astype(jnp.int32)
    large = jnp.minimum(large, half - 1)
    return ret + jnp.where(n < max_exact, n, large)


def _near_offsets(tq, tk):
    unit = math.gcd(tq, tk)
    first = -((tk + MAX_DISTANCE - 2) // unit)
    last = (tq + MAX_DISTANCE - 2) // unit
    return unit, first, last


def _bias_tiles(rel_bias, tq, tk):
    unit, u_min, u_max = _near_offsets(tq, tk)
    n_near = u_max - u_min + 1
    c = jnp.arange(tk, dtype=jnp.int32)[:, None]
    r = jnp.arange(tq, dtype=jnp.int32)[None, :]
    u = jnp.arange(u_min, u_max + 1, dtype=jnp.int32)[:, None, None]
    table = rel_bias.astype(_F32) * LOG2E
    bucket = _t5_bucket(u * unit + c - r)[:, None]
    tiles = jnp.zeros((n_near, DA_HEADS, tk, tq), _F32)
    for b in range(N_BUCKETS):
        tiles = jnp.where(bucket == b, table[b][None, :, None, None], tiles)
    far = table[_t5_bucket(jnp.array([-MAX_DISTANCE, MAX_DISTANCE], jnp.int32))]
    consts = jnp.concatenate([far, jnp.max(table, axis=0, keepdims=True)], axis=0)
    return tiles, consts.T.reshape(-1)


def _diff_attention(p3, vt, lam, subln, bias, far, layer, tq, tk):
    b, s, _ = p3.shape
    vrows = vt.shape[2]
    assert s % (2 * tk) == 0 or (tk == s and s % (2 * tq) == 0)
    lam_init = 0.8 - 0.6 * math.exp(-0.3 * layer)
    kern = functools.partial(_attn_kernel, tq=tq, tk=tk, out_scale=1.0 - lam_init)
    nh = ATTN_HEADS_PER_STEP
    groups = DA_HEADS // nh
    return pl.pallas_call(
        kern,
        grid=(b, groups),
        in_specs=[pl.BlockSpec(memory_space=pltpu.SMEM),
                  pl.BlockSpec(memory_space=pltpu.SMEM),
                  pl.BlockSpec((1, s, nh * DA_W), lambda bi, h: (bi, 0, h),
                               pipeline_mode=pl.Buffered(1)),
                  pl.BlockSpec((1, s, nh * DA_W), lambda bi, h: (bi, 0, groups + h),
                               pipeline_mode=pl.Buffered(1)),
                  pl.BlockSpec((1, nh, vrows, s), lambda bi, h: (bi, h, 0, 0),
                               pipeline_mode=pl.Buffered(1)),
                  pl.BlockSpec((bias.shape[0], nh) + bias.shape[2:], lambda bi, h: (0, h, 0, 0)),
                  pl.BlockSpec((1, DA_W), lambda bi, h: (0, 0))],
        out_specs=pl.BlockSpec((1, s, nh * DA_W), lambda bi, h: (bi, 0, h)),
        out_shape=jax.ShapeDtypeStruct((b, s, MIX_W), _BF),
        scratch_shapes=[pltpu.VMEM((2 * nh, tk, tq), _F32), pltpu.VMEM((2 * nh, tk, tq), _F32),
                        pltpu.VMEM((2 * nh, vrows, tq), _F32)],
        compiler_params=_params("parallel", "parallel"),
        name="diff_attn",
    )(lam, far, p3, p3, vt, bias, subln)


def _hgrn_chunk_stages(z, q, v, lb, st, tri, rev):
    c_rows = z.shape[0]
    e = jnp.exp(-jnp.abs(z))
    r = 1.0 / (1.0 + e)
    sig_pos, sig_neg = jnp.where(z >= 0, r, e * r), jnp.where(z >= 0, e * r, r)
    f = lb + (1.0 - lb) * sig_pos
    kk = (1.0 - lb) * sig_neg
    logf = jnp.log(f)
    hi = logf.astype(_BF)
    lo = (logf - hi.astype(_F32)).astype(_BF)
    cum = (jnp.dot(tri, hi, preferred_element_type=_F32)
           + jnp.dot(tri, lo, preferred_element_type=_F32))
    qh = _silu(q) * (HG_DK ** -0.5)
    vb = v.astype(_BF)
    yield None
    total = cum[0:1, :] if rev else cum[c_rows - 1:c_rows, :]
    o_inter = lax.dot_general((qh * jnp.exp(cum)).astype(_BF), st.astype(_BF), _NT,
                              preferred_element_type=_F32)
    ks = (kk * jnp.exp(total - cum)).astype(_BF)
    st_new = st * jnp.exp(total) + jnp.dot(v.T.astype(_BF), ks, preferred_element_type=_F32)
    blocks = []
    for b in range(c_rows // HG_SUB):
        lo_r, hi_r = b * HG_SUB, (b + 1) * HG_SUB
        if rev:
            cols = slice(lo_r, c_rows)
            base = cum[hi_r:hi_r + 1, :] if hi_r < c_rows else jnp.zeros_like(total)
        else:
            cols = slice(0, hi_r)
            base = cum[lo_r - 1:lo_r, :] if b else jnp.zeros_like(total)
        qq = (qh[lo_r:hi_r] * jnp.exp(cum[lo_r:hi_r] - base)).astype(_BF)
        kt = (kk[cols] * jnp.exp(base - cum[cols])).astype(_BF)
        a = lax.dot_general(qq, kt, _NT, preferred_element_type=_F32)
        blocks.append((lo_r, hi_r, cols, a))
    yield None
    parts = []
    for lo_r, hi_r, cols, a in blocks:
        sub_row = lax.broadcasted_iota(jnp.int32, a.shape, 0) + lo_r
        sub_col = lax.broadcasted_iota(jnp.int32, a.shape, 1) + cols.start
        seen = (sub_col >= sub_row) if rev else (sub_col <= sub_row)
        a = jnp.where(seen, a, 0.0)
        parts.append(o_inter[lo_r:hi_r]
                     + jnp.dot(a.astype(_BF), vb[cols], preferred_element_type=_F32))
    yield jnp.concatenate(parts, axis=0), st_new


def _hgrn_kernel(qf_ref, vf_ref, zf_ref, qb_ref, vb_ref, zb_ref, lb_ref, of_ref, ob_ref, st_ref,
                 *, rows):
    @pl.when(pl.program_id(1) == 0)
    def _():
        st_ref[...] = jnp.zeros_like(st_ref)

    c_rows = HG_CHUNK
    n_chunks = rows // c_rows
    row = lax.broadcasted_iota(jnp.int32, (c_rows, c_rows), 0)
    col = lax.broadcasted_iota(jnp.int32, (c_rows, c_rows), 1)
    tris = ((col <= row).astype(_BF), (col >= row).astype(_BF))
    dirs = ((qf_ref, vf_ref, zf_ref, of_ref), (qb_ref, vb_ref, zb_ref, ob_ref))

    def chunk(c, carry):
        starts = (pl.multiple_of(c * c_rows, c_rows),
                  pl.multiple_of((n_chunks - 1 - c) * c_rows, c_rows))
        for h0 in range(0, HG_HEADS, HG_HEAD_GROUP):
            chains = []
            for h in range(h0, h0 + HG_HEAD_GROUP):
                cs = slice(h * HG_DK, (h + 1) * HG_DK)
                for d, (q_ref, v_ref, z_ref, o_ref) in enumerate(dirs):
                    rs = pl.ds(starts[d], c_rows)
                    gen = _hgrn_chunk_stages(z_ref[0, rs, cs], q_ref[0, rs, cs], v_ref[0, rs, cs],
                                             lb_ref[d, :, cs], st_ref[d, h], tris[d], rev=bool(d))
                    chains.append((gen, o_ref, rs, cs, d, h))
            for _ in range(2):
                for chain in chains:
                    next(chain[0])
            for gen, o_ref, rs, cs, d, h in chains:
                o, st = next(gen)
                o_ref[0, rs, cs] = o
                st_ref[d, h] = st
        return carry

    lax.fori_loop(0, n_chunks, chunk, 0)


def _hgrn_scan(p3, lb, rows):
    b, s, _ = p3.shape
    w = MIX_W
    t = s // rows
    fwd = lambda c: pl.BlockSpec((1, rows, w), lambda bi, i: (bi, i, c))
    bwd = lambda c: pl.BlockSpec((1, rows, w), lambda bi, i: (bi, t - 1 - i, c))
    out = jax.ShapeDtypeStruct((b, s, w), _F32)
    return pl.pallas_call(
        functools.partial(_hgrn_kernel, rows=rows),
        grid=(b, t),
        in_specs=[fwd(0), fwd(1), fwd(3), bwd(0), bwd(1), bwd(4),
                  pl.BlockSpec((2, 1, w), lambda bi, i: (0, 0, 0))],
        out_specs=[fwd(0), bwd(0)],
        out_shape=[out, out],
        scratch_shapes=[pltpu.VMEM((2, HG_HEADS, HG_DK, HG_DK), _F32)],
        compiler_params=_params("parallel", "arbitrary"),
        name="hgrn_scan",
    )(p3, p3, p3, p3, p3, p3, lb)


def _hgrn_finish_kernel(of_ref, ob_ref, g_ref, w_ref, o_ref):
    o = of_ref[...] + ob_ref[...]
    gate = _silu(g_ref[...])
    w = w_ref[...]
    for h in range(HG_HEADS):
        cs = slice(h * HG_DK, (h + 1) * HG_DK)
        oh = o[:, cs]
        ms = jnp.mean(oh * oh, axis=-1, keepdims=True)
        o_ref[:, cs] = (oh * lax.rsqrt(ms + LN_EPS) * w[:, cs] * gate[:, cs]).astype(o_ref.dtype)


def _hgrn_finish(o_fw, o_bw, p2, g_col, norm_w, tm):
    n = o_fw.shape[0]
    return pl.pallas_call(
        _hgrn_finish_kernel,
        grid=(n // tm,),
        in_specs=[pl.BlockSpec((tm, MIX_W), lambda i: (i, 0)),
                  pl.BlockSpec((tm, MIX_W), lambda i: (i, 0)),
                  pl.BlockSpec((tm, MIX_W), lambda i: (i, g_col)),
                  pl.BlockSpec((1, MIX_W), lambda i: (0, 0))],
        out_specs=pl.BlockSpec((tm, MIX_W), lambda i: (i, 0)),
        out_shape=jax.ShapeDtypeStruct((n, MIX_W), _BF),
        compiler_params=_params("parallel"),
        name="hgrn_finish",
    )(o_fw, o_bw, p2, norm_w)


def _sconv_kernel(gb_ref, gc_ref, h_ref, gcp_ref, hp_ref, gcn_ref, hn_ref, w_ref, o_ref, *, halo):
    i = pl.program_id(1)
    u = gc_ref[0].astype(_F32) * h_ref[0].astype(_F32)
    n = u.shape[0]
    u_prev = (gcp_ref[0].astype(_F32) * hp_ref[0].astype(_F32))[halo - 1:halo]
    u_next = (gcn_ref[0].astype(_F32) * hn_ref[0].astype(_F32))[0:1]
    u_prev = jnp.where(i == 0, 0.0, u_prev)
    u_next = jnp.where(i == pl.num_programs(1) - 1, 0.0, u_next)
    row = lax.broadcasted_iota(jnp.int32, u.shape, 0)
    down = jnp.where(row == 0, u_prev, pltpu.roll(u, 1, 0))
    up = jnp.where(row == n - 1, u_next, pltpu.roll(u, n - 1, 0))
    w = w_ref[...]
    y = down * w[0:1] + u * w[1:2] + up * w[2:3]
    o_ref[0] = (gb_ref[0].astype(_F32) * y).astype(o_ref.dtype)


def _short_conv(p3, conv_w, ts):
    b, s, _ = p3.shape
    halo = SUBLANES_BF16
    hb = ts // halo
    last = s // halo - 1
    main = lambda c: pl.BlockSpec((1, ts, MIX_W), lambda bi, i: (bi, i, c))
    prev = lambda c: pl.BlockSpec((1, halo, MIX_W), lambda bi, i: (bi, jnp.maximum(i * hb - 1, 0), c))
    nxt = lambda c: pl.BlockSpec((1, halo, MIX_W), lambda bi, i: (bi, jnp.minimum((i + 1) * hb, last), c))
    return pl.pallas_call(
        functools.partial(_sconv_kernel, halo=halo),
        grid=(b, s // ts),
        in_specs=[main(0), main(1), main(2), prev(1), prev(2), nxt(1), nxt(2),
                  pl.BlockSpec((3, MIX_W), lambda bi, i: (0, 0))],
        out_specs=pl.BlockSpec((1, ts, MIX_W), lambda bi, i: (bi, i, 0)),
        out_shape=jax.ShapeDtypeStruct((b, s, MIX_W), _BF),
        compiler_params=_params("parallel", "parallel"),
        name="short_conv",
    )(p3, p3, p3, p3, p3, p3, p3, conv_w)


def _tile(n, pref):
    return pref if n % pref == 0 else n


def kernel(x, mem, rel_bias, attn_w_in, attn_lambda, attn_subln, hgrn_w_in, hgrn_lower_bound,
           hgrn_norm, conv_w_in, conv_w, mem_w_kv, w_o, ln_gain, ln_bias, ffn_w_up, ffn_conv,
           ffn_w_down):
    b, s, d = x.shape
    n = b * s
    m = mem.shape[1]
    tm = _tile(s, ROW_TILE)
    hg_rows = _tile(s, HG_ROWS)

    bias, far = _bias_tiles(rel_bias, ATTN_Q_TILE, ATTN_ROW_CHUNK)
    mem2 = mem.reshape(b * m, d)
    lbw = jax.nn.softmax(hgrn_lower_bound.astype(_F32), axis=0)
    lb_all = jnp.cumsum(lbw, axis=0) - lbw[0]

    for layer in range(DEPTH):
        kind, j = layer % N_MIXERS, layer // N_MIXERS
        x2 = x.reshape(n, d)
        kv = _matmul(mem2, mem_w_kv[layer].astype(_BF), _BF, _tile(b * m, ROW_TILE))
        kv = kv.reshape(b, m, 2 * XATTN_W)
        if kind == 0:
            q_scale = jnp.where(jnp.arange(attn_w_in.shape[2]) < MIX_W, DA_HD ** -0.5 * LOG2E, 1.0)
            w_in = (attn_w_in[j] * q_scale).astype(_BF)
            p = _matmul(x2, w_in, _BF, tm).reshape(b, s, -1)
            lp = attn_lambda[j].astype(_F32)
            lam_init = 0.8 - 0.6 * math.exp(-0.3 * layer)
            lam = jnp.exp(jnp.sum(lp[0] * lp[1])) - jnp.exp(jnp.sum(lp[2] * lp[3])) + lam_init
            vt = jnp.swapaxes(p[..., 2 * MIX_W:3 * MIX_W], 1, 2).reshape(b, DA_HEADS, DA_W, s)
            vt = jnp.concatenate([vt, jnp.ones((b, DA_HEADS, SUBLANES_BF16, s), _BF)], axis=2)
            mixed = _diff_attention(p, vt, lam.reshape(1),
                                    attn_subln[j].astype(_F32).reshape(1, DA_W),
                                    bias, far, layer, ATTN_Q_TILE, ATTN_K_TILE)
            q_col = 3 * MIX_W // XATTN_W
        elif kind == 1:
            w_in = hgrn_w_in[j].astype(_BF)
            p = _matmul(x2, w_in, _F32, tm).reshape(b, s, -1)
            o_fw, o_bw = _hgrn_scan(p, lb_all[layer].reshape(2, 1, MIX_W), hg_rows)
            mixed = _hgrn_finish(o_fw.reshape(n, MIX_W), o_bw.reshape(n, MIX_W),
                                 p.reshape(n, -1), 2,
                                 hgrn_norm[j].astype(_F32).reshape(1, MIX_W), tm)
            q_col = 5 * MIX_W // XATTN_W
        else:
            w_in = conv_w_in[j].astype(_BF)
            p = _matmul(x2, w_in, _BF, tm).reshape(b, s, -1)
            mixed = _short_conv(p, conv_w[j].astype(_F32), tm)
            q_col = 3 * MIX_W // XATTN_W
        x2 = _out_ln(x2, mixed.reshape(n, MIX_W), p, kv, q_col,
                     w_o[layer].astype(_BF), ln_gain[layer, 0].reshape(1, d),
                     ln_bias[layer, 0].reshape(1, d), tm)
        x = _conv_ffn_ln(x2.reshape(b, s, d), ffn_w_up[layer].astype(_BF),
                         ffn_conv[layer].astype(_F32), ffn_w_down[layer].astype(_BF),
                         ln_gain[layer, 1].reshape(1, d), ln_bias[layer, 1].reshape(1, d),
                         tm, FFN_TILE)
    return x
```
